```python
import jax, jax.numpy as jnp
from jax import lax
import numpy as np

D_MODEL = 1024
BATCH = 2
SEQ = 8192
DEPTH = 2
DEC_BATCH = 16
DEC_SEQ = 4096
PAST_LEN = 128

GRID_W = 64
Q_BLOCK = 128
EPS = 1e-6
ROPE_THETA = 10000.0
NEG_INF = -1e30
A_HEADS = 8
A_HEAD_DIM = 64
DILATED_CONFIGS = ((128, 1), (512, 4), (2048, 16))
MLA_HEADS = 8
MLA_Q_RANK = 256
MLA_KV_RANK = 128
MLA_NOPE_DIM = 64
MLA_ROPE_DIM = 32
MLA_V_DIM = 64
C_HEADS = 16
C_KV_HEADS = 4
C_HEAD_DIM = 64
A_WIDTH = A_HEADS * A_HEAD_DIM
B_WIDTH = MLA_HEADS * MLA_V_DIM
AB_WIDTH = A_WIDTH + B_WIDTH
SPLIT_AB = list(np.cumsum([A_WIDTH, A_WIDTH, A_WIDTH, MLA_Q_RANK, MLA_KV_RANK, MLA_ROPE_DIM]))
IN_AB = 3 * A_WIDTH + MLA_Q_RANK + MLA_KV_RANK + MLA_ROPE_DIM + AB_WIDTH
C_WIDTH = C_HEADS * C_HEAD_DIM
C_KV_WIDTH = C_KV_HEADS * C_HEAD_DIM
SPLIT_C = list(np.cumsum([C_WIDTH, C_KV_WIDTH, C_KV_WIDTH]))
IN_C = C_WIDTH + 2 * C_KV_WIDTH + C_WIDTH
N_EVEN_LAYERS = (DEPTH + 1) // 2
N_ODD_LAYERS = DEPTH // 2

kernel_name = "hybrid_dilated_mla_axial_gqa_encoder"


def rms_norm(x, g):
    xf = x.astype(jnp.float32)
    y = xf * lax.rsqrt(jnp.mean(xf * xf, axis=-1, keepdims=True) + EPS)
    return (y * g.astype(jnp.float32)).astype(x.dtype)


def rope(x, pos):
    half = x.shape[-1] // 2
    freqs = ROPE_THETA ** (-jnp.arange(half, dtype=jnp.float32) / half)
    ang = pos[:, None] * freqs[None, :]
    cos = jnp.cos(ang)[:, None, :].astype(x.dtype)
    sin = jnp.sin(ang)[:, None, :].astype(x.dtype)
    x1, x2 = x[..., :half], x[..., half:]
    return jnp.concatenate([x1 * cos - x2 * sin, x2 * cos + x1 * sin], axis=-1)


def alibi_slopes(n):
    return 2.0 ** (-8.0 * jnp.arange(1, n + 1, dtype=jnp.float32) / n)


def dilated_window_branch(q, k, v, slopes, window, dilation):
    B, T, H, C = q.shape
    d = dilation
    R = window // (2 * d)
    U = T // d
    nb = -(-U // R)
    Up = nb * R

    def to_classes(x):
        x = x.reshape(B, U, d, H, C)
        return jnp.pad(x, ((0, 0), (0, Up - U), (0, 0), (0, 0), (0, 0)))

    def band(x):
        xp = jnp.pad(to_classes(x), ((0, 0), (R, R), (0, 0), (0, 0), (0, 0)))
        xp = xp.reshape(B, nb + 2, R, d, H, C)
        return jnp.concatenate([xp[:, :-2], xp[:, 1:-1], xp[:, 2:]], axis=2)

    qc = to_classes(q).reshape(B, nb, R, d, H, C)
    kb, vb = band(k), band(v)
    s = jnp.einsum('bnqrhc,bnkrhc->bnrhqk', qc, kb).astype(jnp.float32)
    qi = jnp.arange(R)[:, None]
    ki = jnp.arange(3 * R)[None, :]
    diff = ki - R - qi
    u_key = (jnp.arange(nb)[:, None, None] - 1) * R + ki[None]
    valid = (jnp.abs(diff)[None] <= R) & (u_key >= 0) & (u_key < U)
    dist = (d * jnp.abs(diff)).astype(jnp.float32)
    s = s - slopes[:, None, None] * dist[None]
    s = jnp.where(valid[None, :, None, None], s, NEG_INF)
    m = jnp.max(s, axis=-1)
    p = jnp.exp(s - m[..., None])
    z = jnp.sum(p, axis=-1)
    o = jnp.einsum('bnrhqk,bnkrhc->bnqrhc', p, vb.astype(jnp.float32))
    m = jnp.transpose(m, (0, 1, 4, 2, 3)).reshape(B, Up * d, H)[:, :T]
    z = jnp.transpose(z, (0, 1, 4, 2, 3)).reshape(B, Up * d, H)[:, :T]
    o = o.reshape(B, Up * d, H, C)[:, :T]
    return m, z, o


def block_attention(q, k, v, scale):
    B, T, KH, G, C = q.shape
    nq = T // Q_BLOCK
    qb = jnp.moveaxis(q.reshape(B, nq, Q_BLOCK, KH, G, C), 1, 0)

    def one(qblk):
        s = jnp.einsum('bqhgc,bkhc->bhgqk', qblk, k).astype(jnp.float32) * scale
        p = jax.nn.softmax(s, axis=-1).astype(v.dtype)
        return jnp.einsum('bhgqk,bkhc->bqhgc', p, v)

    o = lax.map(one, qb)
    return jnp.moveaxis(o, 0, 1).reshape(B, T, KH * G, v.shape[-1])


def even_layer(x, g_pre, g_post, w_in, q_a_norm, w_uq, kv_a_norm, w_ukv, w_out):
    B, T, _ = x.shape
    h = rms_norm(x, g_pre)
    proj = jnp.einsum('btd,de->bte', h, w_in)
    qa, ka, va, cq, ckv, kr, gate = jnp.split(proj, SPLIT_AB, axis=-1)
    shp = (B, T, A_HEADS, A_HEAD_DIM)
    qa = qa.reshape(shp) * (A_HEAD_DIM ** -0.5)
    ka = ka.reshape(shp)
    va = va.reshape(shp)
    slopes = alibi_slopes(A_HEADS)
    stats = [dilated_window_branch(qa, ka, va, slopes, wdw, dil) for wdw, dil in DILATED_CONFIGS]
    m_all = jnp.stack([st[0] for st in stats])
    z_all = jnp.stack([st[1] for st in stats])
    o_all = jnp.stack([st[2] for st in stats])
    wts = jnp.exp(m_all - jnp.max(m_all, axis=0, keepdims=True))
    z = jnp.sum(z_all * wts, axis=0)
    o_a = jnp.sum(o_all * wts[..., None], axis=0) / z[..., None]
    out_a = o_a.astype(x.dtype).reshape(B, T, A_WIDTH)
    pos = jnp.arange(T, dtype=jnp.float32)
    cq = rms_norm(cq, q_a_norm)
    qb = jnp.einsum('btr,re->bte', cq, w_uq).reshape(B, T, MLA_HEADS, MLA_NOPE_DIM + MLA_ROPE_DIM)
    q_nope, q_rope = qb[..., :MLA_NOPE_DIM], qb[..., MLA_NOPE_DIM:]
    q_b = jnp.concatenate([q_nope, rope(q_rope, pos)], axis=-1)
    ckv = rms_norm(ckv, kv_a_norm)
    kv = jnp.einsum('btr,re->bte', ckv, w_ukv).reshape(B, T, MLA_HEADS, MLA_NOPE_DIM + MLA_V_DIM)
    k_nope, v_b = kv[..., :MLA_NOPE_DIM], kv[..., MLA_NOPE_DIM:]
    k_rope = jnp.broadcast_to(rope(kr[:, :, None, :], pos), (B, T, MLA_HEADS, MLA_ROPE_DIM))
    k_b = jnp.concatenate([k_nope, k_rope], axis=-1)
    out_b = block_attention(q_b[:, :, :, None, :], k_b, v_b,
                            (MLA_NOPE_DIM + MLA_ROPE_DIM) ** -0.5).reshape(B, T, B_WIDTH)
    y = jnp.concatenate([out_a, out_b], axis=-1) * jax.nn.silu(gate)
    y = jnp.einsum('bte,ed->btd', y, w_out)
    return x + rms_norm(y, g_post)


def odd_layer(x, g_pre, g_post, w_in, q_norm, k_norm, w_out):
    B, T, _ = x.shape
    n_rows = T // GRID_W
    rows = jnp.repeat(jnp.arange(n_rows, dtype=jnp.float32), GRID_W)
    cols = jnp.tile(jnp.arange(GRID_W, dtype=jnp.float32), n_rows)
    half = C_HEAD_DIM // 2
    h = rms_norm(x, g_pre)
    proj = jnp.einsum('btd,de->bte', h, w_in)
    q, k, v, gate = jnp.split(proj, SPLIT_C, axis=-1)
    q = rms_norm(q.reshape(B, T, C_HEADS, C_HEAD_DIM), q_norm)
    k = rms_norm(k.reshape(B, T, C_KV_HEADS, C_HEAD_DIM), k_norm)
    v = v.reshape(B, T, C_KV_HEADS, C_HEAD_DIM)
    q = jnp.concatenate([rope(q[..., :half], rows), rope(q[..., half:], cols)], axis=-1)
    k = jnp.concatenate([rope(k[..., :half], rows), rope(k[..., half:], cols)], axis=-1)
    q = q.reshape(B, T, C_KV_HEADS, C_HEADS // C_KV_HEADS, C_HEAD_DIM)
    o = block_attention(q, k, v, C_HEAD_DIM ** -0.5).reshape(B, T, C_WIDTH)
    y = jnp.einsum('bte,ed->btd', o * jax.nn.silu(gate), w_out)
    return x + rms_norm(y, g_post)


def trunk(x, norm_pre, norm_post, w_in_ab, mla_q_norm, w_uq, mla_kv_norm, w_ukv, w_out_ab,
          w_in_c, c_q_norm, c_k_norm, w_out_c):
    for layer in range(DEPTH):
        i = layer // 2
        if layer % 2 == 0:
            x = even_layer(x, norm_pre[layer], norm_post[layer], w_in_ab[i], mla_q_norm[i], w_uq[i],
                           mla_kv_norm[i], w_ukv[i], w_out_ab[i])
        else:
            x = odd_layer(x, norm_pre[layer], norm_post[layer], w_in_c[i], c_q_norm[i], c_k_norm[i], w_out_c[i])
    return x


def setup_inputs(seed: int = 0) -> dict:
    key = jax.random.key(seed)
    ks = jax.random.split(key, 16)

    def w(k, shape, fan_in):
        return jax.random.normal(k, shape, jnp.float32) * (fan_in ** -0.5)

    def gain(k, shape):
        return 1.0 + 0.05 * jax.random.normal(k, shape, jnp.float32)

    return {
        "x_prompt": jax.random.normal(ks[0], (BATCH, SEQ, D_MODEL), jnp.float32),
        "x_sample": jax.random.normal(ks[1], (DEC_BATCH, DEC_SEQ, D_MODEL), jnp.float32),
        "norm_pre": gain(ks[2], (DEPTH, D_MODEL)),
        "norm_post": gain(ks[3], (DEPTH, D_MODEL)),
        "w_in_ab": w(ks[4], (N_EVEN_LAYERS, D_MODEL, IN_AB), D_MODEL),
        "mla_q_norm": gain(ks[5], (N_EVEN_LAYERS, MLA_Q_RANK)),
        "w_uq": w(ks[6], (N_EVEN_LAYERS, MLA_Q_RANK, MLA_HEADS * (MLA_NOPE_DIM + MLA_ROPE_DIM)), MLA_Q_RANK),
        "mla_kv_norm": gain(ks[7], (N_EVEN_LAYERS, MLA_KV_RANK)),
        "w_ukv": w(ks[8], (N_EVEN_LAYERS, MLA_KV_RANK, MLA_HEADS * (MLA_NOPE_DIM + MLA_V_DIM)), MLA_KV_RANK),
        "w_out_ab": w(ks[9], (N_EVEN_LAYERS, AB_WIDTH, D_MODEL), AB_WIDTH),
        "w_in_c": w(ks[10], (N_ODD_LAYERS, D_MODEL, IN_C), D_MODEL),
        "c_q_norm": gain(ks[11], (N_ODD_LAYERS, C_HEAD_DIM)),
        "c_k_norm": gain(ks[12], (N_ODD_LAYERS, C_HEAD_DIM)),
        "w_out_c": w(ks[13], (N_ODD_LAYERS, C_WIDTH, D_MODEL), C_WIDTH),
    }


def reference(x_prompt, x_sample, norm_pre, norm_post, w_in_ab, mla_q_norm, w_uq, mla_kv_norm, w_ukv,
              w_out_ab, w_in_c, c_q_norm, c_k_norm, w_out_c):
    y_prompt = trunk(x_prompt, norm_pre, norm_post, w_in_ab, mla_q_norm, w_uq, mla_kv_norm, w_ukv,
                     w_out_ab, w_in_c, c_q_norm, c_k_norm, w_out_c)
    y_sample = trunk(x_sample, norm_pre, norm_post, w_in_ab, mla_q_norm, w_uq, mla_kv_norm, w_ukv,
                     w_out_ab, w_in_c, c_q_norm, c_k_norm, w_out_c)
    return (y_prompt, y_sample)
```

```python
import functools
import math

import numpy as np
import jax
import jax.numpy as jnp
from jax import lax
from jax.experimental import pallas as pl
from jax.experimental.pallas import tpu as pltpu

F32 = jnp.float32
BF16 = jnp.bfloat16

D_MODEL = 1024
EPS = 1e-6
ROPE_THETA = 10000.0
NEG_INF = -1e30
LOG2E = math.log2(math.e)
GRID_W = 64
A_HEADS = 8
A_HEAD_DIM = 64
DILATED_CONFIGS = ((128, 1), (512, 4), (2048, 16))
A_REACH = max(w // 2 for w, _ in DILATED_CONFIGS)
MLA_HEADS = 8
MLA_Q_RANK = 256
MLA_KV_RANK = 128
MLA_NOPE_DIM = 64
MLA_ROPE_DIM = 32
MLA_V_DIM = 64
MLA_PAD_DIM = 128
C_HEADS = 16
C_KV_HEADS = 4
C_HEAD_DIM = 64
A_WIDTH = A_HEADS * A_HEAD_DIM
B_WIDTH = MLA_HEADS * MLA_V_DIM
AB_WIDTH = A_WIDTH + B_WIDTH
C_WIDTH = C_HEADS * C_HEAD_DIM
C_KV_WIDTH = C_KV_HEADS * C_HEAD_DIM
COL_QA, COL_KA, COL_VA = 0, 512, 1024
COL_CQ, COL_CKV, COL_KR, COL_GATE_AB = 1536, 1792, 1920, 2048
IN_AB_PAD = 3072
KR_LANE = MLA_NOPE_DIM
COL_GATE_C, COL_QC, COL_KC, COL_VC = 0, 1024, 2048, 2304
IN_C = 2560

VMEM_LIMIT = 48 * 1024 * 1024
ROW_TILE = 512


def _params(sem):
    return pltpu.CompilerParams(dimension_semantics=sem, vmem_limit_bytes=VMEM_LIMIT)


def _norm_proj_kernel(x_ref, g_ref, w_ref, o_ref, *, n_chunk):
    x = x_ref[...]
    ms = jnp.mean(x * x, axis=-1, keepdims=True)
    h = (x * lax.rsqrt(ms + EPS) * g_ref[...]).astype(BF16)
    n = o_ref.shape[-1]
    for c in range(0, n, n_chunk):
        o_ref[:, c:c + n_chunk] = jnp.dot(
            h, w_ref[:, c:c + n_chunk], preferred_element_type=F32).astype(BF16)


def norm_proj(x, g, w):
    m, d = x.shape
    n = w.shape[1]
    return pl.pallas_call(
        functools.partial(_norm_proj_kernel, n_chunk=512),
        grid=(m // ROW_TILE,),
        in_specs=[pl.BlockSpec((ROW_TILE, d), lambda i: (i, 0)),
                  pl.BlockSpec((1, d), lambda i: (0, 0)),
                  pl.BlockSpec((d, n), lambda i: (0, 0))],
        out_specs=pl.BlockSpec((ROW_TILE, n), lambda i: (i, 0)),
        out_shape=jax.ShapeDtypeStruct((m, n), BF16),
        compiler_params=_params(("parallel",)),
        name="norm_proj",
    )(x, g.reshape(1, d), w)


def _rope_lanes(x, c, sa, sb):
    return x * c + pltpu.roll(x, 16, 1) * sa + pltpu.roll(x, 112, 1) * sb


def _mla_prep_kernel(cq_ref, ckv_ref, kr_ref, gq_ref, gkv_ref, wuq_ref, wuk_ref, wuv_ref,
                     c_ref, sa_ref, sb_ref, q_ref, k_ref, v_ref, *, q_scale):
    def rms(x, g):
        ms = jnp.mean(x * x, axis=-1, keepdims=True)
        return (x * lax.rsqrt(ms + EPS) * g).astype(BF16)

    cq = rms(cq_ref[...].astype(F32), gq_ref[...])
    ckv = rms(ckv_ref[...].astype(F32), gkv_ref[...])
    c, sa, sb = c_ref[...], sa_ref[...], sb_ref[...]
    kr = _rope_lanes(kr_ref[...].astype(F32), c, sa, sb)
    qf = jnp.dot(cq, wuq_ref[...], preferred_element_type=F32)
    kf = jnp.dot(ckv, wuk_ref[...], preferred_element_type=F32)
    v_ref[...] = jnp.dot(ckv, wuv_ref[...], preferred_element_type=F32).astype(BF16)
    for h in range(MLA_HEADS):
        sl = slice(h * MLA_PAD_DIM, (h + 1) * MLA_PAD_DIM)
        q_ref[:, sl] = (_rope_lanes(qf[:, sl], c, sa, sb) * q_scale).astype(BF16)
        k_ref[:, sl] = (kf[:, sl] + kr).astype(BF16)


def mla_prep(proj, seq, gq, gkv, wuq, wuk, wuv, tabs):
    m = proj.shape[0]
    tiles_per_seq = seq // ROW_TILE
    c, sa, sb = tabs
    row = lambda i: (i, 0)
    const = lambda i: (0, 0)
    pos = lambda i: (i % tiles_per_seq, 0)
    wide = MLA_HEADS * MLA_PAD_DIM
    return pl.pallas_call(
        functools.partial(_mla_prep_kernel,
                          q_scale=(MLA_NOPE_DIM + MLA_ROPE_DIM) ** -0.5 * LOG2E),
        grid=(m // ROW_TILE,),
        in_specs=[pl.BlockSpec((ROW_TILE, MLA_Q_RANK), lambda i: (i, COL_CQ // MLA_Q_RANK)),
                  pl.BlockSpec((ROW_TILE, MLA_KV_RANK), lambda i: (i, COL_CKV // MLA_KV_RANK)),
                  pl.BlockSpec((ROW_TILE, 128), lambda i: (i, COL_KR // 128)),
                  pl.BlockSpec((1, MLA_Q_RANK), const),
                  pl.BlockSpec((1, MLA_KV_RANK), const),
                  pl.BlockSpec(wuq.shape, const),
                  pl.BlockSpec(wuk.shape, const),
                  pl.BlockSpec(wuv.shape, const),
                  pl.BlockSpec((ROW_TILE, 128), pos),
                  pl.BlockSpec((ROW_TILE, 128), pos),
                  pl.BlockSpec((ROW_TILE, 128), pos)],
        out_specs=[pl.BlockSpec((ROW_TILE, wide), row),
                   pl.BlockSpec((ROW_TILE, wide), row),
                   pl.BlockSpec((ROW_TILE, B_WIDTH), row)],
        out_shape=[jax.ShapeDtypeStruct((m, wide), BF16),
                   jax.ShapeDtypeStruct((m, wide), BF16),
                   jax.ShapeDtypeStruct((m, B_WIDTH), BF16)],
        compiler_params=_params(("parallel",)),
        name="mla_prep",
    )(proj, proj, proj, gq.reshape(1, -1), gkv.reshape(1, -1), wuq, wuk, wuv, c, sa, sb)


def _head_prep_kernel(x_ref, g_ref, c_ref, s_ref, p_ref, o_ref, *, scale):
    x = x_ref[0, 0].astype(F32)
    ms = jnp.mean(x * x, axis=-1, keepdims=True)
    xn = x * lax.rsqrt(ms + EPS) * g_ref[...]
    rot = jnp.dot(xn.astype(BF16), p_ref[...], preferred_element_type=F32)
    o_ref[0, 0] = ((xn * c_ref[...] + rot * s_ref[...]) * scale).astype(BF16)


def head_prep(x, g, c, s, p, scale):
    b, h, t, d = x.shape
    blk = pl.BlockSpec((1, 1, ROW_TILE, d), lambda i, j, k: (i, j, k, 0))
    tab = pl.BlockSpec((ROW_TILE, d), lambda i, j, k: (k, 0))
    const = lambda i, j, k: (0, 0)
    return pl.pallas_call(
        functools.partial(_head_prep_kernel, scale=scale),
        grid=(b, h, t // ROW_TILE),
        in_specs=[blk, pl.BlockSpec((1, d), const), tab, tab, pl.BlockSpec((d, d), const)],
        out_specs=blk,
        out_shape=jax.ShapeDtypeStruct(x.shape, BF16),
        compiler_params=_params(("parallel", "parallel", "parallel")),
        name="head_prep",
    )(x, g.reshape(1, d), c, s, p)


def _flash_kernel(q_ref, k_ref, v_ref, o_ref, m_ref, l_ref, acc_ref, *, tk):
    g, tq, d = q_ref.shape[2:]
    rows = g * tq
    q = q_ref[0, 0].reshape(rows, d)
    m_ref[...] = jnp.full(m_ref.shape, NEG_INF, F32)
    l_ref[...] = jnp.zeros(l_ref.shape, F32)
    acc_ref[...] = jnp.zeros(acc_ref.shape, F32)

    def body(j, carry):
        start = pl.multiple_of(j * tk, tk)
        k = k_ref[0, 0, pl.ds(start, tk), :]
        v = v_ref[0, 0, pl.ds(start, tk), :]
        s = lax.dot_general(q, k, (((1,), (1,)), ((), ())), preferred_element_type=F32)
        m_old = m_ref[...]
        m_new = jnp.maximum(m_old, jnp.max(s, axis=-1, keepdims=True))
        alpha = jnp.exp2(m_old - m_new)
        p = jnp.exp2(s - m_new)
        l_ref[...] = alpha * l_ref[...] + jnp.sum(p, axis=-1, keepdims=True)
        acc_ref[...] = alpha * acc_ref[...] + jnp.dot(
            p.astype(BF16), v, preferred_element_type=F32)
        m_ref[...] = m_new
        return carry

    lax.fori_loop(0, k_ref.shape[2] // tk, body, 0)
    out = acc_ref[...] / l_ref[...]
    o_ref[0, 0] = out.reshape(g, tq, out.shape[-1]).astype(o_ref.dtype)


def flash_attention(q, k, v, *, tq, tk):
    b, kh, g, t, d = q.shape
    dv = v.shape[-1]
    rows = g * tq
    return pl.pallas_call(
        functools.partial(_flash_kernel, tk=tk),
        grid=(b, kh, t // tq),
        in_specs=[pl.BlockSpec((1, 1, g, tq, d), lambda i, j, n: (i, j, 0, n, 0)),
                  pl.BlockSpec((1, 1, t, d), lambda i, j, n: (i, j, 0, 0)),
                  pl.BlockSpec((1, 1, t, dv), lambda i, j, n: (i, j, 0, 0))],
        out_specs=pl.BlockSpec((1, 1, g, tq, dv), lambda i, j, n: (i, j, 0, n, 0)),
        out_shape=jax.ShapeDtypeStruct((b, kh, g, t, dv), BF16),
        scratch_shapes=[pltpu.VMEM((rows, 1), F32), pltpu.VMEM((rows, 1), F32),
                        pltpu.VMEM((rows, dv), F32)],
        compiler_params=_params(("parallel", "parallel", "arbitrary")),
        name="flash_attention",
    )(q, k, v)


A_TQ = 256
A_WIN = A_TQ + 2 * A_REACH


def _dilated_kernel(q_ref, k_ref, v_ref, b_ref, o_ref):
    start = pl.multiple_of(pl.program_id(2) * A_TQ, A_TQ)
    k = k_ref[0, 0, pl.ds(start, A_WIN), :]
    v = v_ref[0, 0, pl.ds(start, A_WIN), :]
    s = lax.dot_general(q_ref[0, 0], k, (((1,), (1,)), ((), ())),
                        preferred_element_type=F32) + b_ref[0]
    m = jnp.max(s, axis=-1, keepdims=True)
    p = jnp.exp2(s - m)
    l = jnp.sum(p, axis=-1, keepdims=True)
    o = jnp.dot(p.astype(BF16), v, preferred_element_type=F32)
    o_ref[0, 0] = (o / l).astype(o_ref.dtype)


def dilated_attention(q, k, v, bias):
    b, h, t, d = q.shape
    tp = k.shape[2]
    dv = v.shape[-1]
    return pl.pallas_call(
        _dilated_kernel,
        grid=(h, b, t // A_TQ),
        in_specs=[pl.BlockSpec((1, 1, A_TQ, d), lambda j, i, n: (i, j, n, 0)),
                  pl.BlockSpec((1, 1, tp, d), lambda j, i, n: (i, j, 0, 0)),
                  pl.BlockSpec((1, 1, tp, dv), lambda j, i, n: (i, j, 0, 0)),
                  pl.BlockSpec((1, A_TQ, A_WIN), lambda j, i, n: (j, 0, 0))],
        out_specs=pl.BlockSpec((1, 1, A_TQ, dv), lambda j, i, n: (i, j, n, 0)),
        out_shape=jax.ShapeDtypeStruct((b, h, t, dv), BF16),
        compiler_params=_params(("parallel", "parallel", "arbitrary")),
        name="dilated_attention",
    )(q, k, v, bias)


def _dilated_bias():
    i = jnp.arange(A_TQ, dtype=jnp.int32)[:, None]
    c = jnp.arange(A_WIN, dtype=jnp.int32)[None, :]
    delta = c - i - A_REACH
    dist = jnp.abs(delta)
    mult = jnp.zeros(delta.shape, F32)
    for window, dil in DILATED_CONFIGS:
        mult = mult + ((dist <= window // 2) & (delta % dil == 0)).astype(F32)
    slopes = 2.0 ** (-8.0 * jnp.arange(1, A_HEADS + 1, dtype=F32) / A_HEADS)
    bias = -slopes[:, None, None] * dist.astype(F32)[None] + jnp.log(jnp.maximum(mult, 1.0))[None]
    return jnp.where((mult > 0)[None], bias * LOG2E, NEG_INF)


def _gate_out_kernel(*refs, n_parts):
    parts = refs[:n_parts]
    gate_ref, x_ref, w_ref, g_ref, o_ref = refs[n_parts:]
    o = jnp.concatenate([r[...] for r in parts], axis=-1) if n_parts > 1 else parts[0][...]
    gate = gate_ref[...].astype(F32)
    y = (o.astype(F32) * (gate * jax.nn.sigmoid(gate))).astype(BF16)
    y = jnp.dot(y, w_ref[...], preferred_element_type=F32)
    ms = jnp.mean(y * y, axis=-1, keepdims=True)
    o_ref[...] = x_ref[...] + y * lax.rsqrt(ms + EPS) * g_ref[...]


def gate_out(parts, proj, gate_col, x, w, g):
    m, d = x.shape
    width = w.shape[0]
    row = lambda i: (i, 0)
    const = lambda i: (0, 0)
    in_specs = [pl.BlockSpec((ROW_TILE, p.shape[1]), row) for p in parts]
    in_specs += [pl.BlockSpec((ROW_TILE, width), lambda i: (i, gate_col // width)),
                 pl.BlockSpec((ROW_TILE, d), row),
                 pl.BlockSpec(w.shape, const),
                 pl.BlockSpec((1, d), const)]
    return pl.pallas_call(
        functools.partial(_gate_out_kernel, n_parts=len(parts)),
        grid=(m // ROW_TILE,),
        in_specs=in_specs,
        out_specs=pl.BlockSpec((ROW_TILE, d), row),
        out_shape=jax.ShapeDtypeStruct((m, d), F32),
        compiler_params=_params(("parallel",)),
        name="gate_out",
    )(*parts, proj, x, w, g.reshape(1, d))


def _rope_freqs():
    half = MLA_ROPE_DIM // 2
    return ROPE_THETA ** (-jnp.arange(half, dtype=F32) / half)


def _mla_tables(seq):
    ang = jnp.arange(seq, dtype=F32)[:, None] * _rope_freqs()[None, :]
    cos, sin = jnp.cos(ang), jnp.sin(ang)
    zeros = jnp.zeros_like(cos)
    ones_nope = jnp.ones((seq, MLA_NOPE_DIM), F32)
    zeros_nope = jnp.zeros((seq, MLA_NOPE_DIM), F32)
    tail = jnp.zeros((seq, MLA_PAD_DIM - MLA_NOPE_DIM - MLA_ROPE_DIM), F32)
    c = jnp.concatenate([ones_nope, cos, cos, tail], axis=1)
    sa = jnp.concatenate([zeros_nope, zeros, sin, tail], axis=1)
    sb = jnp.concatenate([zeros_nope, -sin, zeros, tail], axis=1)
    return c, sa, sb


def _axial_tables(seq):
    t = jnp.arange(seq, dtype=jnp.int32)
    rows = (t // GRID_W).astype(F32)
    cols = (t % GRID_W).astype(F32)
    f = _rope_freqs()[None, :]
    ar, ac = rows[:, None] * f, cols[:, None] * f
    c = jnp.concatenate([jnp.cos(ar), jnp.cos(ar), jnp.cos(ac), jnp.cos(ac)], axis=1)
    s = jnp.concatenate([jnp.sin(ar), jnp.sin(ar), jnp.sin(ac), jnp.sin(ac)], axis=1)
    return c, s


def _rotate_half_matrix():
    p = np.zeros((C_HEAD_DIM, C_HEAD_DIM), np.float32)
    for base in (0, 32):
        for j in range(16):
            p[base + 16 + j, base + j] = -1.0
            p[base + j, base + 16 + j] = 1.0
    return jnp.asarray(p, BF16)


def _arrange_w_in_ab(w):
    qa, ka, va, cq, ckv, kr, gate = jnp.split(
        w, list(np.cumsum([A_WIDTH, A_WIDTH, A_WIDTH, MLA_Q_RANK, MLA_KV_RANK, MLA_ROPE_DIM])), axis=1)
    d = w.shape[0]
    kr_blk = jnp.concatenate([jnp.zeros((d, KR_LANE), F32), kr,
                              jnp.zeros((d, 128 - KR_LANE - MLA_ROPE_DIM), F32)], axis=1)
    qa = qa * (A_HEAD_DIM ** -0.5 * LOG2E)
    return jnp.concatenate([qa, ka, va, cq, ckv, kr_blk, gate], axis=1).astype(BF16)


def _arrange_w_in_c(w):
    qkv, gate = w[..., :C_WIDTH + 2 * C_KV_WIDTH], w[..., C_WIDTH + 2 * C_KV_WIDTH:]
    return jnp.concatenate([gate, qkv], axis=-1).astype(BF16)


def _arrange_w_uq(w):
    r = w.shape[0]
    w = w.reshape(r, MLA_HEADS, MLA_NOPE_DIM + MLA_ROPE_DIM)
    pad = jnp.zeros((r, MLA_HEADS, MLA_PAD_DIM - MLA_NOPE_DIM - MLA_ROPE_DIM), F32)
    return jnp.concatenate([w, pad], axis=2).reshape(r, MLA_HEADS * MLA_PAD_DIM).astype(BF16)


def _arrange_w_ukv(w):
    r = w.shape[0]
    w = w.reshape(r, MLA_HEADS, MLA_NOPE_DIM + MLA_V_DIM)
    pad = jnp.zeros((r, MLA_HEADS, MLA_PAD_DIM - MLA_NOPE_DIM), F32)
    wk = jnp.concatenate([w[:, :, :MLA_NOPE_DIM], pad], axis=2).reshape(r, MLA_HEADS * MLA_PAD_DIM)
    wv = w[:, :, MLA_NOPE_DIM:].reshape(r, B_WIDTH)
    return wk.astype(BF16), wv.astype(BF16)


def _to_heads(x2d, b, t, h):
    return x2d.reshape(b, t, h, x2d.shape[1] // h).transpose(0, 2, 1, 3)


def _from_heads(x):
    b, h, t, c = x.shape
    return x.transpose(0, 2, 1, 3).reshape(b * t, h * c)


def _even_layer(x, b, t, g_pre, g_post, w_in, gq, wuq, gkv, wuk, wuv, w_out, bias, mla_tabs):
    proj = norm_proj(x, g_pre, w_in)
    qa = _to_heads(proj[:, COL_QA:COL_QA + A_WIDTH], b, t, A_HEADS)
    ka = _to_heads(proj[:, COL_KA:COL_KA + A_WIDTH], b, t, A_HEADS)
    va = _to_heads(proj[:, COL_VA:COL_VA + A_WIDTH], b, t, A_HEADS)
    lane_pad = jnp.zeros(qa.shape[:3] + (128 - A_HEAD_DIM - 1,), BF16)
    qa = jnp.concatenate([qa, jnp.ones(qa.shape[:3] + (1,), BF16), lane_pad], axis=-1)
    ka = jnp.concatenate([ka, jnp.zeros(ka.shape[:3] + (128 - A_HEAD_DIM,), BF16)], axis=-1)
    row_pad = jnp.zeros(ka.shape[:2] + (A_REACH, 128), BF16).at[..., A_HEAD_DIM].set(NEG_INF)
    ka = jnp.concatenate([row_pad, ka, row_pad], axis=2)
    va = jnp.pad(va, ((0, 0), (0, 0), (A_REACH, A_REACH), (0, 0)))
    out_a = _from_heads(dilated_attention(qa, ka, va, bias))
    qb, kb, vb = mla_prep(proj, t, gq, gkv, wuq, wuk, wuv, mla_tabs)
    qb = _to_heads(qb, b, t, MLA_HEADS)[:, :, None]
    kb = _to_heads(kb, b, t, MLA_HEADS)
    vb = _to_heads(vb, b, t, MLA_HEADS)
    out_b = flash_attention(qb, kb, vb, tq=512, tk=512)
    out_b = _from_heads(out_b[:, :, 0])
    return gate_out([out_a, out_b], proj, COL_GATE_AB, x, w_out, g_post)


def _odd_layer(x, b, t, g_pre, g_post, w_in, q_norm, k_norm, w_out, axial_tabs, rot):
    proj = norm_proj(x, g_pre, w_in)
    c, s = axial_tabs
    q = _to_heads(proj[:, COL_QC:COL_QC + C_WIDTH], b, t, C_HEADS)
    k = _to_heads(proj[:, COL_KC:COL_KC + C_KV_WIDTH], b, t, C_KV_HEADS)
    v = _to_heads(proj[:, COL_VC:COL_VC + C_KV_WIDTH], b, t, C_KV_HEADS)
    q = head_prep(q, q_norm, c, s, rot, C_HEAD_DIM ** -0.5 * LOG2E)
    k = head_prep(k, k_norm, c, s, rot, 1.0)
    groups = C_HEADS // C_KV_HEADS
    q = q.reshape(b, C_KV_HEADS, groups, t, C_HEAD_DIM)
    o = flash_attention(q, k, v, tq=128, tk=512)
    o = _from_heads(o.reshape(b, C_HEADS, t, C_HEAD_DIM))
    return gate_out([o], proj, COL_GATE_C, x, w_out, g_post)


def _trunk(x3, weights, consts):
    (norm_pre, norm_post, w_in_ab, mla_q_norm, w_uq, mla_kv_norm, w_uk, w_uv, w_out_ab,
     w_in_c, c_q_norm, c_k_norm, w_out_c) = weights
    b, t, d = x3.shape
    x = x3.reshape(b * t, d)
    bias, rot = consts
    mla_tabs = _mla_tables(t)
    axial_tabs = _axial_tables(t)
    depth = norm_pre.shape[0]
    for layer in range(depth):
        i = layer // 2
        if layer % 2 == 0:
            x = _even_layer(x, b, t, norm_pre[layer], norm_post[layer], w_in_ab[i], mla_q_norm[i],
                            w_uq[i], mla_kv_norm[i], w_uk[i], w_uv[i], w_out_ab[i], bias, mla_tabs)
        else:
            x = _odd_layer(x, b, t, norm_pre[layer], norm_post[layer], w_in_c[i], c_q_norm[i],
                           c_k_norm[i], w_out_c[i], axial_tabs, rot)
    return x.reshape(b, t, d)


def kernel(x_prompt, x_sample, norm_pre, norm_post, w_in_ab, mla_q_norm, w_uq, mla_kv_norm, w_ukv,
           w_out_ab, w_in_c, c_q_norm, c_k_norm, w_out_c):
    n_even = w_in_ab.shape[0]
    w_in_ab_p = jnp.stack([_arrange_w_in_ab(w_in_ab[i]) for i in range(n_even)])
    w_uq_p = jnp.stack([_arrange_w_uq(w_uq[i]) for i in range(n_even)])
    w_ukv_p = [_arrange_w_ukv(w_ukv[i]) for i in range(n_even)]
    w_uk_p = jnp.stack([p[0] for p in w_ukv_p])
    w_uv_p = jnp.stack([p[1] for p in w_ukv_p])
    weights = (norm_pre, norm_post, w_in_ab_p, mla_q_norm, w_uq_p, mla_kv_norm, w_uk_p, w_uv_p,
               w_out_ab.astype(BF16), _arrange_w_in_c(w_in_c), c_q_norm, c_k_norm, w_out_c.astype(BF16))
    consts = (_dilated_bias(), _rotate_half_matrix())
    return (_trunk(x_prompt, weights, consts), _trunk(x_sample, weights, consts))
```

```python
import functools
import math

import numpy as np
import jax
import jax.numpy as jnp
from jax import lax
from jax.experimental import pallas as pl
from jax.experimental.pallas import tpu as pltpu

F32 = jnp.float32
BF16 = jnp.bfloat16

D_MODEL = 1024
EPS = 1e-6
ROPE_THETA = 10000.0
NEG_INF = -1e30
LOG2E = math.log2(math.e)
GRID_W = 64
A_HEADS = 8
A_HEAD_DIM = 64
DILATED_CONFIGS = ((128, 1), (512, 4), (2048, 16))
A_REACH = max(w // 2 for w, _ in DILATED_CONFIGS)
MLA_HEADS = 8
MLA_Q_RANK = 256
MLA_KV_RANK = 128
MLA_NOPE_DIM = 64
MLA_ROPE_DIM = 32
MLA_V_DIM = 64
MLA_PAD_DIM = 128
C_HEADS = 16
C_KV_HEADS = 4
C_HEAD_DIM = 64
A_WIDTH = A_HEADS * A_HEAD_DIM
B_WIDTH = MLA_HEADS * MLA_V_DIM
AB_WIDTH = A_WIDTH + B_WIDTH
C_WIDTH = C_HEADS * C_HEAD_DIM
C_KV_WIDTH = C_KV_HEADS * C_HEAD_DIM
COL_QA, COL_KA, COL_VA = 0, 512, 1024
COL_CQ, COL_CKV, COL_KR, COL_GATE_AB = 1536, 1792, 1920, 2048
IN_AB_PAD = 3072
KR_LANE = MLA_NOPE_DIM
COL_GATE_C, COL_QC, COL_KC, COL_VC = 0, 1024, 2048, 2304
IN_C = 2560

VMEM_LIMIT = 48 * 1024 * 1024
ROW_TILE = 512


def _params(sem):
    return pltpu.CompilerParams(dimension_semantics=sem, vmem_limit_bytes=VMEM_LIMIT)


def _norm_proj_kernel(x_ref, g_ref, w_ref, o_ref, *, n_chunk):
    x = x_ref[...]
    ms = jnp.mean(x * x, axis=-1, keepdims=True)
    h = (x * lax.rsqrt(ms + EPS) * g_ref[...]).astype(BF16)
    n = o_ref.shape[-1]
    for c in range(0, n, n_chunk):
        o_ref[:, c:c + n_chunk] = jnp.dot(
            h, w_ref[:, c:c + n_chunk], preferred_element_type=F32).astype(BF16)


def norm_proj(x, g, w):
    m, d = x.shape
    n = w.shape[1]
    return pl.pallas_call(
        functools.partial(_norm_proj_kernel, n_chunk=512),
        grid=(m // ROW_TILE,),
        in_specs=[pl.BlockSpec((ROW_TILE, d), lambda i: (i, 0)),
                  pl.BlockSpec((1, d), lambda i: (0, 0)),
                  pl.BlockSpec((d, n), lambda i: (0, 0))],
        out_specs=pl.BlockSpec((ROW_TILE, n), lambda i: (i, 0)),
        out_shape=jax.ShapeDtypeStruct((m, n), BF16),
        compiler_params=_params(("parallel",)),
        name="norm_proj",
    )(x, g.reshape(1, d), w)


def _rope_lanes(x, c, sa, sb):
    return x * c + pltpu.roll(x, 16, 1) * sa + pltpu.roll(x, 112, 1) * sb


def _mla_prep_kernel(cq_ref, ckv_ref, kr_ref, gq_ref, gkv_ref, wuq_ref, wuk_ref, wuv_ref,
                     c_ref, sa_ref, sb_ref, q_ref, k_ref, v_ref, *, q_scale):
    def rms(x, g):
        ms = jnp.mean(x * x, axis=-1, keepdims=True)
        return (x * lax.rsqrt(ms + EPS) * g).astype(BF16)

    cq = rms(cq_ref[...].astype(F32), gq_ref[...])
    ckv = rms(ckv_ref[...].astype(F32), gkv_ref[...])
    c, sa, sb = c_ref[...], sa_ref[...], sb_ref[...]
    kr = _rope_lanes(kr_ref[...].astype(F32), c, sa, sb)
    qf = jnp.dot(cq, wuq_ref[...], preferred_element_type=F32)
    kf = jnp.dot(ckv, wuk_ref[...], preferred_element_type=F32)
    v_ref[...] = jnp.dot(ckv, wuv_ref[...], preferred_element_type=F32).astype(BF16)
    for h in range(MLA_HEADS):
        sl = slice(h * MLA_PAD_DIM, (h + 1) * MLA_PAD_DIM)
        q_ref[:, sl] = (_rope_lanes(qf[:, sl], c, sa, sb) * q_scale).astype(BF16)
        k_ref[:, sl] = (kf[:, sl] + kr).astype(BF16)


def mla_prep(proj, seq, gq, gkv, wuq, wuk, wuv, tabs):
    m = proj.shape[0]
    tiles_per_seq = seq // ROW_TILE
    c, sa, sb = tabs
    row = lambda i: (i, 0)
    const = lambda i: (0, 0)
    pos = lambda i: (i % tiles_per_seq, 0)
    wide = MLA_HEADS * MLA_PAD_DIM
    return pl.pallas_call(
        functools.partial(_mla_prep_kernel,
                          q_scale=(MLA_NOPE_DIM + MLA_ROPE_DIM) ** -0.5 * LOG2E),
        grid=(m // ROW_TILE,),
        in_specs=[pl.BlockSpec((ROW_TILE, MLA_Q_RANK), lambda i: (i, COL_CQ // MLA_Q_RANK)),
                  pl.BlockSpec((ROW_TILE, MLA_KV_RANK), lambda i: (i, COL_CKV // MLA_KV_RANK)),
                  pl.BlockSpec((ROW_TILE, 128), lambda i: (i, COL_KR // 128)),
                  pl.BlockSpec((1, MLA_Q_RANK), const),
                  pl.BlockSpec((1, MLA_KV_RANK), const),
                  pl.BlockSpec(wuq.shape, const),
                  pl.BlockSpec(wuk.shape, const),
                  pl.BlockSpec(wuv.shape, const),
                  pl.BlockSpec((ROW_TILE, 128), pos),
                  pl.BlockSpec((ROW_TILE, 128), pos),
                  pl.BlockSpec((ROW_TILE, 128), pos)],
        out_specs=[pl.BlockSpec((ROW_TILE, wide), row),
                   pl.BlockSpec((ROW_TILE, wide), row),
                   pl.BlockSpec((ROW_TILE, B_WIDTH), row)],
        out_shape=[jax.ShapeDtypeStruct((m, wide), BF16),
                   jax.ShapeDtypeStruct((m, wide), BF16),
                   jax.ShapeDtypeStruct((m, B_WIDTH), BF16)],
        compiler_params=_params(("parallel",)),
        name="mla_prep",
    )(proj, proj, proj, gq.reshape(1, -1), gkv.reshape(1, -1), wuq, wuk, wuv, c, sa, sb)


def _head_prep_kernel(x_ref, g_ref, c_ref, s_ref, p_ref, o_ref, *, scale):
    x = x_ref[0, 0].astype(F32)
    ms = jnp.mean(x * x, axis=-1, keepdims=True)
    xn = x * lax.rsqrt(ms + EPS) * g_ref[...]
    rot = jnp.dot(xn.astype(BF16), p_ref[...], preferred_element_type=F32)
    o_ref[0, 0] = ((xn * c_ref[...] + rot * s_ref[...]) * scale).astype(BF16)


def head_prep(x, g, c, s, p, scale):
    b, h, t, d = x.shape
    blk = pl.BlockSpec((1, 1, ROW_TILE, d), lambda i, j, k: (i, j, k, 0))
    tab = pl.BlockSpec((ROW_TILE, d), lambda i, j, k: (k, 0))
    const = lambda i, j, k: (0, 0)
    return pl.pallas_call(
        functools.partial(_head_prep_kernel, scale=scale),
        grid=(b, h, t // ROW_TILE),
        in_specs=[blk, pl.BlockSpec((1, d), const), tab, tab, pl.BlockSpec((d, d), const)],
        out_specs=blk,
        out_shape=jax.ShapeDtypeStruct(x.shape, BF16),
        compiler_params=_params(("parallel", "parallel", "parallel")),
        name="head_prep",
    )(x, g.reshape(1, d), c, s, p)


LANES = 128


def _flash_kernel(q_ref, k_ref, v_ref, o_ref, m_ref, acc_ref, *, tk, dv, unroll):
    g, tq, d = q_ref.shape[2:]
    rows = g * tq
    q = q_ref[0, 0].reshape(rows, d)
    m_ref[...] = jnp.full(m_ref.shape, NEG_INF, F32)
    acc_ref[...] = jnp.zeros(acc_ref.shape, F32)

    def body(j, carry):
        start = pl.multiple_of(j * tk, tk)
        k = k_ref[0, 0, pl.ds(start, tk), :]
        v = v_ref[0, 0, pl.ds(start, tk), :]
        s = lax.dot_general(q, k, (((1,), (1,)), ((), ())), preferred_element_type=F32)
        m_old = m_ref[...]
        m_new = jnp.maximum(m_old, jnp.max(s, axis=-1, keepdims=True))
        alpha = jnp.exp2(m_old - m_new)
        p = jnp.exp2(s - jnp.tile(m_new, (1, tk // LANES)))
        acc_ref[...] = alpha * acc_ref[...] + jnp.dot(
            p.astype(BF16), v, preferred_element_type=F32)
        m_ref[...] = m_new
        return carry

    lax.fori_loop(0, k_ref.shape[2] // tk, body, 0, unroll=unroll)
    acc = acc_ref[...]
    out = acc[:, :dv] / acc[:, dv:dv + 1]
    o_ref[0, 0] = out.reshape(g, tq, dv).astype(o_ref.dtype)


def _with_ones_lane(v):
    dv = v.shape[-1]
    return jnp.concatenate([v, jnp.ones(v.shape[:-1] + (1,), v.dtype),
                            jnp.zeros(v.shape[:-1] + (LANES - dv - 1,), v.dtype)], axis=-1)


def flash_attention(q, k, v, *, tq, tk, unroll=1):
    b, kh, g, t, d = q.shape
    dv = v.shape[-1]
    rows = g * tq
    return pl.pallas_call(
        functools.partial(_flash_kernel, tk=tk, dv=dv, unroll=unroll),
        grid=(b, kh, t // tq),
        in_specs=[pl.BlockSpec((1, 1, g, tq, d), lambda i, j, n: (i, j, 0, n, 0)),
                  pl.BlockSpec((1, 1, t, d), lambda i, j, n: (i, j, 0, 0)),
                  pl.BlockSpec((1, 1, t, LANES), lambda i, j, n: (i, j, 0, 0))],
        out_specs=pl.BlockSpec((1, 1, g, tq, dv), lambda i, j, n: (i, j, 0, n, 0)),
        out_shape=jax.ShapeDtypeStruct((b, kh, g, t, dv), BF16),
        scratch_shapes=[pltpu.VMEM((rows, LANES), F32), pltpu.VMEM((rows, LANES), F32)],
        compiler_params=_params(("parallel", "parallel", "arbitrary")),
        name="flash_attention",
    )(q, k, _with_ones_lane(v))


A_TQ = 256
A_WIN = A_TQ + 2 * A_REACH


def _dilated_kernel(q_ref, k_ref, v_ref, b_ref, o_ref):
    start = pl.multiple_of(pl.program_id(2) * A_TQ, A_TQ)
    k = k_ref[0, 0, pl.ds(start, A_WIN), :]
    v = v_ref[0, 0, pl.ds(start, A_WIN), :]
    s = lax.dot_general(q_ref[0, 0], k, (((1,), (1,)), ((), ())),
                        preferred_element_type=F32) + b_ref[0]
    m = jnp.max(s, axis=-1, keepdims=True)
    p = jnp.exp2(s - m)
    l = jnp.sum(p, axis=-1, keepdims=True)
    o = jnp.dot(p.astype(BF16), v, preferred_element_type=F32)
    o_ref[0, 0] = (o / l).astype(o_ref.dtype)


def dilated_attention(q, k, v, bias):
    b, h, t, d = q.shape
    tp = k.shape[2]
    dv = v.shape[-1]
    return pl.pallas_call(
        _dilated_kernel,
        grid=(h, b, t // A_TQ),
        in_specs=[pl.BlockSpec((1, 1, A_TQ, d), lambda j, i, n: (i, j, n, 0)),
                  pl.BlockSpec((1, 1, tp, d), lambda j, i, n: (i, j, 0, 0)),
                  pl.BlockSpec((1, 1, tp, dv), lambda j, i, n: (i, j, 0, 0)),
                  pl.BlockSpec((1, A_TQ, A_WIN), lambda j, i, n: (j, 0, 0))],
        out_specs=pl.BlockSpec((1, 1, A_TQ, dv), lambda j, i, n: (i, j, n, 0)),
        out_shape=jax.ShapeDtypeStruct((b, h, t, dv), BF16),
        compiler_params=_params(("parallel", "parallel", "arbitrary")),
        name="dilated_attention",
    )(q, k, v, bias)


def _dilated_bias():
    i = jnp.arange(A_TQ, dtype=jnp.int32)[:, None]
    c = jnp.arange(A_WIN, dtype=jnp.int32)[None, :]
    delta = c - i - A_REACH
    dist = jnp.abs(delta)
    mult = jnp.zeros(delta.shape, F32)
    for window, dil in DILATED_CONFIGS:
        mult = mult + ((dist <= window // 2) & (delta % dil == 0)).astype(F32)
    slopes = 2.0 ** (-8.0 * jnp.arange(1, A_HEADS + 1, dtype=F32) / A_HEADS)
    bias = -slopes[:, None, None] * dist.astype(F32)[None] + jnp.log(jnp.maximum(mult, 1.0))[None]
    return jnp.where((mult > 0)[None], bias * LOG2E, NEG_INF)


def _gate_out_kernel(*refs, n_parts):
    parts = refs[:n_parts]
    gate_ref, x_ref, w_ref, g_ref, o_ref = refs[n_parts:]
    o = jnp.concatenate([r[...] for r in parts], axis=-1) if n_parts > 1 else parts[0][...]
    gate = gate_ref[...].astype(F32)
    y = (o.astype(F32) * (gate * jax.nn.sigmoid(gate))).astype(BF16)
    y = jnp.dot(y, w_ref[...], preferred_element_type=F32)
    ms = jnp.mean(y * y, axis=-1, keepdims=True)
    o_ref[...] = x_ref[...] + y * lax.rsqrt(ms + EPS) * g_ref[...]


def gate_out(parts, proj, gate_col, x, w, g):
    m, d = x.shape
    width = w.shape[0]
    row = lambda i: (i, 0)
    const = lambda i: (0, 0)
    in_specs = [pl.BlockSpec((ROW_TILE, p.shape[1]), row) for p in parts]
    in_specs += [pl.BlockSpec((ROW_TILE, width), lambda i: (i, gate_col // width)),
                 pl.BlockSpec((ROW_TILE, d), row),
                 pl.BlockSpec(w.shape, const),
                 pl.BlockSpec((1, d), const)]
    return pl.pallas_call(
        functools.partial(_gate_out_kernel, n_parts=len(parts)),
        grid=(m // ROW_TILE,),
        in_specs=in_specs,
        out_specs=pl.BlockSpec((ROW_TILE, d), row),
        out_shape=jax.ShapeDtypeStruct((m, d), F32),
        compiler_params=_params(("parallel",)),
        name="gate_out",
    )(*parts, proj, x, w, g.reshape(1, d))


def _rope_freqs():
    half = MLA_ROPE_DIM // 2
    return ROPE_THETA ** (-jnp.arange(half, dtype=F32) / half)


def _mla_tables(seq):
    ang = jnp.arange(seq, dtype=F32)[:, None] * _rope_freqs()[None, :]
    cos, sin = jnp.cos(ang), jnp.sin(ang)
    zeros = jnp.zeros_like(cos)
    ones_nope = jnp.ones((seq, MLA_NOPE_DIM), F32)
    zeros_nope = jnp.zeros((seq, MLA_NOPE_DIM), F32)
    tail = jnp.zeros((seq, MLA_PAD_DIM - MLA_NOPE_DIM - MLA_ROPE_DIM), F32)
    c = jnp.concatenate([ones_nope, cos, cos, tail], axis=1)
    sa = jnp.concatenate([zeros_nope, zeros, sin, tail], axis=1)
    sb = jnp.concatenate([zeros_nope, -sin, zeros, tail], axis=1)
    return c, sa, sb


def _axial_tables(seq):
    t = jnp.arange(seq, dtype=jnp.int32)
    rows = (t // GRID_W).astype(F32)
    cols = (t % GRID_W).astype(F32)
    f = _rope_freqs()[None, :]
    ar, ac = rows[:, None] * f, cols[:, None] * f
    c = jnp.concatenate([jnp.cos(ar), jnp.cos(ar), jnp.cos(ac), jnp.cos(ac)], axis=1)
    s = jnp.concatenate([jnp.sin(ar), jnp.sin(ar), jnp.sin(ac), jnp.sin(ac)], axis=1)
    return c, s


def _rotate_half_matrix():
    p = np.zeros((C_HEAD_DIM, C_HEAD_DIM), np.float32)
    for base in (0, 32):
        for j in range(16):
            p[base + 16 + j, base + j] = -1.0
            p[base + j, base + 16 + j] = 1.0
    return jnp.asarray(p, BF16)


def _arrange_w_in_ab(w):
    qa, ka, va, cq, ckv, kr, gate = jnp.split(
        w, list(np.cumsum([A_WIDTH, A_WIDTH, A_WIDTH, MLA_Q_RANK, MLA_KV_RANK, MLA_ROPE_DIM])), axis=1)
    d = w.shape[0]
    kr_blk = jnp.concatenate([jnp.zeros((d, KR_LANE), F32), kr,
                              jnp.zeros((d, 128 - KR_LANE - MLA_ROPE_DIM), F32)], axis=1)
    qa = qa * (A_HEAD_DIM ** -0.5 * LOG2E)
    return jnp.concatenate([qa, ka, va, cq, ckv, kr_blk, gate], axis=1).astype(BF16)


def _arrange_w_in_c(w):
    qkv, gate = w[..., :C_WIDTH + 2 * C_KV_WIDTH], w[..., C_WIDTH + 2 * C_KV_WIDTH:]
    return jnp.concatenate([gate, qkv], axis=-1).astype(BF16)


def _arrange_w_uq(w):
    r = w.shape[0]
    w = w.reshape(r, MLA_HEADS, MLA_NOPE_DIM + MLA_ROPE_DIM)
    pad = jnp.zeros((r, MLA_HEADS, MLA_PAD_DIM - MLA_NOPE_DIM - MLA_ROPE_DIM), F32)
    return jnp.concatenate([w, pad], axis=2).reshape(r, MLA_HEADS * MLA_PAD_DIM).astype(BF16)


def _arrange_w_ukv(w):
    r = w.shape[0]
    w = w.reshape(r, MLA_HEADS, MLA_NOPE_DIM + MLA_V_DIM)
    pad = jnp.zeros((r, MLA_HEADS, MLA_PAD_DIM - MLA_NOPE_DIM), F32)
    wk = jnp.concatenate([w[:, :, :MLA_NOPE_DIM], pad], axis=2).reshape(r, MLA_HEADS * MLA_PAD_DIM)
    wv = w[:, :, MLA_NOPE_DIM:].reshape(r, B_WIDTH)
    return wk.astype(BF16), wv.astype(BF16)


def _to_heads(x2d, b, t, h):
    return x2d.reshape(b, t, h, x2d.shape[1] // h).transpose(0, 2, 1, 3)


def _from_heads(x):
    b, h, t, c = x.shape
    return x.transpose(0, 2, 1, 3).reshape(b * t, h * c)


def _even_layer(x, b, t, g_pre, g_post, w_in, gq, wuq, gkv, wuk, wuv, w_out, bias, mla_tabs):
    proj = norm_proj(x, g_pre, w_in)
    qa = _to_heads(proj[:, COL_QA:COL_QA + A_WIDTH], b, t, A_HEADS)
    ka = _to_heads(proj[:, COL_KA:COL_KA + A_WIDTH], b, t, A_HEADS)
    va = _to_heads(proj[:, COL_VA:COL_VA + A_WIDTH], b, t, A_HEADS)
    lane_pad = jnp.zeros(qa.shape[:3] + (128 - A_HEAD_DIM - 1,), BF16)
    qa = jnp.concatenate([qa, jnp.ones(qa.shape[:3] + (1,), BF16), lane_pad], axis=-1)
    ka = jnp.concatenate([ka, jnp.zeros(ka.shape[:3] + (128 - A_HEAD_DIM,), BF16)], axis=-1)
    row_pad = jnp.zeros(ka.shape[:2] + (A_REACH, 128), BF16).at[..., A_HEAD_DIM].set(NEG_INF)
    ka = jnp.concatenate([row_pad, ka, row_pad], axis=2)
    va = jnp.pad(va, ((0, 0), (0, 0), (A_REACH, A_REACH), (0, 0)))
    out_a = _from_heads(dilated_attention(qa, ka, va, bias))
    qb, kb, vb = mla_prep(proj, t, gq, gkv, wuq, wuk, wuv, mla_tabs)
    qb = _to_heads(qb, b, t, MLA_HEADS)[:, :, None]
    kb = _to_heads(kb, b, t, MLA_HEADS)
    vb = _to_heads(vb, b, t, MLA_HEADS)
    out_b = flash_attention(qb, kb, vb, tq=512, tk=1024)
    out_b = _from_heads(out_b[:, :, 0])
    return gate_out([out_a, out_b], proj, COL_GATE_AB, x, w_out, g_post)


def _odd_layer(x, b, t, g_pre, g_post, w_in, q_norm, k_norm, w_out, axial_tabs, rot):
    proj = norm_proj(x, g_pre, w_in)
    c, s = axial_tabs
    q = _to_heads(proj[:, COL_QC:COL_QC + C_WIDTH], b, t, C_HEADS)
    k = _to_heads(proj[:, COL_KC:COL_KC + C_KV_WIDTH], b, t, C_KV_HEADS)
    v = _to_heads(proj[:, COL_VC:COL_VC + C_KV_WIDTH], b, t, C_KV_HEADS)
    q = head_prep(q, q_norm, c, s, rot, C_HEAD_DIM ** -0.5 * LOG2E)
    k = head_prep(k, k_norm, c, s, rot, 1.0)
    groups = C_HEADS // C_KV_HEADS
    q = q.reshape(b, C_KV_HEADS, groups, t, C_HEAD_DIM)
    o = flash_attention(q, k, v, tq=128, tk=1024, unroll=2)
    o = _from_heads(o.reshape(b, C_HEADS, t, C_HEAD_DIM))
    return gate_out([o], proj, COL_GATE_C, x, w_out, g_post)


def _trunk(x3, weights, consts):
    (norm_pre, norm_post, w_in_ab, mla_q_norm, w_uq, mla_kv_norm, w_uk, w_uv, w_out_ab,
     w_in_c, c_q_norm, c_k_norm, w_out_c) = weights
    b, t, d = x3.shape
    x = x3.reshape(b * t, d)
    bias, rot = consts
    mla_tabs = _mla_tables(t)
    axial_tabs = _axial_tables(t)
    depth = norm_pre.shape[0]
    for layer in range(depth):
        i = layer // 2
        if layer % 2 == 0:
            x = _even_layer(x, b, t, norm_pre[layer], norm_post[layer], w_in_ab[i], mla_q_norm[i],
                            w_uq[i], mla_kv_norm[i], w_uk[i], w_uv[i], w_out_ab[i], bias, mla_tabs)
        else:
            x = _odd_layer(x, b, t, norm_pre[layer], norm_post[layer], w_in_c[i], c_q_norm[i],
                           c_k_norm[i], w_out_c[i], axial_tabs, rot)
    return x.reshape(b, t, d)


def kernel(x_prompt, x_sample, norm_pre, norm_post, w_in_ab, mla_q_norm, w_uq, mla_kv_norm, w_ukv,
           w_out_ab, w_in_c, c_q_norm, c_k_norm, w_out_c):
    n_even = w_in_ab.shape[0]
    w_in_ab_p = jnp.stack([_arrange_w_in_ab(w_in_ab[i]) for i in range(n_even)])
    w_uq_p = jnp.stack([_arrange_w_uq(w_uq[i]) for i in range(n_even)])
    w_ukv_p = [_arrange_w_ukv(w_ukv[i]) for i in range(n_even)]
    w_uk_p = jnp.stack([p[0] for p in w_ukv_p])
    w_uv_p = jnp.stack([p[1] for p in w_ukv_p])
    weights = (norm_pre, norm_post, w_in_ab_p, mla_q_norm, w_uq_p, mla_kv_norm, w_uk_p, w_uv_p,
               w_out_ab.astype(BF16), _arrange_w_in_c(w_in_c), c_q_norm, c_k_norm, w_out_c.astype(BF16))
    consts = (_dilated_bias(), _rotate_half_matrix())
    return (_trunk(x_prompt, weights, consts), _trunk(x_sample, weights, consts))
```

```python
import functools
import math

import numpy as np
import jax
import jax.numpy as jnp
from jax import lax
from jax.experimental import pallas as pl
from jax.experimental.pallas import tpu as pltpu

F32 = jnp.float32
BF16 = jnp.bfloat16

D_MODEL = 1024
EPS = 1e-6
ROPE_THETA = 10000.0
NEG_INF = -1e30
LOG2E = math.log2(math.e)
GRID_W = 64
LANES = 128
A_HEADS = 8
A_HEAD_DIM = 64
DILATED_CONFIGS = ((128, 1), (512, 4), (2048, 16))
A_RADIUS = 64
A_REACH = max(w // 2 for w, _ in DILATED_CONFIGS)
MLA_HEADS = 8
MLA_Q_RANK = 256
MLA_KV_RANK = 128
MLA_NOPE_DIM = 64
MLA_ROPE_DIM = 32
MLA_V_DIM = 64
C_HEADS = 16
C_KV_HEADS = 4
C_GROUPS = C_HEADS // C_KV_HEADS
C_HEAD_DIM = 64
A_WIDTH = A_HEADS * A_HEAD_DIM
B_WIDTH = MLA_HEADS * MLA_V_DIM
AB_WIDTH = A_WIDTH + B_WIDTH
C_WIDTH = C_HEADS * C_HEAD_DIM
MLA_WIDE = MLA_HEADS * LANES
COL_QKV_A, COL_LATENT, COL_GATE_AB = 0, 1536, 2048
IN_AB_PAD = 3072
KR_LANE = MLA_NOPE_DIM
COL_GATE_C, COL_QC, COL_KC, COL_VC = 0, 1024, 2048, 2560
IN_C_PAD = 3072

VMEM_LIMIT = 56 * 1024 * 1024
ROW_TILE = 512


def _params(sem):
    return pltpu.CompilerParams(dimension_semantics=sem, vmem_limit_bytes=VMEM_LIMIT)


def _rms(x, g):
    ms = jnp.mean(x * x, axis=-1, keepdims=True)
    return x * lax.rsqrt(ms + EPS) * g


def _rope_lanes(x, c, sa, sb):
    return x * c + pltpu.roll(x, 16, 1) * sa + pltpu.roll(x, 112, 1) * sb


def _ones_lane_row(width, lane):
    idx = lax.broadcasted_iota(jnp.int32, (1, width), 1)
    return jnp.where(idx % LANES == lane, 1.0, 0.0).astype(F32)


def _in_ab_kernel(x_ref, g_ref, w_ref, gq_ref, gkv_ref, wuq_ref, wuk_ref, wuv_ref,
                  c_ref, sa_ref, sb_ref, qkv_ref, gate_ref, qb_ref, kb_ref, vb_ref, *, q_scale):
    h = _rms(x_ref[...], g_ref[...]).astype(BF16)

    def proj(c0, c1):
        return jnp.dot(h, w_ref[:, c0:c1], preferred_element_type=F32)

    for c in range(0, COL_LATENT, 512):
        qkv_ref[:, c:c + 512] = proj(c, c + 512).astype(BF16)
    for c in range(0, AB_WIDTH, 512):
        gate_ref[:, c:c + 512] = proj(COL_GATE_AB + c, COL_GATE_AB + c + 512).astype(BF16)
    lat = proj(COL_LATENT, COL_GATE_AB)
    cq = _rms(lat[:, :MLA_Q_RANK], gq_ref[...]).astype(BF16)
    ckv = _rms(lat[:, MLA_Q_RANK:MLA_Q_RANK + MLA_KV_RANK], gkv_ref[...]).astype(BF16)
    c, sa, sb = c_ref[...], sa_ref[...], sb_ref[...]
    kr = _rope_lanes(lat[:, MLA_Q_RANK + MLA_KV_RANK:], c, sa, sb)
    qf = jnp.dot(cq, wuq_ref[...], preferred_element_type=F32)
    kf = jnp.dot(ckv, wuk_ref[...], preferred_element_type=F32)
    vf = jnp.dot(ckv, wuv_ref[...], preferred_element_type=F32)
    vb_ref[...] = (vf + _ones_lane_row(MLA_WIDE, MLA_V_DIM)).astype(BF16)
    for hd in range(MLA_HEADS):
        sl = slice(hd * LANES, (hd + 1) * LANES)
        qb_ref[:, sl] = (_rope_lanes(qf[:, sl], c, sa, sb) * q_scale).astype(BF16)
        kb_ref[:, sl] = (kf[:, sl] + kr).astype(BF16)


def in_proj_ab(x, seq, g, w, gq, gkv, wuq, wuk, wuv, tabs):
    m, d = x.shape
    tiles_per_seq = seq // ROW_TILE
    row = lambda i: (i, 0)
    const = lambda i: (0, 0)
    pos = lambda i: (i % tiles_per_seq, 0)
    out_w = (COL_LATENT, AB_WIDTH, MLA_WIDE, MLA_WIDE, MLA_WIDE)
    return pl.pallas_call(
        functools.partial(_in_ab_kernel, q_scale=(MLA_NOPE_DIM + MLA_ROPE_DIM) ** -0.5 * LOG2E),
        grid=(m // ROW_TILE,),
        in_specs=[pl.BlockSpec((ROW_TILE, d), row),
                  pl.BlockSpec((1, d), const),
                  pl.BlockSpec(w.shape, const),
                  pl.BlockSpec((1, MLA_Q_RANK), const),
                  pl.BlockSpec((1, MLA_KV_RANK), const),
                  pl.BlockSpec(wuq.shape, const),
                  pl.BlockSpec(wuk.shape, const),
                  pl.BlockSpec(wuv.shape, const)] + [pl.BlockSpec((ROW_TILE, LANES), pos)] * 3,
        out_specs=[pl.BlockSpec((ROW_TILE, n), row) for n in out_w],
        out_shape=[jax.ShapeDtypeStruct((m, n), BF16) for n in out_w],
        compiler_params=_params(("parallel",)),
        name="in_proj_ab",
    )(x, g.reshape(1, d), w, gq.reshape(1, -1), gkv.reshape(1, -1), wuq, wuk, wuv, *tabs)


def _in_c_kernel(x_ref, g_ref, w_ref, gq_ref, gk_ref, bd_ref, c_ref, sa_ref, sb_ref,
                 gate_ref, q_ref, kk_ref, vv_ref, *, q_scale):
    h = _rms(x_ref[...], g_ref[...]).astype(BF16)
    c, sa, sb = c_ref[...], sa_ref[...], sb_ref[...]
    bd = bd_ref[...]

    def proj(c0, c1):
        return jnp.dot(h, w_ref[:, c0:c1], preferred_element_type=F32)

    def head_norm_rope(x, gain, scale):
        sq = x * x
        hi = sq.astype(BF16)
        lo = (sq - hi.astype(F32)).astype(BF16)
        ss = (jnp.dot(hi, bd, preferred_element_type=F32) +
              jnp.dot(lo, bd, preferred_element_type=F32))
        xn = x * lax.rsqrt(ss * (1.0 / C_HEAD_DIM) + EPS) * gain
        halves = [_rope_lanes(xn[:, s:s + LANES], c, sa, sb) for s in (0, LANES)]
        return (jnp.concatenate(halves, axis=1) * scale).astype(BF16)

    for cc in range(0, C_WIDTH, 512):
        gate_ref[:, cc:cc + 512] = proj(COL_GATE_C + cc, COL_GATE_C + cc + 512).astype(BF16)
    for cc in range(0, C_WIDTH, 256):
        q_ref[:, cc:cc + 256] = head_norm_rope(proj(COL_QC + cc, COL_QC + cc + 256), gq_ref[...], q_scale)
    for cc in range(0, 2 * C_KV_HEADS * C_HEAD_DIM, 256):
        kk_ref[:, cc:cc + 256] = head_norm_rope(proj(COL_KC + cc, COL_KC + cc + 256), gk_ref[...], 1.0)
    vv = proj(COL_VC, IN_C_PAD)
    vv_ref[...] = (vv + _ones_lane_row(vv.shape[1], C_HEAD_DIM)).astype(BF16)


def in_proj_c(x, seq, g, w, gq, gk, bd, tabs):
    m, d = x.shape
    tiles_per_seq = seq // ROW_TILE
    row = lambda i: (i, 0)
    const = lambda i: (0, 0)
    pos = lambda i: (i % tiles_per_seq, 0)
    out_w = (C_WIDTH, C_WIDTH, C_KV_HEADS * LANES, C_KV_HEADS * LANES)
    tile4 = lambda v: jnp.tile(v.reshape(1, -1), (1, 4))
    return pl.pallas_call(
        functools.partial(_in_c_kernel, q_scale=C_HEAD_DIM ** -0.5 * LOG2E),
        grid=(m // ROW_TILE,),
        in_specs=[pl.BlockSpec((ROW_TILE, d), row),
                  pl.BlockSpec((1, d), const),
                  pl.BlockSpec(w.shape, const),
                  pl.BlockSpec((1, 256), const),
                  pl.BlockSpec((1, 256), const),
                  pl.BlockSpec((256, 256), const)] + [pl.BlockSpec((ROW_TILE, LANES), pos)] * 3,
        out_specs=[pl.BlockSpec((ROW_TILE, n), row) for n in out_w],
        out_shape=[jax.ShapeDtypeStruct((m, n), BF16) for n in out_w],
        compiler_params=_params(("parallel",)),
        name="in_proj_c",
    )(x, g.reshape(1, d), w, tile4(gq), tile4(gk), bd, *tabs)


def _online_softmax(q, k_ref, v_ref, m_ref, acc_ref, *, tk, unroll, v_groups):
    m_ref[...] = jnp.full(m_ref.shape, NEG_INF, F32)
    acc_ref[...] = jnp.zeros(acc_ref.shape, F32)

    def body(j, carry):
        start = pl.multiple_of(j * tk, tk)
        k = k_ref[0, pl.ds(start, tk), :]
        s = lax.dot_general(q, k, (((1,), (1,)), ((), ())), preferred_element_type=F32)
        m_old = m_ref[...]
        m_new = jnp.maximum(m_old, jnp.max(s, axis=-1, keepdims=True))
        alpha = jnp.exp2(m_old - m_new)
        p = jnp.exp2(s - jnp.tile(m_new, (1, tk // LANES))).astype(BF16)
        for r0, r1, l0 in v_groups:
            v = v_ref[0, pl.ds(start, tk), l0:l0 + LANES]
            acc_ref[r0:r1] = alpha[r0:r1] * acc_ref[r0:r1] + jnp.dot(
                p[r0:r1], v, preferred_element_type=F32)
        m_ref[...] = m_new
        return carry

    lax.fori_loop(0, k_ref.shape[1] // tk, body, 0, unroll=unroll)


def _normalised(acc, dv):
    return acc[:, :dv] / acc[:, dv:dv + 1]


def _gqa_kernel(q_ref, k_ref, v_ref, o_ref, m_ref, acc_ref, *, tk, unroll):
    tq = q_ref.shape[1]
    q = q_ref[0].astype(F32)
    low = lax.broadcasted_iota(jnp.int32, (tq, LANES), 1) < C_HEAD_DIM
    parts = []
    for half in (q[:, :LANES], q[:, LANES:]):
        parts += [jnp.where(low, half, 0.0), jnp.where(low, 0.0, half)]
    qs = jnp.concatenate(parts, axis=0).astype(BF16)
    _online_softmax(qs, k_ref, v_ref, m_ref, acc_ref, tk=tk, unroll=unroll,
                    v_groups=((0, C_GROUPS * tq, 0),))
    acc = acc_ref[...]
    o_ref[0] = jnp.concatenate(
        [_normalised(acc[g * tq:(g + 1) * tq], C_HEAD_DIM) for g in range(C_GROUPS)],
        axis=1).astype(o_ref.dtype)


def gqa_attention(q, kk, vv, *, tq, tk, unroll):
    b, t, _ = q.shape
    rows = C_GROUPS * tq
    return pl.pallas_call(
        functools.partial(_gqa_kernel, tk=tk, unroll=unroll),
        grid=(b, C_KV_HEADS, t // tq),
        in_specs=[pl.BlockSpec((1, tq, 256), lambda i, j, n: (i, n, j)),
                  pl.BlockSpec((1, t, LANES), lambda i, j, n: (i, 0, j)),
                  pl.BlockSpec((1, t, LANES), lambda i, j, n: (i, 0, j))],
        out_specs=pl.BlockSpec((1, tq, 256), lambda i, j, n: (i, n, j)),
        out_shape=jax.ShapeDtypeStruct((b, t, C_WIDTH), BF16),
        scratch_shapes=[pltpu.VMEM((rows, LANES), F32), pltpu.VMEM((rows, LANES), F32)],
        compiler_params=_params(("parallel", "parallel", "arbitrary")),
        name="gqa_attention",
    )(q, kk, vv)


def _mla_kernel(q_ref, k_ref, v_ref, o_ref, m_ref, acc_ref, *, tk, unroll):
    tq = q_ref.shape[1]
    q = q_ref[0]
    zeros = jnp.zeros((tq, LANES), q.dtype)
    qs = jnp.concatenate([jnp.concatenate([q[:, :LANES], zeros], axis=1),
                          jnp.concatenate([zeros, q[:, LANES:]], axis=1)], axis=0)
    _online_softmax(qs, k_ref, v_ref, m_ref, acc_ref, tk=tk, unroll=unroll,
                    v_groups=((0, tq, 0), (tq, 2 * tq, LANES)))
    acc = acc_ref[...]
    o_ref[0] = jnp.concatenate(
        [_normalised(acc[g * tq:(g + 1) * tq], MLA_V_DIM) for g in range(2)],
        axis=1).astype(o_ref.dtype)


def mla_attention(qb, kb, vb, *, tq, tk, unroll):
    b, t, _ = qb.shape
    rows = 2 * tq
    return pl.pallas_call(
        functools.partial(_mla_kernel, tk=tk, unroll=unroll),
        grid=(b, MLA_HEADS // 2, t // tq),
        in_specs=[pl.BlockSpec((1, tq, 256), lambda i, j, n: (i, n, j)),
                  pl.BlockSpec((1, t, 256), lambda i, j, n: (i, 0, j)),
                  pl.BlockSpec((1, t, 256), lambda i, j, n: (i, 0, j))],
        out_specs=pl.BlockSpec((1, tq, LANES), lambda i, j, n: (i, n, j)),
        out_shape=jax.ShapeDtypeStruct((b, t, B_WIDTH), BF16),
        scratch_shapes=[pltpu.VMEM((rows, LANES), F32), pltpu.VMEM((rows, LANES), F32)],
        compiler_params=_params(("parallel", "parallel", "arbitrary")),
        name="mla_attention",
    )(qb, kb, vb)


A_TQ = 2048
A_SUB = 128
A_BAND = A_SUB + 2 * A_RADIUS


def _dilated_kernel(q_ref, k_ref, v_ref, b_ref, o_ref, qf_ref, kf_ref, vf_ref, oacc_ref, m_ref, l_ref):
    seq = k_ref.shape[1]
    qi = pl.program_id(2)

    @pl.when(qi == 0)
    def _():
        pad = jnp.zeros((A_REACH, LANES), F32)
        for ref, src in ((kf_ref, k_ref), (vf_ref, v_ref)):
            ref[0:A_REACH] = pad
            ref[A_REACH + seq:] = pad
            ref[A_REACH:A_REACH + seq] = src[0].astype(F32)

    qf_ref[...] = q_ref[0].astype(F32)
    t0 = qi * A_TQ
    low = lax.broadcasted_iota(jnp.int32, (A_SUB, LANES), 1) < A_HEAD_DIM
    key_col = lax.broadcasted_iota(jnp.int32, (1, A_BAND), 1)

    for ci, (_, dil) in enumerate(DILATED_CONFIGS):
        shift = dil.bit_length() - 1

        def body(i, carry, ci=ci, dil=dil, shift=shift):
            base = (i & (dil - 1)) + ((i >> shift) << (shift + 7))
            rows = pl.ds(base, A_SUB, stride=dil) if dil > 1 else pl.ds(pl.multiple_of(base, A_SUB), A_SUB)
            first = t0 + base - A_RADIUS * dil
            keys = pl.ds(first + A_REACH, A_BAND, stride=dil) if dil > 1 else pl.ds(first + A_REACH, A_BAND)
            q = qf_ref[rows, :]
            qs = jnp.concatenate([jnp.where(low, q, 0.0), jnp.where(low, 0.0, q)], axis=0).astype(BF16)
            kw = kf_ref[keys, :].astype(BF16)
            vw = vf_ref[keys, :].astype(BF16)
            key_pos = first + dil * key_col
            valid = jnp.where((key_pos >= 0) & (key_pos < seq), 0.0, NEG_INF)
            s = lax.dot_general(qs, kw, (((1,), (1,)), ((), ())), preferred_element_type=F32)
            s = s + b_ref[0, ci] + valid
            m_blk = jnp.max(s, axis=-1, keepdims=True)
            p = jnp.exp2(s - m_blk)
            l_blk = jnp.sum(p, axis=-1, keepdims=True)
            o_blk = jnp.dot(p.astype(BF16), vw, preferred_element_type=F32)
            o2 = jnp.where(low, o_blk[:A_SUB], o_blk[A_SUB:])
            m2 = jnp.where(low, m_blk[:A_SUB], m_blk[A_SUB:])
            l2 = jnp.where(low, l_blk[:A_SUB], l_blk[A_SUB:])
            if ci == 0:
                oacc_ref[rows, :] = o2
                m_ref[rows, :] = m2
                l_ref[rows, :] = l2
            else:
                m_old = m_ref[rows, :]
                m_new = jnp.maximum(m_old, m2)
                a_old = jnp.exp2(m_old - m_new)
                a_blk = jnp.exp2(m2 - m_new)
                oacc_ref[rows, :] = oacc_ref[rows, :] * a_old + o2 * a_blk
                l_ref[rows, :] = l_ref[rows, :] * a_old + l2 * a_blk
                m_ref[rows, :] = m_new
            return carry

        lax.fori_loop(0, A_TQ // A_SUB, body, 0, unroll=4)

    o_ref[0] = (oacc_ref[...] / l_ref[...]).astype(o_ref.dtype)


def dilated_attention(qkv, bias):
    b, t, _ = qkv.shape
    pairs = A_HEADS // 2
    state = pltpu.VMEM((A_TQ, LANES), F32)
    padded = pltpu.VMEM((t + 2 * A_REACH, LANES), F32)
    return pl.pallas_call(
        _dilated_kernel,
        grid=(b, pairs, t // A_TQ),
        in_specs=[pl.BlockSpec((1, A_TQ, LANES), lambda i, j, n: (i, n, j)),
                  pl.BlockSpec((1, t, LANES), lambda i, j, n: (i, 0, pairs + j)),
                  pl.BlockSpec((1, t, LANES), lambda i, j, n: (i, 0, 2 * pairs + j)),
                  pl.BlockSpec((1, len(DILATED_CONFIGS), 2 * A_SUB, A_BAND), lambda i, j, n: (j, 0, 0, 0))],
        out_specs=pl.BlockSpec((1, A_TQ, LANES), lambda i, j, n: (i, n, j)),
        out_shape=jax.ShapeDtypeStruct((b, t, A_WIDTH), BF16),
        scratch_shapes=[state, padded, padded, state, state, state],
        compiler_params=_params(("parallel", "parallel", "arbitrary")),
        name="dilated_attention",
    )(qkv, qkv, qkv, bias)


def _dilated_bias():
    i = jnp.arange(A_SUB, dtype=jnp.int32)[:, None]
    c = jnp.arange(A_BAND, dtype=jnp.int32)[None, :]
    hops = jnp.abs(c - A_RADIUS - i)
    slopes = 2.0 ** (-8.0 * jnp.arange(1, A_HEADS + 1, dtype=F32) / A_HEADS)
    per_cfg = []
    for _, dil in DILATED_CONFIGS:
        dist = (dil * hops).astype(F32)
        bias = jnp.where(hops <= A_RADIUS, -slopes[:, None, None] * dist[None] * LOG2E, NEG_INF)
        per_cfg.append(bias.reshape(A_HEADS // 2, 2 * A_SUB, A_BAND))
    return jnp.stack(per_cfg, axis=1)


def _gate_out_kernel(*refs, n_parts):
    parts = refs[:n_parts]
    gate_ref, x_ref, w_ref, g_ref, o_ref = refs[n_parts:]
    o = jnp.concatenate([r[...] for r in parts], axis=-1) if n_parts > 1 else parts[0][...]
    gate = gate_ref[...].astype(F32)
    y = (o.astype(F32) * (gate * jax.nn.sigmoid(gate))).astype(BF16)
    y = jnp.dot(y, w_ref[...], preferred_element_type=F32)
    o_ref[...] = x_ref[...] + _rms(y, g_ref[...])


def gate_out(parts, gate, x, w, g):
    m, d = x.shape
    row = lambda i: (i, 0)
    const = lambda i: (0, 0)
    in_specs = [pl.BlockSpec((ROW_TILE, p.shape[1]), row) for p in parts]
    in_specs += [pl.BlockSpec((ROW_TILE, gate.shape[1]), row),
                 pl.BlockSpec((ROW_TILE, d), row),
                 pl.BlockSpec(w.shape, const),
                 pl.BlockSpec((1, d), const)]
    return pl.pallas_call(
        functools.partial(_gate_out_kernel, n_parts=len(parts)),
        grid=(m // ROW_TILE,),
        in_specs=in_specs,
        out_specs=pl.BlockSpec((ROW_TILE, d), row),
        out_shape=jax.ShapeDtypeStruct((m, d), F32),
        compiler_params=_params(("parallel",)),
        name="gate_out",
    )(*parts, gate, x, w, g.reshape(1, d))


def _rope_freqs():
    half = MLA_ROPE_DIM // 2
    return ROPE_THETA ** (-jnp.arange(half, dtype=F32) / half)


def _mla_tables(seq):
    ang = jnp.arange(seq, dtype=F32)[:, None] * _rope_freqs()[None, :]
    cos, sin = jnp.cos(ang), jnp.sin(ang)
    zeros = jnp.zeros_like(cos)
    ones_nope = jnp.ones((seq, MLA_NOPE_DIM), F32)
    zeros_nope = jnp.zeros((seq, MLA_NOPE_DIM), F32)
    tail = jnp.zeros((seq, LANES - MLA_NOPE_DIM - MLA_ROPE_DIM), F32)
    c = jnp.concatenate([ones_nope, cos, cos, tail], axis=1)
    sa = jnp.concatenate([zeros_nope, zeros, sin, tail], axis=1)
    sb = jnp.concatenate([zeros_nope, -sin, zeros, tail], axis=1)
    return c, sa, sb


def _axial_tables(seq):
    t = jnp.arange(seq, dtype=jnp.int32)
    f = _rope_freqs()[None, :]
    ar = (t // GRID_W).astype(F32)[:, None] * f
    ac = (t % GRID_W).astype(F32)[:, None] * f
    zeros = jnp.zeros_like(ar)
    c = jnp.concatenate([jnp.cos(ar), jnp.cos(ar), jnp.cos(ac), jnp.cos(ac)], axis=1)
    sa = jnp.concatenate([zeros, jnp.sin(ar), zeros, jnp.sin(ac)], axis=1)
    sb = jnp.concatenate([-jnp.sin(ar), zeros, -jnp.sin(ac), zeros], axis=1)
    return tuple(jnp.tile(x, (1, 2)) for x in (c, sa, sb))


def _head_block_diag():
    idx = np.arange(256) // C_HEAD_DIM
    return jnp.asarray(idx[:, None] == idx[None, :], BF16)


def _arrange_w_in_ab(w):
    qa, ka, va, cq, ckv, kr, gate = jnp.split(
        w, list(np.cumsum([A_WIDTH, A_WIDTH, A_WIDTH, MLA_Q_RANK, MLA_KV_RANK, MLA_ROPE_DIM])), axis=1)
    d = w.shape[0]
    kr_blk = jnp.concatenate([jnp.zeros((d, KR_LANE), F32), kr,
                              jnp.zeros((d, LANES - KR_LANE - MLA_ROPE_DIM), F32)], axis=1)
    qa = qa * (A_HEAD_DIM ** -0.5 * LOG2E)
    return jnp.concatenate([qa, ka, va, cq, ckv, kr_blk, gate], axis=1).astype(BF16)


def _arrange_w_in_c(w):
    d = w.shape[0]
    q, k, v, gate = jnp.split(w, list(np.cumsum([C_WIDTH, C_KV_HEADS * C_HEAD_DIM,
                                                 C_KV_HEADS * C_HEAD_DIM])), axis=1)
    k = k.reshape(d, C_KV_HEADS, 1, C_HEAD_DIM)
    kk = jnp.broadcast_to(k, (d, C_KV_HEADS, 2, C_HEAD_DIM)).reshape(d, C_KV_HEADS * LANES)
    v = v.reshape(d, C_KV_HEADS, C_HEAD_DIM)
    vv = jnp.concatenate([v, jnp.zeros_like(v)], axis=2).reshape(d, C_KV_HEADS * LANES)
    return jnp.concatenate([gate, q, kk, vv], axis=1).astype(BF16)


def _arrange_w_uq(w):
    r = w.shape[0]
    w = w.reshape(r, MLA_HEADS, MLA_NOPE_DIM + MLA_ROPE_DIM)
    pad = jnp.zeros((r, MLA_HEADS, LANES - MLA_NOPE_DIM - MLA_ROPE_DIM), F32)
    return jnp.concatenate([w, pad], axis=2).reshape(r, MLA_WIDE).astype(BF16)


def _arrange_w_ukv(w):
    r = w.shape[0]
    w = w.reshape(r, MLA_HEADS, MLA_NOPE_DIM + MLA_V_DIM)
    pad = jnp.zeros((r, MLA_HEADS, LANES - MLA_NOPE_DIM), F32)
    wk = jnp.concatenate([w[:, :, :MLA_NOPE_DIM], pad], axis=2).reshape(r, MLA_WIDE)
    wv = jnp.concatenate([w[:, :, MLA_NOPE_DIM:], pad], axis=2).reshape(r, MLA_WIDE)
    return wk.astype(BF16), wv.astype(BF16)


def _even_layer(x, b, t, g_pre, g_post, w_in, gq, wuq, gkv, wuk, wuv, w_out, bias, tabs):
    qkv, gate, qb, kb, vb = in_proj_ab(x, t, g_pre, w_in, gq, gkv, wuq, wuk, wuv, tabs)
    three = lambda a: a.reshape(b, t, a.shape[1])
    out_a = dilated_attention(three(qkv), bias)
    out_b = mla_attention(three(qb), three(kb), three(vb), tq=512, tk=1024, unroll=4)
    return gate_out([out_a.reshape(b * t, A_WIDTH), out_b.reshape(b * t, B_WIDTH)], gate, x, w_out, g_post)


def _odd_layer(x, b, t, g_pre, g_post, w_in, q_norm, k_norm, w_out, bd, tabs):
    gate, q, kk, vv = in_proj_c(x, t, g_pre, w_in, q_norm, k_norm, bd, tabs)
    three = lambda a: a.reshape(b, t, a.shape[1])
    o = gqa_attention(three(q), three(kk), three(vv), tq=256, tk=1024, unroll=4)
    return gate_out([o.reshape(b * t, C_WIDTH)], gate, x, w_out, g_post)


def _trunk(x3, weights, consts):
    (norm_pre, norm_post, w_in_ab, mla_q_norm, w_uq, mla_kv_norm, w_uk, w_uv, w_out_ab,
     w_in_c, c_q_norm, c_k_norm, w_out_c) = weights
    b, t, d = x3.shape
    x = x3.reshape(b * t, d)
    bias, bd = consts
    mla_tabs = _mla_tables(t)
    axial_tabs = _axial_tables(t)
    depth = norm_pre.shape[0]
    for layer in range(depth):
        i = layer // 2
        if layer % 2 == 0:
            x = _even_layer(x, b, t, norm_pre[layer], norm_post[layer], w_in_ab[i], mla_q_norm[i],
                            w_uq[i], mla_kv_norm[i], w_uk[i], w_uv[i], w_out_ab[i], bias, mla_tabs)
        else:
            x = _odd_layer(x, b, t, norm_pre[layer], norm_post[layer], w_in_c[i], c_q_norm[i],
                           c_k_norm[i], w_out_c[i], bd, axial_tabs)
    return x.reshape(b, t, d)


def kernel(x_prompt, x_sample, norm_pre, norm_post, w_in_ab, mla_q_norm, w_uq, mla_kv_norm, w_ukv,
           w_out_ab, w_in_c, c_q_norm, c_k_norm, w_out_c):
    n_even, n_odd = w_in_ab.shape[0], w_in_c.shape[0]
    w_in_ab_p = jnp.stack([_arrange_w_in_ab(w_in_ab[i]) for i in range(n_even)])
    w_uq_p = jnp.stack([_arrange_w_uq(w_uq[i]) for i in range(n_even)])
    w_ukv_p = [_arrange_w_ukv(w_ukv[i]) for i in range(n_even)]
    w_uk_p = jnp.stack([p[0] for p in w_ukv_p])
    w_uv_p = jnp.stack([p[1] for p in w_ukv_p])
    w_in_c_p = jnp.stack([_arrange_w_in_c(w_in_c[i]) for i in range(n_odd)])
    weights = (norm_pre, norm_post, w_in_ab_p, mla_q_norm, w_uq_p, mla_kv_norm, w_uk_p, w_uv_p,
               w_out_ab.astype(BF16), w_in_c_p, c_q_norm, c_k_norm, w_out_c.astype(BF16))
    consts = (_dilated_bias(), _head_block_diag())
    return (_trunk(x_prompt, weights, consts), _trunk(x_sample, weights, consts))
```

```python
import functools
import math

import numpy as np
import jax
import jax.numpy as jnp
from jax import lax
from jax.experimental import pallas as pl
from jax.experimental.pallas import tpu as pltpu

F32 = jnp.float32
BF16 = jnp.bfloat16

D_MODEL = 1024
EPS = 1e-6
ROPE_THETA = 10000.0
NEG_INF = -1e30
LOG2E = math.log2(math.e)
GRID_W = 64
LANES = 128
A_HEADS = 8
A_HEAD_DIM = 64
DILATED_CONFIGS = ((128, 1), (512, 4), (2048, 16))
A_RADIUS = 64
A_REACH = max(w // 2 for w, _ in DILATED_CONFIGS)
MLA_HEADS = 8
MLA_Q_RANK = 256
MLA_KV_RANK = 128
MLA_NOPE_DIM = 64
MLA_ROPE_DIM = 32
MLA_V_DIM = 64
C_HEADS = 16
C_KV_HEADS = 4
C_GROUPS = C_HEADS // C_KV_HEADS
C_HEAD_DIM = 64
A_WIDTH = A_HEADS * A_HEAD_DIM
B_WIDTH = MLA_HEADS * MLA_V_DIM
AB_WIDTH = A_WIDTH + B_WIDTH
C_WIDTH = C_HEADS * C_HEAD_DIM
MLA_WIDE = MLA_HEADS * LANES
COL_QKV_A, COL_LATENT, COL_GATE_AB = 0, 1536, 2048
IN_AB_PAD = 3072
KR_LANE = MLA_NOPE_DIM
COL_GATE_C, COL_QC, COL_KC = 0, 1024, 2048

VMEM_LIMIT = 56 * 1024 * 1024
ROW_TILE = 512


def _params(sem):
    return pltpu.CompilerParams(dimension_semantics=sem, vmem_limit_bytes=VMEM_LIMIT)


def _rms(x, g):
    ms = jnp.mean(x * x, axis=-1, keepdims=True)
    return x * lax.rsqrt(ms + EPS) * g


def _rope_lanes(x, c, sa, sb):
    return x * c + pltpu.roll(x, 16, 1) * sa + pltpu.roll(x, 112, 1) * sb


def _ones_row_col(height, row):
    idx = lax.broadcasted_iota(jnp.int32, (height, 1), 0)
    return jnp.where(idx % LANES == row, 1.0, 0.0).astype(F32)


def _in_ab_kernel(x_ref, g_ref, w_ref, gq_ref, gkv_ref, wuq_ref, wuk_ref, wuv_ref,
                  c_ref, sa_ref, sb_ref, qkv_ref, gate_ref, qb_ref, kb_ref, vbt_ref, *, q_scale):
    h = _rms(x_ref[...], g_ref[...]).astype(BF16)

    def proj(c0, c1):
        return jnp.dot(h, w_ref[:, c0:c1], preferred_element_type=F32)

    for c in range(0, COL_LATENT, 512):
        qkv_ref[:, c:c + 512] = proj(c, c + 512).astype(BF16)
    for c in range(0, AB_WIDTH, 512):
        gate_ref[:, c:c + 512] = proj(COL_GATE_AB + c, COL_GATE_AB + c + 512).astype(BF16)
    lat = proj(COL_LATENT, COL_GATE_AB)
    cq = _rms(lat[:, :MLA_Q_RANK], gq_ref[...]).astype(BF16)
    ckv = _rms(lat[:, MLA_Q_RANK:MLA_Q_RANK + MLA_KV_RANK], gkv_ref[...]).astype(BF16)
    c, sa, sb = c_ref[...], sa_ref[...], sb_ref[...]
    kr = _rope_lanes(lat[:, MLA_Q_RANK + MLA_KV_RANK:], c, sa, sb)
    qf = jnp.dot(cq, wuq_ref[...], preferred_element_type=F32)
    kf = jnp.dot(ckv, wuk_ref[...], preferred_element_type=F32)
    vft = lax.dot_general(wuv_ref[...], ckv, (((1,), (1,)), ((), ())), preferred_element_type=F32)
    vbt_ref[0] = (vft + _ones_row_col(MLA_WIDE, MLA_V_DIM)).astype(BF16)
    for hd in range(MLA_HEADS):
        sl = slice(hd * LANES, (hd + 1) * LANES)
        qb_ref[:, sl] = (_rope_lanes(qf[:, sl], c, sa, sb) * q_scale).astype(BF16)
        kb_ref[:, sl] = (kf[:, sl] + kr).astype(BF16)


def in_proj_ab(x, seq, g, w, gq, gkv, wuq, wuk, wuv, tabs):
    m, d = x.shape
    tiles_per_seq = seq // ROW_TILE
    row = lambda i: (i, 0)
    const = lambda i: (0, 0)
    pos = lambda i: (i % tiles_per_seq, 0)
    out_w = (COL_LATENT, AB_WIDTH, MLA_WIDE, MLA_WIDE)
    return pl.pallas_call(
        functools.partial(_in_ab_kernel, q_scale=(MLA_NOPE_DIM + MLA_ROPE_DIM) ** -0.5 * LOG2E),
        grid=(m // ROW_TILE,),
        in_specs=[pl.BlockSpec((ROW_TILE, d), row),
                  pl.BlockSpec((1, d), const),
                  pl.BlockSpec(w.shape, const),
                  pl.BlockSpec((1, MLA_Q_RANK), const),
                  pl.BlockSpec((1, MLA_KV_RANK), const),
                  pl.BlockSpec(wuq.shape, const),
                  pl.BlockSpec(wuk.shape, const),
                  pl.BlockSpec(wuv.shape, const)] + [pl.BlockSpec((ROW_TILE, LANES), pos)] * 3,
        out_specs=[pl.BlockSpec((ROW_TILE, n), row) for n in out_w] + [
            pl.BlockSpec((1, MLA_WIDE, ROW_TILE), lambda i: (i // tiles_per_seq, 0, i % tiles_per_seq))],
        out_shape=[jax.ShapeDtypeStruct((m, n), BF16) for n in out_w] + [
            jax.ShapeDtypeStruct((m // seq, MLA_WIDE, seq), BF16)],
        compiler_params=_params(("parallel",)),
        name="in_proj_ab",
    )(x, g.reshape(1, d), w, gq.reshape(1, -1), gkv.reshape(1, -1), wuq, wuk, wuv, *tabs)


def _in_c_kernel(x_ref, g_ref, w_ref, wvt_ref, gq_ref, gk_ref, bd_ref, c_ref, sa_ref, sb_ref,
                 gate_ref, q_ref, kk_ref, vvt_ref, *, q_scale):
    h = _rms(x_ref[...], g_ref[...]).astype(BF16)
    c, sa, sb = c_ref[...], sa_ref[...], sb_ref[...]
    bd = bd_ref[...]

    def proj(c0, c1):
        return jnp.dot(h, w_ref[:, c0:c1], preferred_element_type=F32)

    def head_norm_rope(x, gain, scale):
        sq = x * x
        hi = sq.astype(BF16)
        lo = (sq - hi.astype(F32)).astype(BF16)
        ss = (jnp.dot(hi, bd, preferred_element_type=F32) +
              jnp.dot(lo, bd, preferred_element_type=F32))
        xn = x * lax.rsqrt(ss * (1.0 / C_HEAD_DIM) + EPS) * gain
        halves = [_rope_lanes(xn[:, s:s + LANES], c, sa, sb) for s in (0, LANES)]
        return (jnp.concatenate(halves, axis=1) * scale).astype(BF16)

    for cc in range(0, C_WIDTH, 512):
        gate_ref[:, cc:cc + 512] = proj(COL_GATE_C + cc, COL_GATE_C + cc + 512).astype(BF16)
    for cc in range(0, C_WIDTH, 256):
        q_ref[:, cc:cc + 256] = head_norm_rope(proj(COL_QC + cc, COL_QC + cc + 256), gq_ref[...], q_scale)
    for cc in range(0, 2 * C_KV_HEADS * C_HEAD_DIM, 256):
        kk_ref[:, cc:cc + 256] = head_norm_rope(proj(COL_KC + cc, COL_KC + cc + 256), gk_ref[...], 1.0)
    vvt = lax.dot_general(wvt_ref[...], h, (((1,), (1,)), ((), ())), preferred_element_type=F32)
    vvt_ref[0] = (vvt + _ones_row_col(vvt.shape[0], C_HEAD_DIM)).astype(BF16)


def in_proj_c(x, seq, g, w, wvt, gq, gk, bd, tabs):
    m, d = x.shape
    tiles_per_seq = seq // ROW_TILE
    row = lambda i: (i, 0)
    const = lambda i: (0, 0)
    pos = lambda i: (i % tiles_per_seq, 0)
    kv_w = C_KV_HEADS * LANES
    out_w = (C_WIDTH, C_WIDTH, kv_w)
    tile4 = lambda v: jnp.tile(v.reshape(1, -1), (1, 4))
    return pl.pallas_call(
        functools.partial(_in_c_kernel, q_scale=C_HEAD_DIM ** -0.5 * LOG2E),
        grid=(m // ROW_TILE,),
        in_specs=[pl.BlockSpec((ROW_TILE, d), row),
                  pl.BlockSpec((1, d), const),
                  pl.BlockSpec(w.shape, const),
                  pl.BlockSpec(wvt.shape, const),
                  pl.BlockSpec((1, 256), const),
                  pl.BlockSpec((1, 256), const),
                  pl.BlockSpec((256, 256), const)] + [pl.BlockSpec((ROW_TILE, LANES), pos)] * 3,
        out_specs=[pl.BlockSpec((ROW_TILE, n), row) for n in out_w] + [
            pl.BlockSpec((1, kv_w, ROW_TILE), lambda i: (i // tiles_per_seq, 0, i % tiles_per_seq))],
        out_shape=[jax.ShapeDtypeStruct((m, n), BF16) for n in out_w] + [
            jax.ShapeDtypeStruct((m // seq, kv_w, seq), BF16)],
        compiler_params=_params(("parallel",)),
        name="in_proj_c",
    )(x, g.reshape(1, d), w, wvt, tile4(gq), tile4(gk), bd, *tabs)


QUERY_GROUP = 512


def _online_softmax_t(q, k_ref, vt_ref, m_ref, acc_ref, *, tk, v_groups):
    m_ref[...] = jnp.full(m_ref.shape, NEG_INF, F32)
    acc_ref[...] = jnp.zeros(acc_ref.shape, F32)
    stages = [(start, slice(g0, g0 + QUERY_GROUP), r0)
              for start in range(0, k_ref.shape[1], tk)
              for c0, c1, r0 in v_groups for g0 in range(c0, c1, QUERY_GROUP)]

    def scores(stage):
        start, cols, _ = stage
        return lax.dot_general(k_ref[0, start:start + tk, :], q[cols], (((1,), (1,)), ((), ())),
                               preferred_element_type=F32)

    def accumulate(stage, alpha, pt):
        start, cols, r0 = stage
        acc_ref[:, cols] = alpha * acc_ref[:, cols] + jnp.dot(
            vt_ref[0, r0:r0 + LANES, start:start + tk], pt, preferred_element_type=F32)

    st_next = scores(stages[0])
    pending = None
    for i, stage in enumerate(stages):
        st = st_next
        if i + 1 < len(stages):
            st_next = scores(stages[i + 1])
        if pending is not None:
            accumulate(*pending)
        cols = stage[1]
        m_old = m_ref[:, cols]
        m_new = jnp.maximum(m_old, jnp.max(st, axis=0, keepdims=True))
        m_ref[:, cols] = m_new
        pending = (stage, jnp.exp2(m_old - m_new), jnp.exp2(st - m_new).astype(BF16))
    accumulate(*pending)


def _normalised_t(acc_t, dv):
    return (acc_t[:dv] / acc_t[dv:dv + 1]).T


def _attention_call(kernel, q, k, vt, *, name, heads_axis, tq, q_w, kv_w, out_w, out_total):
    b, t, _ = q.shape
    rows = QUERY_GROUP * 2
    return pl.pallas_call(
        kernel,
        grid=(b, heads_axis, t // tq),
        in_specs=[pl.BlockSpec((1, tq, q_w), lambda i, j, n: (i, n, j)),
                  pl.BlockSpec((1, t, kv_w), lambda i, j, n: (i, 0, j)),
                  pl.BlockSpec((1, kv_w, t), lambda i, j, n: (i, j, 0))],
        out_specs=pl.BlockSpec((1, tq, out_w), lambda i, j, n: (i, n, j)),
        out_shape=jax.ShapeDtypeStruct((b, t, out_total), BF16),
        scratch_shapes=[pltpu.VMEM((1, rows), F32), pltpu.VMEM((LANES, rows), F32)],
        compiler_params=_params(("parallel", "parallel", "arbitrary")),
        name=name,
    )(q, k, vt)


def _gqa_kernel(q_ref, k_ref, vt_ref, o_ref, m_ref, acc_ref, *, tk):
    tq = q_ref.shape[1]
    q = q_ref[0].astype(F32)
    low = lax.broadcasted_iota(jnp.int32, (tq, LANES), 1) < C_HEAD_DIM
    parts = []
    for half in (q[:, :LANES], q[:, LANES:]):
        parts += [jnp.where(low, half, 0.0), jnp.where(low, 0.0, half)]
    qs = jnp.concatenate(parts, axis=0).astype(BF16)
    _online_softmax_t(qs, k_ref, vt_ref, m_ref, acc_ref, tk=tk, v_groups=((0, C_GROUPS * tq, 0),))
    acc = acc_ref[...]
    o_ref[0] = jnp.concatenate(
        [_normalised_t(acc[:, g * tq:(g + 1) * tq], C_HEAD_DIM) for g in range(C_GROUPS)],
        axis=1).astype(o_ref.dtype)


def gqa_attention(q, kk, vvt, *, tk):
    return _attention_call(functools.partial(_gqa_kernel, tk=tk), q, kk, vvt, name="gqa_attention",
                           heads_axis=C_KV_HEADS, tq=2 * QUERY_GROUP // C_GROUPS, q_w=256, kv_w=LANES,
                           out_w=256, out_total=C_WIDTH)


def _mla_kernel(q_ref, k_ref, vt_ref, o_ref, m_ref, acc_ref, *, tk):
    tq = q_ref.shape[1]
    q = q_ref[0]
    zeros = jnp.zeros((tq, LANES), q.dtype)
    qs = jnp.concatenate([jnp.concatenate([q[:, :LANES], zeros], axis=1),
                          jnp.concatenate([zeros, q[:, LANES:]], axis=1)], axis=0)
    _online_softmax_t(qs, k_ref, vt_ref, m_ref, acc_ref, tk=tk,
                      v_groups=((0, tq, 0), (tq, 2 * tq, LANES)))
    acc = acc_ref[...]
    o_ref[0] = jnp.concatenate(
        [_normalised_t(acc[:, g * tq:(g + 1) * tq], MLA_V_DIM) for g in range(2)],
        axis=1).astype(o_ref.dtype)


def mla_attention(qb, kb, vbt, *, tk):
    return _attention_call(functools.partial(_mla_kernel, tk=tk), qb, kb, vbt, name="mla_attention",
                           heads_axis=MLA_HEADS // 2, tq=QUERY_GROUP, q_w=256, kv_w=256,
                           out_w=LANES, out_total=B_WIDTH)


A_TQ = 2048
A_SUB = 128
A_BAND = A_SUB + 2 * A_RADIUS


def _dilated_kernel(q_ref, k_ref, v_ref, b_ref, o_ref, qf_ref, kf_ref, vf_ref, oacc_ref, m_ref, l_ref):
    seq = k_ref.shape[1]
    qi = pl.program_id(2)

    @pl.when(qi == 0)
    def _():
        pad = jnp.zeros((A_REACH, LANES), F32)
        for ref, src in ((kf_ref, k_ref), (vf_ref, v_ref)):
            ref[0:A_REACH] = pad
            ref[A_REACH + seq:] = pad
            ref[A_REACH:A_REACH + seq] = src[0].astype(F32)

    qf_ref[...] = q_ref[0].astype(F32)
    t0 = qi * A_TQ
    low = lax.broadcasted_iota(jnp.int32, (A_SUB, LANES), 1) < A_HEAD_DIM
    key_col = lax.broadcasted_iota(jnp.int32, (1, A_BAND), 1)

    for ci, (_, dil) in enumerate(DILATED_CONFIGS):
        shift = dil.bit_length() - 1

        def body(i, carry, ci=ci, dil=dil, shift=shift):
            base = (i & (dil - 1)) + ((i >> shift) << (shift + 7))
            rows = pl.ds(base, A_SUB, stride=dil) if dil > 1 else pl.ds(pl.multiple_of(base, A_SUB), A_SUB)
            first = t0 + base - A_RADIUS * dil
            keys = pl.ds(first + A_REACH, A_BAND, stride=dil) if dil > 1 else pl.ds(first + A_REACH, A_BAND)
            q = qf_ref[rows, :]
            qs = jnp.concatenate([jnp.where(low, q, 0.0), jnp.where(low, 0.0, q)], axis=0).astype(BF16)
            kw = kf_ref[keys, :].astype(BF16)
            vw = vf_ref[keys, :].astype(BF16)
            key_pos = first + dil * key_col
            valid = jnp.where((key_pos >= 0) & (key_pos < seq), 0.0, NEG_INF)
            s = lax.dot_general(qs, kw, (((1,), (1,)), ((), ())), preferred_element_type=F32)
            s = s + b_ref[0, ci] + valid
            m_blk = jnp.max(s, axis=-1, keepdims=True)
            p = jnp.exp2(s - m_blk)
            l_blk = jnp.sum(p, axis=-1, keepdims=True)
            o_blk = jnp.dot(p.astype(BF16), vw, preferred_element_type=F32)
            o2 = jnp.where(low, o_blk[:A_SUB], o_blk[A_SUB:])
            m2 = jnp.where(low, m_blk[:A_SUB], m_blk[A_SUB:])
            l2 = jnp.where(low, l_blk[:A_SUB], l_blk[A_SUB:])
            if ci == 0:
                oacc_ref[rows, :] = o2
                m_ref[rows, :] = m2
                l_ref[rows, :] = l2
            else:
                m_old = m_ref[rows, :]
                m_new = jnp.maximum(m_old, m2)
                a_old = jnp.exp2(m_old - m_new)
                a_blk = jnp.exp2(m2 - m_new)
                oacc_ref[rows, :] = oacc_ref[rows, :] * a_old + o2 * a_blk
                l_ref[rows, :] = l_ref[rows, :] * a_old + l2 * a_blk
                m_ref[rows, :] = m_new
            return carry

        lax.fori_loop(0, A_TQ // A_SUB, body, 0, unroll=4)

    o_ref[0] = (oacc_ref[...] / l_ref[...]).astype(o_ref.dtype)


def dilated_attention(qkv, bias):
    b, t, _ = qkv.shape
    pairs = A_HEADS // 2
    state = pltpu.VMEM((A_TQ, LANES), F32)
    padded = pltpu.VMEM((t + 2 * A_REACH, LANES), F32)
    return pl.pallas_call(
        _dilated_kernel,
        grid=(b, pairs, t // A_TQ),
        in_specs=[pl.BlockSpec((1, A_TQ, LANES), lambda i, j, n: (i, n, j)),
                  pl.BlockSpec((1, t, LANES), lambda i, j, n: (i, 0, pairs + j)),
                  pl.BlockSpec((1, t, LANES), lambda i, j, n: (i, 0, 2 * pairs + j)),
                  pl.BlockSpec((1, len(DILATED_CONFIGS), 2 * A_SUB, A_BAND), lambda i, j, n: (j, 0, 0, 0))],
        out_specs=pl.BlockSpec((1, A_TQ, LANES), lambda i, j, n: (i, n, j)),
        out_shape=jax.ShapeDtypeStruct((b, t, A_WIDTH), BF16),
        scratch_shapes=[state, padded, padded, state, state, state],
        compiler_params=_params(("parallel", "parallel", "arbitrary")),
        name="dilated_attention",
    )(qkv, qkv, qkv, bias)


def _dilated_bias():
    i = jnp.arange(A_SUB, dtype=jnp.int32)[:, None]
    c = jnp.arange(A_BAND, dtype=jnp.int32)[None, :]
    hops = jnp.abs(c - A_RADIUS - i)
    slopes = 2.0 ** (-8.0 * jnp.arange(1, A_HEADS + 1, dtype=F32) / A_HEADS)
    per_cfg = []
    for _, dil in DILATED_CONFIGS:
        dist = (dil * hops).astype(F32)
        bias = jnp.where(hops <= A_RADIUS, -slopes[:, None, None] * dist[None] * LOG2E, NEG_INF)
        per_cfg.append(bias.reshape(A_HEADS // 2, 2 * A_SUB, A_BAND))
    return jnp.stack(per_cfg, axis=1)


def _gate_out_kernel(*refs, n_parts):
    parts = refs[:n_parts]
    gate_ref, x_ref, w_ref, g_ref, o_ref = refs[n_parts:]
    o = jnp.concatenate([r[...] for r in parts], axis=-1) if n_parts > 1 else parts[0][...]
    gate = gate_ref[...].astype(F32)
    y = (o.astype(F32) * (gate * jax.nn.sigmoid(gate))).astype(BF16)
    y = jnp.dot(y, w_ref[...], preferred_element_type=F32)
    o_ref[...] = x_ref[...] + _rms(y, g_ref[...])


def gate_out(parts, gate, x, w, g):
    m, d = x.shape
    row = lambda i: (i, 0)
    const = lambda i: (0, 0)
    in_specs = [pl.BlockSpec((ROW_TILE, p.shape[1]), row) for p in parts]
    in_specs += [pl.BlockSpec((ROW_TILE, gate.shape[1]), row),
                 pl.BlockSpec((ROW_TILE, d), row),
                 pl.BlockSpec(w.shape, const),
                 pl.BlockSpec((1, d), const)]
    return pl.pallas_call(
        functools.partial(_gate_out_kernel, n_parts=len(parts)),
        grid=(m // ROW_TILE,),
        in_specs=in_specs,
        out_specs=pl.BlockSpec((ROW_TILE, d), row),
        out_shape=jax.ShapeDtypeStruct((m, d), F32),
        compiler_params=_params(("parallel",)),
        name="gate_out",
    )(*parts, gate, x, w, g.reshape(1, d))


def _rope_freqs():
    half = MLA_ROPE_DIM // 2
    return ROPE_THETA ** (-jnp.arange(half, dtype=F32) / half)


def _mla_tables(seq):
    ang = jnp.arange(seq, dtype=F32)[:, None] * _rope_freqs()[None, :]
    cos, sin = jnp.cos(ang), jnp.sin(ang)
    zeros = jnp.zeros_like(cos)
    ones_nope = jnp.ones((seq, MLA_NOPE_DIM), F32)
    zeros_nope = jnp.zeros((seq, MLA_NOPE_DIM), F32)
    tail = jnp.zeros((seq, LANES - MLA_NOPE_DIM - MLA_ROPE_DIM), F32)
    c = jnp.concatenate([ones_nope, cos, cos, tail], axis=1)
    sa = jnp.concatenate([zeros_nope, zeros, sin, tail], axis=1)
    sb = jnp.concatenate([zeros_nope, -sin, zeros, tail], axis=1)
    return c, sa, sb


def _axial_tables(seq):
    t = jnp.arange(seq, dtype=jnp.int32)
    f = _rope_freqs()[None, :]
    ar = (t // GRID_W).astype(F32)[:, None] * f
    ac = (t % GRID_W).astype(F32)[:, None] * f
    zeros = jnp.zeros_like(ar)
    c = jnp.concatenate([jnp.cos(ar), jnp.cos(ar), jnp.cos(ac), jnp.cos(ac)], axis=1)
    sa = jnp.concatenate([zeros, jnp.sin(ar), zeros, jnp.sin(ac)], axis=1)
    sb = jnp.concatenate([-jnp.sin(ar), zeros, -jnp.sin(ac), zeros], axis=1)
    return tuple(jnp.tile(x, (1, 2)) for x in (c, sa, sb))


def _head_block_diag():
    idx = np.arange(256) // C_HEAD_DIM
    return jnp.asarray(idx[:, None] == idx[None, :], BF16)


def _arrange_w_in_ab(w):
    qa, ka, va, cq, ckv, kr, gate = jnp.split(
        w, list(np.cumsum([A_WIDTH, A_WIDTH, A_WIDTH, MLA_Q_RANK, MLA_KV_RANK, MLA_ROPE_DIM])), axis=1)
    d = w.shape[0]
    kr_blk = jnp.concatenate([jnp.zeros((d, KR_LANE), F32), kr,
                              jnp.zeros((d, LANES - KR_LANE - MLA_ROPE_DIM), F32)], axis=1)
    qa = qa * (A_HEAD_DIM ** -0.5 * LOG2E)
    return jnp.concatenate([qa, ka, va, cq, ckv, kr_blk, gate], axis=1).astype(BF16)


def _arrange_w_in_c(w):
    d = w.shape[0]
    q, k, v, gate = jnp.split(w, list(np.cumsum([C_WIDTH, C_KV_HEADS * C_HEAD_DIM,
                                                 C_KV_HEADS * C_HEAD_DIM])), axis=1)
    k = k.reshape(d, C_KV_HEADS, 1, C_HEAD_DIM)
    kk = jnp.broadcast_to(k, (d, C_KV_HEADS, 2, C_HEAD_DIM)).reshape(d, C_KV_HEADS * LANES)
    v = v.reshape(d, C_KV_HEADS, C_HEAD_DIM)
    vv = jnp.concatenate([v, jnp.zeros_like(v)], axis=2).reshape(d, C_KV_HEADS * LANES)
    return jnp.concatenate([gate, q, kk], axis=1).astype(BF16), vv.T.astype(BF16)


def _arrange_w_uq(w):
    r = w.shape[0]
    w = w.reshape(r, MLA_HEADS, MLA_NOPE_DIM + MLA_ROPE_DIM)
    pad = jnp.zeros((r, MLA_HEADS, LANES - MLA_NOPE_DIM - MLA_ROPE_DIM), F32)
    return jnp.concatenate([w, pad], axis=2).reshape(r, MLA_WIDE).astype(BF16)


def _arrange_w_ukv(w):
    r = w.shape[0]
    w = w.reshape(r, MLA_HEADS, MLA_NOPE_DIM + MLA_V_DIM)
    pad = jnp.zeros((r, MLA_HEADS, LANES - MLA_NOPE_DIM), F32)
    wk = jnp.concatenate([w[:, :, :MLA_NOPE_DIM], pad], axis=2).reshape(r, MLA_WIDE)
    wv = jnp.concatenate([w[:, :, MLA_NOPE_DIM:], pad], axis=2).reshape(r, MLA_WIDE)
    return wk.astype(BF16), wv.T.astype(BF16)


def _even_layer(x, b, t, g_pre, g_post, w_in, gq, wuq, gkv, wuk, wuv, w_out, bias, tabs):
    qkv, gate, qb, kb, vbt = in_proj_ab(x, t, g_pre, w_in, gq, gkv, wuq, wuk, wuv, tabs)
    three = lambda a: a.reshape(b, t, a.shape[1])
    out_a = dilated_attention(three(qkv), bias)
    out_b = mla_attention(three(qb), three(kb), vbt, tk=1024)
    return gate_out([out_a.reshape(b * t, A_WIDTH), out_b.reshape(b * t, B_WIDTH)], gate, x, w_out, g_post)


def _odd_layer(x, b, t, g_pre, g_post, w_in, w_vt, q_norm, k_norm, w_out, bd, tabs):
    gate, q, kk, vvt = in_proj_c(x, t, g_pre, w_in, w_vt, q_norm, k_norm, bd, tabs)
    three = lambda a: a.reshape(b, t, a.shape[1])
    o = gqa_attention(three(q), three(kk), vvt, tk=1024)
    return gate_out([o.reshape(b * t, C_WIDTH)], gate, x, w_out, g_post)


def _trunk(x3, weights, consts):
    (norm_pre, norm_post, w_in_ab, mla_q_norm, w_uq, mla_kv_norm, w_uk, w_uv, w_out_ab,
     w_in_c, w_vt_c, c_q_norm, c_k_norm, w_out_c) = weights
    b, t, d = x3.shape
    x = x3.reshape(b * t, d)
    bias, bd = consts
    mla_tabs = _mla_tables(t)
    axial_tabs = _axial_tables(t)
    depth = norm_pre.shape[0]
    for layer in range(depth):
        i = layer // 2
        if layer % 2 == 0:
            x = _even_layer(x, b, t, norm_pre[layer], norm_post[layer], w_in_ab[i], mla_q_norm[i],
                            w_uq[i], mla_kv_norm[i], w_uk[i], w_uv[i], w_out_ab[i], bias, mla_tabs)
        else:
            x = _odd_layer(x, b, t, norm_pre[layer], norm_post[layer], w_in_c[i], w_vt_c[i], c_q_norm[i],
                           c_k_norm[i], w_out_c[i], bd, axial_tabs)
    return x.reshape(b, t, d)


def kernel(x_prompt, x_sample, norm_pre, norm_post, w_in_ab, mla_q_norm, w_uq, mla_kv_norm, w_ukv,
           w_out_ab, w_in_c, c_q_norm, c_k_norm, w_out_c):
    n_even, n_odd = w_in_ab.shape[0], w_in_c.shape[0]
    w_in_ab_p = jnp.stack([_arrange_w_in_ab(w_in_ab[i]) for i in range(n_even)])
    w_uq_p = jnp.stack([_arrange_w_uq(w_uq[i]) for i in range(n_even)])
    w_ukv_p = [_arrange_w_ukv(w_ukv[i]) for i in range(n_even)]
    w_uk_p = jnp.stack([p[0] for p in w_ukv_p])
    w_uv_p = jnp.stack([p[1] for p in w_ukv_p])
    w_in_c_pairs = [_arrange_w_in_c(w_in_c[i]) for i in range(n_odd)]
    w_in_c_p = jnp.stack([p[0] for p in w_in_c_pairs])
    w_vt_c_p = jnp.stack([p[1] for p in w_in_c_pairs])
    weights = (norm_pre, norm_post, w_in_ab_p, mla_q_norm, w_uq_p, mla_kv_norm, w_uk_p, w_uv_p,
               w_out_ab.astype(BF16), w_in_c_p, w_vt_c_p, c_q_norm, c_k_norm, w_out_c.astype(BF16))
    consts = (_dilated_bias(), _head_block_diag())
    return (_trunk(x_prompt, weights, consts), _trunk(x_sample, weights, consts))
```

```python
import functools
import math

import numpy as np
import jax
import jax.numpy as jnp
from jax import lax
from jax.experimental import pallas as pl
from jax.experimental.pallas import tpu as pltpu

F32 = jnp.float32
BF16 = jnp.bfloat16

D_MODEL = 1024
EPS = 1e-6
ROPE_THETA = 10000.0
NEG_INF = -1e30
LOG2E = math.log2(math.e)
GRID_W = 64
LANES = 128
A_HEADS = 8
A_HEAD_DIM = 64
DILATED_CONFIGS = ((128, 1), (512, 4), (2048, 16))
A_RADIUS = 64
A_REACH = max(w // 2 for w, _ in DILATED_CONFIGS)
MLA_HEADS = 8
MLA_Q_RANK = 256
MLA_KV_RANK = 128
MLA_NOPE_DIM = 64
MLA_ROPE_DIM = 32
MLA_V_DIM = 64
C_HEADS = 16
C_KV_HEADS = 4
C_GROUPS = C_HEADS // C_KV_HEADS
C_HEAD_DIM = 64
A_WIDTH = A_HEADS * A_HEAD_DIM
B_WIDTH = MLA_HEADS * MLA_V_DIM
AB_WIDTH = A_WIDTH + B_WIDTH
C_WIDTH = C_HEADS * C_HEAD_DIM
MLA_WIDE = MLA_HEADS * LANES
COL_QKV_A, COL_LATENT, COL_GATE_AB = 0, 1536, 2048
IN_AB_PAD = 3072
KR_LANE = MLA_NOPE_DIM
COL_GATE_C, COL_QC, COL_KC, COL_VC = 0, 1024, 2048, 2560
IN_C_PAD = 3072

VMEM_LIMIT = 56 * 1024 * 1024
ROW_TILE = 512


def _params(sem):
    return pltpu.CompilerParams(dimension_semantics=sem, vmem_limit_bytes=VMEM_LIMIT)


def _rms(x, g):
    ms = jnp.mean(x * x, axis=-1, keepdims=True)
    return x * lax.rsqrt(ms + EPS) * g


def _rope_lanes(x, c, sa, sb):
    return x * c + pltpu.roll(x, 16, 1) * sa + pltpu.roll(x, 112, 1) * sb


def _ones_lane_row(width, lane):
    idx = lax.broadcasted_iota(jnp.int32, (1, width), 1)
    return jnp.where(idx % LANES == lane, 1.0, 0.0).astype(F32)


def _pipelined(jobs):
    nxt = jobs[0][0]()
    for i, (_, consume) in enumerate(jobs):
        cur = nxt
        nxt = jobs[i + 1][0]() if i + 1 < len(jobs) else None
        consume(cur)


def _in_ab_kernel(x_ref, g_ref, w_ref, gq_ref, gkv_ref, wuq_ref, wuk_ref, wuv_ref,
                  c_ref, sa_ref, sb_ref, qkv_ref, gate_ref, qb_ref, kb_ref, vb_ref, *, q_scale):
    h = _rms(x_ref[...], g_ref[...]).astype(BF16)
    c, sa, sb = c_ref[...], sa_ref[...], sb_ref[...]

    def proj(c0):
        return lambda: jnp.dot(h, w_ref[:, c0:c0 + 512], preferred_element_type=F32)

    def store(ref, c0):
        def consume(v):
            ref[:, c0:c0 + 512] = v.astype(BF16)
        return consume

    latent = {}

    def up_project(lat):
        cq = _rms(lat[:, :MLA_Q_RANK], gq_ref[...]).astype(BF16)
        ckv = _rms(lat[:, MLA_Q_RANK:MLA_Q_RANK + MLA_KV_RANK], gkv_ref[...]).astype(BF16)
        latent["kr"] = _rope_lanes(lat[:, MLA_Q_RANK + MLA_KV_RANK:], c, sa, sb)
        latent["qf"] = jnp.dot(cq, wuq_ref[...], preferred_element_type=F32)
        latent["kf"] = jnp.dot(ckv, wuk_ref[...], preferred_element_type=F32)
        vf = jnp.dot(ckv, wuv_ref[...], preferred_element_type=F32)
        vb_ref[...] = (vf + _ones_lane_row(MLA_WIDE, MLA_V_DIM)).astype(BF16)

    def rope_heads(ref, c0, heads):
        store_plain = store(ref, c0)

        def consume(v):
            store_plain(v)
            for hd in heads:
                sl = slice(hd * LANES, (hd + 1) * LANES)
                qb_ref[:, sl] = (_rope_lanes(latent["qf"][:, sl], c, sa, sb) * q_scale).astype(BF16)
                kb_ref[:, sl] = (latent["kf"][:, sl] + latent["kr"]).astype(BF16)
        return consume

    half = MLA_HEADS // 2
    _pipelined([
        (proj(COL_LATENT), up_project),
        (proj(COL_QKV_A), store(qkv_ref, 0)),
        (proj(COL_QKV_A + 512), store(qkv_ref, 512)),
        (proj(COL_QKV_A + 1024), rope_heads(qkv_ref, 1024, range(0, half))),
        (proj(COL_GATE_AB), rope_heads(gate_ref, 0, range(half, MLA_HEADS))),
        (proj(COL_GATE_AB + 512), store(gate_ref, 512)),
    ])


def in_proj_ab(x, seq, g, w, gq, gkv, wuq, wuk, wuv, tabs):
    m, d = x.shape
    tiles_per_seq = seq // ROW_TILE
    row = lambda i: (i, 0)
    const = lambda i: (0, 0)
    pos = lambda i: (i % tiles_per_seq, 0)
    out_w = (COL_LATENT, AB_WIDTH, MLA_WIDE, MLA_WIDE, MLA_WIDE)
    return pl.pallas_call(
        functools.partial(_in_ab_kernel, q_scale=(MLA_NOPE_DIM + MLA_ROPE_DIM) ** -0.5 * LOG2E),
        grid=(m // ROW_TILE,),
        in_specs=[pl.BlockSpec((ROW_TILE, d), row),
                  pl.BlockSpec((1, d), const),
                  pl.BlockSpec(w.shape, const),
                  pl.BlockSpec((1, MLA_Q_RANK), const),
                  pl.BlockSpec((1, MLA_KV_RANK), const),
                  pl.BlockSpec(wuq.shape, const),
                  pl.BlockSpec(wuk.shape, const),
                  pl.BlockSpec(wuv.shape, const)] + [pl.BlockSpec((ROW_TILE, LANES), pos)] * 3,
        out_specs=[pl.BlockSpec((ROW_TILE, n), row) for n in out_w],
        out_shape=[jax.ShapeDtypeStruct((m, n), BF16) for n in out_w],
        compiler_params=_params(("parallel",)),
        name="in_proj_ab",
    )(x, g.reshape(1, d), w, gq.reshape(1, -1), gkv.reshape(1, -1), wuq, wuk, wuv, *tabs)


def _in_c_kernel(x_ref, g_ref, w_ref, gq_ref, gk_ref, bd_ref, c_ref, sa_ref, sb_ref,
                 gate_ref, q_ref, kk_ref, vv_ref, *, q_scale):
    h = _rms(x_ref[...], g_ref[...]).astype(BF16)
    c, sa, sb = c_ref[...], sa_ref[...], sb_ref[...]
    bd = bd_ref[...]

    def proj(c0):
        return lambda: jnp.dot(h, w_ref[:, c0:c0 + 512], preferred_element_type=F32)

    def head_norm_rope(x, gain, scale):
        sq = x * x
        hi = sq.astype(BF16)
        lo = (sq - hi.astype(F32)).astype(BF16)
        ss = (jnp.dot(hi, bd, preferred_element_type=F32) +
              jnp.dot(lo, bd, preferred_element_type=F32))
        xn = x * lax.rsqrt(ss * (1.0 / C_HEAD_DIM) + EPS) * gain
        halves = [_rope_lanes(xn[:, s:s + LANES], c, sa, sb) for s in (0, LANES)]
        return (jnp.concatenate(halves, axis=1) * scale).astype(BF16)

    def store(ref, c0):
        def consume(v):
            ref[:, c0:c0 + 512] = v.astype(BF16)
        return consume

    def store_heads(ref, c0, gain_ref, scale):
        def consume(v):
            for s in (0, 256):
                ref[:, c0 + s:c0 + s + 256] = head_norm_rope(v[:, s:s + 256], gain_ref[...], scale)
        return consume

    def store_values(v):
        vv_ref[...] = (v + _ones_lane_row(v.shape[1], C_HEAD_DIM)).astype(BF16)

    _pipelined([
        (proj(COL_QC), store_heads(q_ref, 0, gq_ref, q_scale)),
        (proj(COL_GATE_C), store(gate_ref, 0)),
        (proj(COL_QC + 512), store_heads(q_ref, 512, gq_ref, q_scale)),
        (proj(COL_GATE_C + 512), store(gate_ref, 512)),
        (proj(COL_KC), store_heads(kk_ref, 0, gk_ref, 1.0)),
        (proj(COL_VC), store_values),
    ])


def in_proj_c(x, seq, g, w, gq, gk, bd, tabs):
    m, d = x.shape
    tiles_per_seq = seq // ROW_TILE
    row = lambda i: (i, 0)
    const = lambda i: (0, 0)
    pos = lambda i: (i % tiles_per_seq, 0)
    out_w = (C_WIDTH, C_WIDTH, C_KV_HEADS * LANES, C_KV_HEADS * LANES)
    tile4 = lambda v: jnp.tile(v.reshape(1, -1), (1, 4))
    return pl.pallas_call(
        functools.partial(_in_c_kernel, q_scale=C_HEAD_DIM ** -0.5 * LOG2E),
        grid=(m // ROW_TILE,),
        in_specs=[pl.BlockSpec((ROW_TILE, d), row),
                  pl.BlockSpec((1, d), const),
                  pl.BlockSpec(w.shape, const),
                  pl.BlockSpec((1, 256), const),
                  pl.BlockSpec((1, 256), const),
                  pl.BlockSpec((256, 256), const)] + [pl.BlockSpec((ROW_TILE, LANES), pos)] * 3,
        out_specs=[pl.BlockSpec((ROW_TILE, n), row) for n in out_w],
        out_shape=[jax.ShapeDtypeStruct((m, n), BF16) for n in out_w],
        compiler_params=_params(("parallel",)),
        name="in_proj_c",
    )(x, g.reshape(1, d), w, tile4(gq), tile4(gk), bd, *tabs)


def _online_softmax(q, k_ref, v_ref, m_ref, acc_ref, *, tk, unroll, v_groups):
    m_ref[...] = jnp.full(m_ref.shape, NEG_INF, F32)
    acc_ref[...] = jnp.zeros(acc_ref.shape, F32)

    def body(j, carry):
        start = pl.multiple_of(j * tk, tk)
        k = k_ref[0, pl.ds(start, tk), :]
        s = lax.dot_general(q, k, (((1,), (1,)), ((), ())), preferred_element_type=F32)
        m_old = m_ref[...]
        m_new = jnp.maximum(m_old, jnp.max(s, axis=-1, keepdims=True))
        alpha = jnp.exp2(m_old - m_new)
        p = jnp.exp2(s - jnp.tile(m_new, (1, tk // LANES))).astype(BF16)
        for r0, r1, l0 in v_groups:
            v = v_ref[0, pl.ds(start, tk), l0:l0 + LANES]
            acc_ref[r0:r1] = alpha[r0:r1] * acc_ref[r0:r1] + jnp.dot(
                p[r0:r1], v, preferred_element_type=F32)
        m_ref[...] = m_new
        return carry

    lax.fori_loop(0, k_ref.shape[1] // tk, body, 0, unroll=unroll)


def _normalised(acc, dv):
    return acc[:, :dv] / acc[:, dv:dv + 1]


def _gqa_kernel(q_ref, k_ref, v_ref, o_ref, m_ref, acc_ref, *, tk, unroll):
    tq = q_ref.shape[1]
    q = q_ref[0].astype(F32)
    low = lax.broadcasted_iota(jnp.int32, (tq, LANES), 1) < C_HEAD_DIM
    parts = []
    for half in (q[:, :LANES], q[:, LANES:]):
        parts += [jnp.where(low, half, 0.0), jnp.where(low, 0.0, half)]
    qs = jnp.concatenate(parts, axis=0).astype(BF16)
    _online_softmax(qs, k_ref, v_ref, m_ref, acc_ref, tk=tk, unroll=unroll,
                    v_groups=((0, C_GROUPS * tq, 0),))
    acc = acc_ref[...]
    o_ref[0] = jnp.concatenate(
        [_normalised(acc[g * tq:(g + 1) * tq], C_HEAD_DIM) for g in range(C_GROUPS)],
        axis=1).astype(o_ref.dtype)


def gqa_attention(q, kk, vv, *, tq, tk, unroll):
    b, t, _ = q.shape
    rows = C_GROUPS * tq
    return pl.pallas_call(
        functools.partial(_gqa_kernel, tk=tk, unroll=unroll),
        grid=(b, C_KV_HEADS, t // tq),
        in_specs=[pl.BlockSpec((1, tq, 256), lambda i, j, n: (i, n, j)),
                  pl.BlockSpec((1, t, LANES), lambda i, j, n: (i, 0, j)),
                  pl.BlockSpec((1, t, LANES), lambda i, j, n: (i, 0, j))],
        out_specs=pl.BlockSpec((1, tq, 256), lambda i, j, n: (i, n, j)),
        out_shape=jax.ShapeDtypeStruct((b, t, C_WIDTH), BF16),
        scratch_shapes=[pltpu.VMEM((rows, LANES), F32), pltpu.VMEM((rows, LANES), F32)],
        compiler_params=_params(("parallel", "parallel", "arbitrary")),
        name="gqa_attention",
    )(q, kk, vv)


def _mla_kernel(q_ref, k_ref, v_ref, o_ref, m_ref, acc_ref, *, tk, unroll):
    tq = q_ref.shape[1]
    q = q_ref[0]
    zeros = jnp.zeros((tq, LANES), q.dtype)
    qs = jnp.concatenate([jnp.concatenate([q[:, :LANES], zeros], axis=1),
                          jnp.concatenate([zeros, q[:, LANES:]], axis=1)], axis=0)
    _online_softmax(qs, k_ref, v_ref, m_ref, acc_ref, tk=tk, unroll=unroll,
                    v_groups=((0, tq, 0), (tq, 2 * tq, LANES)))
    acc = acc_ref[...]
    o_ref[0] = jnp.concatenate(
        [_normalised(acc[g * tq:(g + 1) * tq], MLA_V_DIM) for g in range(2)],
        axis=1).astype(o_ref.dtype)


def mla_attention(qb, kb, vb, *, tq, tk, unroll):
    b, t, _ = qb.shape
    rows = 2 * tq
    return pl.pallas_call(
        functools.partial(_mla_kernel, tk=tk, unroll=unroll),
        grid=(b, MLA_HEADS // 2, t // tq),
        in_specs=[pl.BlockSpec((1, tq, 256), lambda i, j, n: (i, n, j)),
                  pl.BlockSpec((1, t, 256), lambda i, j, n: (i, 0, j)),
                  pl.BlockSpec((1, t, 256), lambda i, j, n: (i, 0, j))],
        out_specs=pl.BlockSpec((1, tq, LANES), lambda i, j, n: (i, n, j)),
        out_shape=jax.ShapeDtypeStruct((b, t, B_WIDTH), BF16),
        scratch_shapes=[pltpu.VMEM((rows, LANES), F32), pltpu.VMEM((rows, LANES), F32)],
        compiler_params=_params(("parallel", "parallel", "arbitrary")),
        name="mla_attention",
    )(qb, kb, vb)


A_TQ = 2048
A_SUB = 128
A_BAND = A_SUB + 2 * A_RADIUS
A_LOCKSTEP = 4


def _dilated_kernel(q_ref, k_ref, v_ref, b_ref, o_ref, qf_ref, kf_ref, vf_ref, oacc_ref, m_ref, l_ref):
    seq = k_ref.shape[1]
    qi = pl.program_id(2)

    @pl.when(qi == 0)
    def _():
        pad = jnp.zeros((A_REACH, LANES), F32)
        for ref, src in ((kf_ref, k_ref), (vf_ref, v_ref)):
            ref[0:A_REACH] = pad
            ref[A_REACH + seq:] = pad
            ref[A_REACH:A_REACH + seq] = src[0].astype(F32)

    qf_ref[...] = q_ref[0].astype(F32)
    t0 = qi * A_TQ
    low = lax.broadcasted_iota(jnp.int32, (A_SUB, LANES), 1) < A_HEAD_DIM
    key_col = lax.broadcasted_iota(jnp.int32, (1, A_BAND), 1)

    for ci, (_, dil) in enumerate(DILATED_CONFIGS):
        shift = dil.bit_length() - 1

        def body(it, carry, ci=ci, dil=dil, shift=shift):
            subs = [it * A_LOCKSTEP + u for u in range(A_LOCKSTEP)]
            bases = [(i & (dil - 1)) + ((i >> shift) << (shift + 7)) for i in subs]
            firsts = [t0 + base - A_RADIUS * dil for base in bases]
            if dil > 1:
                rows = [pl.ds(base, A_SUB, stride=dil) for base in bases]
                keys = [pl.ds(first + A_REACH, A_BAND, stride=dil) for first in firsts]
            else:
                rows = [pl.ds(pl.multiple_of(base, A_SUB), A_SUB) for base in bases]
                keys = [pl.ds(first + A_REACH, A_BAND) for first in firsts]
            qs = []
            for r in rows:
                q = qf_ref[r, :]
                qs.append(jnp.concatenate([jnp.where(low, q, 0.0), jnp.where(low, 0.0, q)],
                                          axis=0).astype(BF16))
            kw = [kf_ref[k, :].astype(BF16) for k in keys]
            vw = [vf_ref[k, :].astype(BF16) for k in keys]
            s = [lax.dot_general(a, b_, (((1,), (1,)), ((), ())), preferred_element_type=F32)
                 for a, b_ in zip(qs, kw)]
            bias = b_ref[0, ci]
            masked = []
            for s_u, first in zip(s, firsts):
                key_pos = first + dil * key_col
                valid = jnp.where((key_pos >= 0) & (key_pos < seq), 0.0, NEG_INF)
                masked.append(s_u + bias + valid)
            m_blk = [jnp.max(x, axis=-1, keepdims=True) for x in masked]
            p = [jnp.exp2(x - m) for x, m in zip(masked, m_blk)]
            l_blk = [jnp.sum(x, axis=-1, keepdims=True) for x in p]
            o_blk = [jnp.dot(x.astype(BF16), v, preferred_element_type=F32) for x, v in zip(p, vw)]
            for r, o_u, m_u, l_u in zip(rows, o_blk, m_blk, l_blk):
                o2 = jnp.where(low, o_u[:A_SUB], o_u[A_SUB:])
                m2 = jnp.where(low, m_u[:A_SUB], m_u[A_SUB:])
                l2 = jnp.where(low, l_u[:A_SUB], l_u[A_SUB:])
                if ci == 0:
                    oacc_ref[r, :] = o2
                    m_ref[r, :] = m2
                    l_ref[r, :] = l2
                else:
                    m_old = m_ref[r, :]
                    m_new = jnp.maximum(m_old, m2)
                    a_old = jnp.exp2(m_old - m_new)
                    a_blk = jnp.exp2(m2 - m_new)
                    oacc_ref[r, :] = oacc_ref[r, :] * a_old + o2 * a_blk
                    l_ref[r, :] = l_ref[r, :] * a_old + l2 * a_blk
                    m_ref[r, :] = m_new
            return carry

        lax.fori_loop(0, A_TQ // A_SUB // A_LOCKSTEP, body, 0)

    o_ref[0] = (oacc_ref[...] / l_ref[...]).astype(o_ref.dtype)


def dilated_attention(qkv, bias):
    b, t, _ = qkv.shape
    pairs = A_HEADS // 2
    state = pltpu.VMEM((A_TQ, LANES), F32)
    padded = pltpu.VMEM((t + 2 * A_REACH, LANES), F32)
    return pl.pallas_call(
        _dilated_kernel,
        grid=(b, pairs, t // A_TQ),
        in_specs=[pl.BlockSpec((1, A_TQ, LANES), lambda i, j, n: (i, n, j)),
                  pl.BlockSpec((1, t, LANES), lambda i, j, n: (i, 0, pairs + j)),
                  pl.BlockSpec((1, t, LANES), lambda i, j, n: (i, 0, 2 * pairs + j)),
                  pl.BlockSpec((1, len(DILATED_CONFIGS), 2 * A_SUB, A_BAND), lambda i, j, n: (j, 0, 0, 0))],
        out_specs=pl.BlockSpec((1, A_TQ, LANES), lambda i, j, n: (i, n, j)),
        out_shape=jax.ShapeDtypeStruct((b, t, A_WIDTH), BF16),
        scratch_shapes=[state, padded, padded, state, state, state],
        compiler_params=_params(("parallel", "parallel", "arbitrary")),
        name="dilated_attention",
    )(qkv, qkv, qkv, bias)


def _dilated_bias():
    i = jnp.arange(A_SUB, dtype=jnp.int32)[:, None]
    c = jnp.arange(A_BAND, dtype=jnp.int32)[None, :]
    hops = jnp.abs(c - A_RADIUS - i)
    slopes = 2.0 ** (-8.0 * jnp.arange(1, A_HEADS + 1, dtype=F32) / A_HEADS)
    per_cfg = []
    for _, dil in DILATED_CONFIGS:
        dist = (dil * hops).astype(F32)
        bias = jnp.where(hops <= A_RADIUS, -slopes[:, None, None] * dist[None] * LOG2E, NEG_INF)
        per_cfg.append(bias.reshape(A_HEADS // 2, 2 * A_SUB, A_BAND))
    return jnp.stack(per_cfg, axis=1)


def _gate_out_kernel(*refs, n_parts):
    parts = refs[:n_parts]
    gate_ref, x_ref, w_ref, g_ref, o_ref = refs[n_parts:]
    o = jnp.concatenate([r[...] for r in parts], axis=-1) if n_parts > 1 else parts[0][...]
    gate = gate_ref[...].astype(F32)
    y = (o.astype(F32) * (gate * jax.nn.sigmoid(gate))).astype(BF16)
    y = jnp.dot(y, w_ref[...], preferred_element_type=F32)
    o_ref[...] = x_ref[...] + _rms(y, g_ref[...])


def gate_out(parts, gate, x, w, g):
    m, d = x.shape
    row = lambda i: (i, 0)
    const = lambda i: (0, 0)
    in_specs = [pl.BlockSpec((ROW_TILE, p.shape[1]), row) for p in parts]
    in_specs += [pl.BlockSpec((ROW_TILE, gate.shape[1]), row),
                 pl.BlockSpec((ROW_TILE, d), row),
                 pl.BlockSpec(w.shape, const),
                 pl.BlockSpec((1, d), const)]
    return pl.pallas_call(
        functools.partial(_gate_out_kernel, n_parts=len(parts)),
        grid=(m // ROW_TILE,),
        in_specs=in_specs,
        out_specs=pl.BlockSpec((ROW_TILE, d), row),
        out_shape=jax.ShapeDtypeStruct((m, d), F32),
        compiler_params=_params(("parallel",)),
        name="gate_out",
    )(*parts, gate, x, w, g.reshape(1, d))


def _rope_freqs():
    half = MLA_ROPE_DIM // 2
    return ROPE_THETA ** (-jnp.arange(half, dtype=F32) / half)


def _mla_tables(seq):
    ang = jnp.arange(seq, dtype=F32)[:, None] * _rope_freqs()[None, :]
    cos, sin = jnp.cos(ang), jnp.sin(ang)
    zeros = jnp.zeros_like(cos)
    ones_nope = jnp.ones((seq, MLA_NOPE_DIM), F32)
    zeros_nope = jnp.zeros((seq, MLA_NOPE_DIM), F32)
    tail = jnp.zeros((seq, LANES - MLA_NOPE_DIM - MLA_ROPE_DIM), F32)
    c = jnp.concatenate([ones_nope, cos, cos, tail], axis=1)
    sa = jnp.concatenate([zeros_nope, zeros, sin, tail], axis=1)
    sb = jnp.concatenate([zeros_nope, -sin, zeros, tail], axis=1)
    return c, sa, sb


def _axial_tables(seq):
    t = jnp.arange(seq, dtype=jnp.int32)
    f = _rope_freqs()[None, :]
    ar = (t // GRID_W).astype(F32)[:, None] * f
    ac = (t % GRID_W).astype(F32)[:, None] * f
    zeros = jnp.zeros_like(ar)
    c = jnp.concatenate([jnp.cos(ar), jnp.cos(ar), jnp.cos(ac), jnp.cos(ac)], axis=1)
    sa = jnp.concatenate([zeros, jnp.sin(ar), zeros, jnp.sin(ac)], axis=1)
    sb = jnp.concatenate([-jnp.sin(ar), zeros, -jnp.sin(ac), zeros], axis=1)
    return tuple(jnp.tile(x, (1, 2)) for x in (c, sa, sb))


def _head_block_diag():
    idx = np.arange(256) // C_HEAD_DIM
    return jnp.asarray(idx[:, None] == idx[None, :], BF16)


def _arrange_w_in_ab(w):
    qa, ka, va, cq, ckv, kr, gate = jnp.split(
        w, list(np.cumsum([A_WIDTH, A_WIDTH, A_WIDTH, MLA_Q_RANK, MLA_KV_RANK, MLA_ROPE_DIM])), axis=1)
    d = w.shape[0]
    kr_blk = jnp.concatenate([jnp.zeros((d, KR_LANE), F32), kr,
                              jnp.zeros((d, LANES - KR_LANE - MLA_ROPE_DIM), F32)], axis=1)
    qa = qa * (A_HEAD_DIM ** -0.5 * LOG2E)
    return jnp.concatenate([qa, ka, va, cq, ckv, kr_blk, gate], axis=1).astype(BF16)


def _arrange_w_in_c(w):
    d = w.shape[0]
    q, k, v, gate = jnp.split(w, list(np.cumsum([C_WIDTH, C_KV_HEADS * C_HEAD_DIM,
                                                 C_KV_HEADS * C_HEAD_DIM])), axis=1)
    k = k.reshape(d, C_KV_HEADS, 1, C_HEAD_DIM)
    kk = jnp.broadcast_to(k, (d, C_KV_HEADS, 2, C_HEAD_DIM)).reshape(d, C_KV_HEADS * LANES)
    v = v.reshape(d, C_KV_HEADS, C_HEAD_DIM)
    vv = jnp.concatenate([v, jnp.zeros_like(v)], axis=2).reshape(d, C_KV_HEADS * LANES)
    return jnp.concatenate([gate, q, kk, vv], axis=1).astype(BF16)


def _arrange_w_uq(w):
    r = w.shape[0]
    w = w.reshape(r, MLA_HEADS, MLA_NOPE_DIM + MLA_ROPE_DIM)
    pad = jnp.zeros((r, MLA_HEADS, LANES - MLA_NOPE_DIM - MLA_ROPE_DIM), F32)
    return jnp.concatenate([w, pad], axis=2).reshape(r, MLA_WIDE).astype(BF16)


def _arrange_w_ukv(w):
    r = w.shape[0]
    w = w.reshape(r, MLA_HEADS, MLA_NOPE_DIM + MLA_V_DIM)
    pad = jnp.zeros((r, MLA_HEADS, LANES - MLA_NOPE_DIM), F32)
    wk = jnp.concatenate([w[:, :, :MLA_NOPE_DIM], pad], axis=2).reshape(r, MLA_WIDE)
    wv = jnp.concatenate([w[:, :, MLA_NOPE_DIM:], pad], axis=2).reshape(r, MLA_WIDE)
    return wk.astype(BF16), wv.astype(BF16)


def _even_layer(x, b, t, g_pre, g_post, w_in, gq, wuq, gkv, wuk, wuv, w_out, bias, tabs):
    qkv, gate, qb, kb, vb = in_proj_ab(x, t, g_pre, w_in, gq, gkv, wuq, wuk, wuv, tabs)
    three = lambda a: a.reshape(b, t, a.shape[1])
    out_a = dilated_attention(three(qkv), bias)
    out_b = mla_attention(three(qb), three(kb), three(vb), tq=512, tk=1024, unroll=4)
    return gate_out([out_a.reshape(b * t, A_WIDTH), out_b.reshape(b * t, B_WIDTH)], gate, x, w_out, g_post)


def _odd_layer(x, b, t, g_pre, g_post, w_in, q_norm, k_norm, w_out, bd, tabs):
    gate, q, kk, vv = in_proj_c(x, t, g_pre, w_in, q_norm, k_norm, bd, tabs)
    three = lambda a: a.reshape(b, t, a.shape[1])
    o = gqa_attention(three(q), three(kk), three(vv), tq=256, tk=1024, unroll=4)
    return gate_out([o.reshape(b * t, C_WIDTH)], gate, x, w_out, g_post)


def _trunk(x3, weights, consts):
    (norm_pre, norm_post, w_in_ab, mla_q_norm, w_uq, mla_kv_norm, w_uk, w_uv, w_out_ab,
     w_in_c, c_q_norm, c_k_norm, w_out_c) = weights
    b, t, d = x3.shape
    x = x3.reshape(b * t, d)
    bias, bd = consts
    mla_tabs = _mla_tables(t)
    axial_tabs = _axial_tables(t)
    depth = norm_pre.shape[0]
    for layer in range(depth):
        i = layer // 2
        if layer % 2 == 0:
            x = _even_layer(x, b, t, norm_pre[layer], norm_post[layer], w_in_ab[i], mla_q_norm[i],
                            w_uq[i], mla_kv_norm[i], w_uk[i], w_uv[i], w_out_ab[i], bias, mla_tabs)
        else:
            x = _odd_layer(x, b, t, norm_pre[layer], norm_post[layer], w_in_c[i], c_q_norm[i],
                           c_k_norm[i], w_out_c[i], bd, axial_tabs)
    return x.reshape(b, t, d)


def kernel(x_prompt, x_sample, norm_pre, norm_post, w_in_ab, mla_q_norm, w_uq, mla_kv_norm, w_ukv,
           w_out_ab, w_in_c, c_q_norm, c_k_norm, w_out_c):
    n_even, n_odd = w_in_ab.shape[0], w_in_c.shape[0]
    w_in_ab_p = jnp.stack([_arrange_w_in_ab(w_in_ab[i]) for i in range(n_even)])
    w_uq_p = jnp.stack([_arrange_w_uq(w_uq[i]) for i in range(n_even)])
    w_ukv_p = [_arrange_w_ukv(w_ukv[i]) for i in range(n_even)]
    w_uk_p = jnp.stack([p[0] for p in w_ukv_p])
    w_uv_p = jnp.stack([p[1] for p in w_ukv_p])
    w_in_c_p = jnp.stack([_arrange_w_in_c(w_in_c[i]) for i in range(n_odd)])
    weights = (norm_pre, norm_post, w_in_ab_p, mla_q_norm, w_uq_p, mla_kv_norm, w_uk_p, w_uv_p,
               w_out_ab.astype(BF16), w_in_c_p, c_q_norm, c_k_norm, w_out_c.astype(BF16))
    consts = (_dilated_bias(), _head_block_diag())
    return (_trunk(x_prompt, weights, consts), _trunk(x_sample, weights, consts))
```

```python
import functools
import math

import numpy as np
import jax
import jax.numpy as jnp
from jax import lax
from jax.experimental import pallas as pl
from jax.experimental.pallas import tpu as pltpu

F32 = jnp.float32
BF16 = jnp.bfloat16

D_MODEL = 1024
EPS = 1e-6
ROPE_THETA = 10000.0
NEG_INF = -1e30
LOG2E = math.log2(math.e)
GRID_W = 64
LANES = 128
A_HEADS = 8
A_HEAD_DIM = 64
DILATED_CONFIGS = ((128, 1), (512, 4), (2048, 16))
A_RADIUS = 64
A_REACH = max(w // 2 for w, _ in DILATED_CONFIGS)
MLA_HEADS = 8
MLA_Q_RANK = 256
MLA_KV_RANK = 128
MLA_NOPE_DIM = 64
MLA_ROPE_DIM = 32
MLA_V_DIM = 64
C_HEADS = 16
C_KV_HEADS = 4
C_GROUPS = C_HEADS // C_KV_HEADS
C_HEAD_DIM = 64
A_WIDTH = A_HEADS * A_HEAD_DIM
B_WIDTH = MLA_HEADS * MLA_V_DIM
AB_WIDTH = A_WIDTH + B_WIDTH
C_WIDTH = C_HEADS * C_HEAD_DIM
MLA_WIDE = MLA_HEADS * LANES
COL_QKV_A, COL_LATENT, COL_GATE_AB = 0, 1536, 2048
IN_AB_PAD = 3072
KR_LANE = MLA_NOPE_DIM
COL_GATE_C, COL_QC, COL_KC, COL_VC = 0, 1024, 2048, 2560
IN_C_PAD = 3072

VMEM_LIMIT = 56 * 1024 * 1024
ROW_TILE = 512


def _params(sem):
    return pltpu.CompilerParams(dimension_semantics=sem, vmem_limit_bytes=VMEM_LIMIT)


def _rms(x, g):
    ms = jnp.mean(x * x, axis=-1, keepdims=True)
    return x * lax.rsqrt(ms + EPS) * g


def _rope_lanes(x, c, sa, sb):
    return x * c + pltpu.roll(x, 16, 1) * sa + pltpu.roll(x, 112, 1) * sb


def _ones_lane_row(width, lane):
    idx = lax.broadcasted_iota(jnp.int32, (1, width), 1)
    return jnp.where(idx % LANES == lane, 1.0, 0.0).astype(F32)


def _pipelined(jobs):
    nxt = jobs[0][0]()
    for i, (_, consume) in enumerate(jobs):
        cur = nxt
        nxt = jobs[i + 1][0]() if i + 1 < len(jobs) else None
        consume(cur)


def _in_ab_kernel(x_ref, g_ref, w_ref, gq_ref, gkv_ref, wuq_ref, wuk_ref, wuv_ref,
                  c_ref, sa_ref, sb_ref, qkv_ref, gate_ref, qb_ref, kb_ref, vb_ref, *, q_scale):
    h = _rms(x_ref[...], g_ref[...]).astype(BF16)
    c, sa, sb = c_ref[...], sa_ref[...], sb_ref[...]

    def proj(c0):
        return lambda: jnp.dot(h, w_ref[:, c0:c0 + 512], preferred_element_type=F32)

    def store(ref, c0):
        def consume(v):
            ref[:, c0:c0 + 512] = v.astype(BF16)
        return consume

    latent = {}

    def up_project(lat):
        cq = _rms(lat[:, :MLA_Q_RANK], gq_ref[...]).astype(BF16)
        ckv = _rms(lat[:, MLA_Q_RANK:MLA_Q_RANK + MLA_KV_RANK], gkv_ref[...]).astype(BF16)
        latent["kr"] = _rope_lanes(lat[:, MLA_Q_RANK + MLA_KV_RANK:], c, sa, sb)
        latent["qf"] = jnp.dot(cq, wuq_ref[...], preferred_element_type=F32)
        latent["kf"] = jnp.dot(ckv, wuk_ref[...], preferred_element_type=F32)
        vf = jnp.dot(ckv, wuv_ref[...], preferred_element_type=F32)
        vb_ref[...] = (vf + _ones_lane_row(MLA_WIDE, MLA_V_DIM)).astype(BF16)

    def rope_heads(ref, c0, heads):
        store_plain = store(ref, c0)

        def consume(v):
            store_plain(v)
            for hd in heads:
                sl = slice(hd * LANES, (hd + 1) * LANES)
                qb_ref[:, sl] = (_rope_lanes(latent["qf"][:, sl], c, sa, sb) * q_scale).astype(BF16)
                kb_ref[:, sl] = (latent["kf"][:, sl] + latent["kr"]).astype(BF16)
        return consume

    half = MLA_HEADS // 2
    _pipelined([
        (proj(COL_LATENT), up_project),
        (proj(COL_QKV_A), store(qkv_ref, 0)),
        (proj(COL_QKV_A + 512), store(qkv_ref, 512)),
        (proj(COL_QKV_A + 1024), rope_heads(qkv_ref, 1024, range(0, half))),
        (proj(COL_GATE_AB), rope_heads(gate_ref, 0, range(half, MLA_HEADS))),
        (proj(COL_GATE_AB + 512), store(gate_ref, 512)),
    ])


def in_proj_ab(x, seq, g, w, gq, gkv, wuq, wuk, wuv, tabs):
    m, d = x.shape
    tiles_per_seq = seq // ROW_TILE
    row = lambda i: (i, 0)
    const = lambda i: (0, 0)
    pos = lambda i: (i % tiles_per_seq, 0)
    out_w = (COL_LATENT, AB_WIDTH, MLA_WIDE, MLA_WIDE, MLA_WIDE)
    return pl.pallas_call(
        functools.partial(_in_ab_kernel, q_scale=(MLA_NOPE_DIM + MLA_ROPE_DIM) ** -0.5 * LOG2E),
        grid=(m // ROW_TILE,),
        in_specs=[pl.BlockSpec((ROW_TILE, d), row),
                  pl.BlockSpec((1, d), const),
                  pl.BlockSpec(w.shape, const),
                  pl.BlockSpec((1, MLA_Q_RANK), const),
                  pl.BlockSpec((1, MLA_KV_RANK), const),
                  pl.BlockSpec(wuq.shape, const),
                  pl.BlockSpec(wuk.shape, const),
                  pl.BlockSpec(wuv.shape, const)] + [pl.BlockSpec((ROW_TILE, LANES), pos)] * 3,
        out_specs=[pl.BlockSpec((ROW_TILE, n), row) for n in out_w],
        out_shape=[jax.ShapeDtypeStruct((m, n), BF16) for n in out_w],
        compiler_params=_params(("parallel",)),
        name="in_proj_ab",
    )(x, g.reshape(1, d), w, gq.reshape(1, -1), gkv.reshape(1, -1), wuq, wuk, wuv, *tabs)


def _in_c_kernel(x_ref, g_ref, w_ref, gq_ref, gk_ref, bd_ref, c_ref, sa_ref, sb_ref,
                 gate_ref, q_ref, kk_ref, vv_ref, *, q_scale):
    h = _rms(x_ref[...], g_ref[...]).astype(BF16)
    c, sa, sb = c_ref[...], sa_ref[...], sb_ref[...]
    bd = bd_ref[...]

    def proj(c0):
        return lambda: jnp.dot(h, w_ref[:, c0:c0 + 512], preferred_element_type=F32)

    def head_norm_rope(x, gain, scale):
        sq = x * x
        hi = sq.astype(BF16)
        lo = (sq - hi.astype(F32)).astype(BF16)
        ss = (jnp.dot(hi, bd, preferred_element_type=F32) +
              jnp.dot(lo, bd, preferred_element_type=F32))
        xn = x * lax.rsqrt(ss * (1.0 / C_HEAD_DIM) + EPS) * gain
        halves = [_rope_lanes(xn[:, s:s + LANES], c, sa, sb) for s in (0, LANES)]
        return (jnp.concatenate(halves, axis=1) * scale).astype(BF16)

    def store(ref, c0):
        def consume(v):
            ref[:, c0:c0 + 512] = v.astype(BF16)
        return consume

    def store_heads(ref, c0, gain_ref, scale):
        def consume(v):
            for s in (0, 256):
                ref[:, c0 + s:c0 + s + 256] = head_norm_rope(v[:, s:s + 256], gain_ref[...], scale)
        return consume

    def store_values(v):
        vv_ref[...] = (v + _ones_lane_row(v.shape[1], C_HEAD_DIM)).astype(BF16)

    _pipelined([
        (proj(COL_QC), store_heads(q_ref, 0, gq_ref, q_scale)),
        (proj(COL_GATE_C), store(gate_ref, 0)),
        (proj(COL_QC + 512), store_heads(q_ref, 512, gq_ref, q_scale)),
        (proj(COL_GATE_C + 512), store(gate_ref, 512)),
        (proj(COL_KC), store_heads(kk_ref, 0, gk_ref, 1.0)),
        (proj(COL_VC), store_values),
    ])


def in_proj_c(x, seq, g, w, gq, gk, bd, tabs):
    m, d = x.shape
    tiles_per_seq = seq // ROW_TILE
    row = lambda i: (i, 0)
    const = lambda i: (0, 0)
    pos = lambda i: (i % tiles_per_seq, 0)
    out_w = (C_WIDTH, C_WIDTH, C_KV_HEADS * LANES, C_KV_HEADS * LANES)
    tile4 = lambda v: jnp.tile(v.reshape(1, -1), (1, 4))
    return pl.pallas_call(
        functools.partial(_in_c_kernel, q_scale=C_HEAD_DIM ** -0.5 * LOG2E),
        grid=(m // ROW_TILE,),
        in_specs=[pl.BlockSpec((ROW_TILE, d), row),
                  pl.BlockSpec((1, d), const),
                  pl.BlockSpec(w.shape, const),
                  pl.BlockSpec((1, 256), const),
                  pl.BlockSpec((1, 256), const),
                  pl.BlockSpec((256, 256), const)] + [pl.BlockSpec((ROW_TILE, LANES), pos)] * 3,
        out_specs=[pl.BlockSpec((ROW_TILE, n), row) for n in out_w],
        out_shape=[jax.ShapeDtypeStruct((m, n), BF16) for n in out_w],
        compiler_params=_params(("parallel",)),
        name="in_proj_c",
    )(x, g.reshape(1, d), w, tile4(gq), tile4(gk), bd, *tabs)


def _online_softmax(q, k_ref, v_ref, m_ref, acc_ref, *, tk, unroll, v_groups):
    m_ref[...] = jnp.full(m_ref.shape, NEG_INF, F32)
    acc_ref[...] = jnp.zeros(acc_ref.shape, F32)

    def body(j, carry):
        start = pl.multiple_of(j * tk, tk)
        k = k_ref[0, pl.ds(start, tk), :]
        s = lax.dot_general(q, k, (((1,), (1,)), ((), ())), preferred_element_type=F32)
        m_old = m_ref[...]
        m_new = jnp.maximum(m_old, jnp.max(s, axis=-1, keepdims=True))
        alpha = jnp.exp2(m_old - m_new)
        p = jnp.exp2(s - jnp.tile(m_new, (1, tk // LANES))).astype(BF16)
        for r0, r1, l0 in v_groups:
            v = v_ref[0, pl.ds(start, tk), l0:l0 + LANES]
            acc_ref[r0:r1] = alpha[r0:r1] * acc_ref[r0:r1] + jnp.dot(
                p[r0:r1], v, preferred_element_type=F32)
        m_ref[...] = m_new
        return carry

    lax.fori_loop(0, k_ref.shape[1] // tk, body, 0, unroll=unroll)


def _normalised(acc, dv):
    return acc[:, :dv] / acc[:, dv:dv + 1]


def _gqa_kernel(q_ref, k_ref, v_ref, o_ref, m_ref, acc_ref, *, tk, unroll):
    tq = q_ref.shape[1]
    q = q_ref[0].astype(F32)
    low = lax.broadcasted_iota(jnp.int32, (tq, LANES), 1) < C_HEAD_DIM
    parts = []
    for half in (q[:, :LANES], q[:, LANES:]):
        parts += [jnp.where(low, half, 0.0), jnp.where(low, 0.0, half)]
    qs = jnp.concatenate(parts, axis=0).astype(BF16)
    _online_softmax(qs, k_ref, v_ref, m_ref, acc_ref, tk=tk, unroll=unroll,
                    v_groups=((0, C_GROUPS * tq, 0),))
    acc = acc_ref[...]
    o_ref[0] = jnp.concatenate(
        [_normalised(acc[g * tq:(g + 1) * tq], C_HEAD_DIM) for g in range(C_GROUPS)],
        axis=1).astype(o_ref.dtype)


def gqa_attention(q, kk, vv, *, tq, tk, unroll):
    b, t, _ = q.shape
    rows = C_GROUPS * tq
    return pl.pallas_call(
        functools.partial(_gqa_kernel, tk=tk, unroll=unroll),
        grid=(b, C_KV_HEADS, t // tq),
        in_specs=[pl.BlockSpec((1, tq, 256), lambda i, j, n: (i, n, j)),
                  pl.BlockSpec((1, t, LANES), lambda i, j, n: (i, 0, j)),
                  pl.BlockSpec((1, t, LANES), lambda i, j, n: (i, 0, j))],
        out_specs=pl.BlockSpec((1, tq, 256), lambda i, j, n: (i, n, j)),
        out_shape=jax.ShapeDtypeStruct((b, t, C_WIDTH), BF16),
        scratch_shapes=[pltpu.VMEM((rows, LANES), F32), pltpu.VMEM((rows, LANES), F32)],
        compiler_params=_params(("parallel", "parallel", "arbitrary")),
        name="gqa_attention",
    )(q, kk, vv)


def _mla_kernel(q_ref, k_ref, v_ref, o_ref, m_ref, acc_ref, *, tk, unroll):
    tq = q_ref.shape[1]
    q = q_ref[0]
    zeros = jnp.zeros((tq, LANES), q.dtype)
    qs = jnp.concatenate([jnp.concatenate([q[:, :LANES], zeros], axis=1),
                          jnp.concatenate([zeros, q[:, LANES:]], axis=1)], axis=0)
    _online_softmax(qs, k_ref, v_ref, m_ref, acc_ref, tk=tk, unroll=unroll,
                    v_groups=((0, tq, 0), (tq, 2 * tq, LANES)))
    acc = acc_ref[...]
    o_ref[0] = jnp.concatenate(
        [_normalised(acc[g * tq:(g + 1) * tq], MLA_V_DIM) for g in range(2)],
        axis=1).astype(o_ref.dtype)


def mla_attention(qb, kb, vb, *, tq, tk, unroll):
    b, t, _ = qb.shape
    rows = 2 * tq
    return pl.pallas_call(
        functools.partial(_mla_kernel, tk=tk, unroll=unroll),
        grid=(b, MLA_HEADS // 2, t // tq),
        in_specs=[pl.BlockSpec((1, tq, 256), lambda i, j, n: (i, n, j)),
                  pl.BlockSpec((1, t, 256), lambda i, j, n: (i, 0, j)),
                  pl.BlockSpec((1, t, 256), lambda i, j, n: (i, 0, j))],
        out_specs=pl.BlockSpec((1, tq, LANES), lambda i, j, n: (i, n, j)),
        out_shape=jax.ShapeDtypeStruct((b, t, B_WIDTH), BF16),
        scratch_shapes=[pltpu.VMEM((rows, LANES), F32), pltpu.VMEM((rows, LANES), F32)],
        compiler_params=_params(("parallel", "parallel", "arbitrary")),
        name="mla_attention",
    )(qb, kb, vb)


A_TQ = 2048
A_SUB = 128
A_BAND = A_SUB + 2 * A_RADIUS
A_LOCKSTEP = 4


def _dilated_kernel(q_ref, k_ref, v_ref, b_ref, o_ref, qf_ref, kf_ref, vf_ref, oacc_ref, m_ref, l_ref):
    seq = k_ref.shape[1]
    qi = pl.program_id(2)

    @pl.when(qi == 0)
    def _():
        pad = jnp.zeros((A_REACH, LANES), F32)
        for ref, src in ((kf_ref, k_ref), (vf_ref, v_ref)):
            ref[0:A_REACH] = pad
            ref[A_REACH + seq:] = pad
            ref[A_REACH:A_REACH + seq] = src[0].astype(F32)

    qf_ref[...] = q_ref[0].astype(F32)
    t0 = qi * A_TQ
    low = lax.broadcasted_iota(jnp.int32, (A_SUB, LANES), 1) < A_HEAD_DIM
    key_col = lax.broadcasted_iota(jnp.int32, (1, A_BAND), 1)

    for ci, (_, dil) in enumerate(DILATED_CONFIGS):
        shift = dil.bit_length() - 1

        def body(it, carry, ci=ci, dil=dil, shift=shift):
            subs = [it * A_LOCKSTEP + u for u in range(A_LOCKSTEP)]
            bases = [(i & (dil - 1)) + ((i >> shift) << (shift + 7)) for i in subs]
            firsts = [t0 + base - A_RADIUS * dil for base in bases]
            if dil > 1:
                rows = [pl.ds(base, A_SUB, stride=dil) for base in bases]
                keys = [pl.ds(first + A_REACH, A_BAND, stride=dil) for first in firsts]
            else:
                rows = [pl.ds(pl.multiple_of(base, A_SUB), A_SUB) for base in bases]
                keys = [pl.ds(first + A_REACH, A_BAND) for first in firsts]
            qs = []
            for r in rows:
                q = qf_ref[r, :]
                qs.append(jnp.concatenate([jnp.where(low, q, 0.0), jnp.where(low, 0.0, q)],
                                          axis=0).astype(BF16))
            kw = [kf_ref[k, :].astype(BF16) for k in keys]
            vw = [vf_ref[k, :].astype(BF16) for k in keys]
            s = [lax.dot_general(a, b_, (((1,), (1,)), ((), ())), preferred_element_type=F32)
                 for a, b_ in zip(qs, kw)]
            bias = b_ref[0, ci]
            masked = []
            for s_u, first in zip(s, firsts):
                key_pos = first + dil * key_col
                valid = jnp.where((key_pos >= 0) & (key_pos < seq), 0.0, NEG_INF)
                masked.append(s_u + bias + valid)
            m_blk = [jnp.max(x, axis=-1, keepdims=True) for x in masked]
            p = [jnp.exp2(x - m) for x, m in zip(masked, m_blk)]
            l_blk = [jnp.sum(x, axis=-1, keepdims=True) for x in p]
            o_blk = [jnp.dot(x.astype(BF16), v, preferred_element_type=F32) for x, v in zip(p, vw)]
            for r, o_u, m_u, l_u in zip(rows, o_blk, m_blk, l_blk):
                o2 = jnp.where(low, o_u[:A_SUB], o_u[A_SUB:])
                m2 = jnp.where(low, m_u[:A_SUB], m_u[A_SUB:])
                l2 = jnp.where(low, l_u[:A_SUB], l_u[A_SUB:])
                if ci == 0:
                    oacc_ref[r, :] = o2
                    m_ref[r, :] = m2
                    l_ref[r, :] = l2
                else:
                    m_old = m_ref[r, :]
                    m_new = jnp.maximum(m_old, m2)
                    a_old = jnp.exp2(m_old - m_new)
                    a_blk = jnp.exp2(m2 - m_new)
                    oacc_ref[r, :] = oacc_ref[r, :] * a_old + o2 * a_blk
                    l_ref[r, :] = l_ref[r, :] * a_old + l2 * a_blk
                    m_ref[r, :] = m_new
            return carry

        lax.fori_loop(0, A_TQ // A_SUB // A_LOCKSTEP, body, 0)

    o_ref[0] = (oacc_ref[...] / l_ref[...]).astype(o_ref.dtype)


def dilated_attention(qkv, bias):
    b, t, _ = qkv.shape
    pairs = A_HEADS // 2
    state = pltpu.VMEM((A_TQ, LANES), F32)
    padded = pltpu.VMEM((t + 2 * A_REACH, LANES), F32)
    return pl.pallas_call(
        _dilated_kernel,
        grid=(b, pairs, t // A_TQ),
        in_specs=[pl.BlockSpec((1, A_TQ, LANES), lambda i, j, n: (i, n, j)),
                  pl.BlockSpec((1, t, LANES), lambda i, j, n: (i, 0, pairs + j)),
                  pl.BlockSpec((1, t, LANES), lambda i, j, n: (i, 0, 2 * pairs + j)),
                  pl.BlockSpec((1, len(DILATED_CONFIGS), 2 * A_SUB, A_BAND), lambda i, j, n: (j, 0, 0, 0))],
        out_specs=pl.BlockSpec((1, A_TQ, LANES), lambda i, j, n: (i, n, j)),
        out_shape=jax.ShapeDtypeStruct((b, t, A_WIDTH), BF16),
        scratch_shapes=[state, padded, padded, state, state, state],
        compiler_params=_params(("parallel", "parallel", "arbitrary")),
        name="dilated_attention",
    )(qkv, qkv, qkv, bias)


def _dilated_bias():
    i = jnp.arange(A_SUB, dtype=jnp.int32)[:, None]
    c = jnp.arange(A_BAND, dtype=jnp.int32)[None, :]
    hops = jnp.abs(c - A_RADIUS - i)
    slopes = 2.0 ** (-8.0 * jnp.arange(1, A_HEADS + 1, dtype=F32) / A_HEADS)
    per_cfg = []
    for _, dil in DILATED_CONFIGS:
        dist = (dil * hops).astype(F32)
        bias = jnp.where(hops <= A_RADIUS, -slopes[:, None, None] * dist[None] * LOG2E, NEG_INF)
        per_cfg.append(bias.reshape(A_HEADS // 2, 2 * A_SUB, A_BAND))
    return jnp.stack(per_cfg, axis=1)


def _gate_out_kernel(*refs, n_parts):
    parts = refs[:n_parts]
    gate_ref, x_ref, w_ref, g_ref, o_ref = refs[n_parts:]
    o = jnp.concatenate([r[...] for r in parts], axis=-1) if n_parts > 1 else parts[0][...]
    gate = gate_ref[...].astype(F32)
    y = (o.astype(F32) * (gate * jax.nn.sigmoid(gate))).astype(BF16)
    y = jnp.dot(y, w_ref[...], preferred_element_type=F32)
    o_ref[...] = x_ref[...] + _rms(y, g_ref[...])


def gate_out(parts, gate, x, w, g):
    m, d = x.shape
    row = lambda i: (i, 0)
    const = lambda i: (0, 0)
    in_specs = [pl.BlockSpec((ROW_TILE, p.shape[1]), row) for p in parts]
    in_specs += [pl.BlockSpec((ROW_TILE, gate.shape[1]), row),
                 pl.BlockSpec((ROW_TILE, d), row),
                 pl.BlockSpec(w.shape, const),
                 pl.BlockSpec((1, d), const)]
    return pl.pallas_call(
        functools.partial(_gate_out_kernel, n_parts=len(parts)),
        grid=(m // ROW_TILE,),
        in_specs=in_specs,
        out_specs=pl.BlockSpec((ROW_TILE, d), row),
        out_shape=jax.ShapeDtypeStruct((m, d), F32),
        compiler_params=_params(("parallel",)),
        name="gate_out",
    )(*parts, gate, x, w, g.reshape(1, d))


def _rope_freqs():
    half = MLA_ROPE_DIM // 2
    return ROPE_THETA ** (-jnp.arange(half, dtype=F32) / half)


def _mla_tables(seq):
    ang = jnp.arange(seq, dtype=F32)[:, None] * _rope_freqs()[None, :]
    cos, sin = jnp.cos(ang), jnp.sin(ang)
    zeros = jnp.zeros_like(cos)
    ones_nope = jnp.ones((seq, MLA_NOPE_DIM), F32)
    zeros_nope = jnp.zeros((seq, MLA_NOPE_DIM), F32)
    tail = jnp.zeros((seq, LANES - MLA_NOPE_DIM - MLA_ROPE_DIM), F32)
    c = jnp.concatenate([ones_nope, cos, cos, tail], axis=1)
    sa = jnp.concatenate([zeros_nope, zeros, sin, tail], axis=1)
    sb = jnp.concatenate([zeros_nope, -sin, zeros, tail], axis=1)
    return c, sa, sb


def _axial_tables(seq):
    t = jnp.arange(seq, dtype=jnp.int32)
    f = _rope_freqs()[None, :]
    ar = (t // GRID_W).astype(F32)[:, None] * f
    ac = (t % GRID_W).astype(F32)[:, None] * f
    zeros = jnp.zeros_like(ar)
    c = jnp.concatenate([jnp.cos(ar), jnp.cos(ar), jnp.cos(ac), jnp.cos(ac)], axis=1)
    sa = jnp.concatenate([zeros, jnp.sin(ar), zeros, jnp.sin(ac)], axis=1)
    sb = jnp.concatenate([-jnp.sin(ar), zeros, -jnp.sin(ac), zeros], axis=1)
    return tuple(jnp.tile(x, (1, 2)) for x in (c, sa, sb))


def _head_block_diag():
    idx = np.arange(256) // C_HEAD_DIM
    return jnp.asarray(idx[:, None] == idx[None, :], BF16)


def _arrange_w_in_ab(w):
    qa, ka, va, cq, ckv, kr, gate = jnp.split(
        w, list(np.cumsum([A_WIDTH, A_WIDTH, A_WIDTH, MLA_Q_RANK, MLA_KV_RANK, MLA_ROPE_DIM])), axis=1)
    d = w.shape[0]
    kr_blk = jnp.concatenate([jnp.zeros((d, KR_LANE), F32), kr,
                              jnp.zeros((d, LANES - KR_LANE - MLA_ROPE_DIM), F32)], axis=1)
    qa = qa * (A_HEAD_DIM ** -0.5 * LOG2E)
    return jnp.concatenate([qa, ka, va, cq, ckv, kr_blk, gate], axis=1).astype(BF16)


def _arrange_w_in_c(w):
    d = w.shape[0]
    q, k, v, gate = jnp.split(w, list(np.cumsum([C_WIDTH, C_KV_HEADS * C_HEAD_DIM,
                                                 C_KV_HEADS * C_HEAD_DIM])), axis=1)
    k = k.reshape(d, C_KV_HEADS, 1, C_HEAD_DIM)
    kk = jnp.broadcast_to(k, (d, C_KV_HEADS, 2, C_HEAD_DIM)).reshape(d, C_KV_HEADS * LANES)
    v = v.reshape(d, C_KV_HEADS, C_HEAD_DIM)
    vv = jnp.concatenate([v, jnp.zeros_like(v)], axis=2).reshape(d, C_KV_HEADS * LANES)
    return jnp.concatenate([gate, q, kk, vv], axis=1).astype(BF16)


def _arrange_w_uq(w):
    r = w.shape[0]
    w = w.reshape(r, MLA_HEADS, MLA_NOPE_DIM + MLA_ROPE_DIM)
    pad = jnp.zeros((r, MLA_HEADS, LANES - MLA_NOPE_DIM - MLA_ROPE_DIM), F32)
    return jnp.concatenate([w, pad], axis=2).reshape(r, MLA_WIDE).astype(BF16)


def _arrange_w_ukv(w):
    r = w.shape[0]
    w = w.reshape(r, MLA_HEADS, MLA_NOPE_DIM + MLA_V_DIM)
    pad = jnp.zeros((r, MLA_HEADS, LANES - MLA_NOPE_DIM), F32)
    wk = jnp.concatenate([w[:, :, :MLA_NOPE_DIM], pad], axis=2).reshape(r, MLA_WIDE)
    wv = jnp.concatenate([w[:, :, MLA_NOPE_DIM:], pad], axis=2).reshape(r, MLA_WIDE)
    return wk.astype(BF16), wv.astype(BF16)


def _even_layer(x, b, t, g_pre, g_post, w_in, gq, wuq, gkv, wuk, wuv, w_out, bias, tabs):
    qkv, gate, qb, kb, vb = in_proj_ab(x, t, g_pre, w_in, gq, gkv, wuq, wuk, wuv, tabs)
    three = lambda a: a.reshape(b, t, a.shape[1])
    out_a = dilated_attention(three(qkv), bias)
    out_b = mla_attention(three(qb), three(kb), three(vb), tq=1024, tk=1024, unroll=4)
    return gate_out([out_a.reshape(b * t, A_WIDTH), out_b.reshape(b * t, B_WIDTH)], gate, x, w_out, g_post)


def _odd_layer(x, b, t, g_pre, g_post, w_in, q_norm, k_norm, w_out, bd, tabs):
    gate, q, kk, vv = in_proj_c(x, t, g_pre, w_in, q_norm, k_norm, bd, tabs)
    three = lambda a: a.reshape(b, t, a.shape[1])
    o = gqa_attention(three(q), three(kk), three(vv), tq=512, tk=1024, unroll=4)
    return gate_out([o.reshape(b * t, C_WIDTH)], gate, x, w_out, g_post)


def _trunk(x3, weights, consts):
    (norm_pre, norm_post, w_in_ab, mla_q_norm, w_uq, mla_kv_norm, w_uk, w_uv, w_out_ab,
     w_in_c, c_q_norm, c_k_norm, w_out_c) = weights
    b, t, d = x3.shape
    x = x3.reshape(b * t, d)
    bias, bd = consts
    mla_tabs = _mla_tables(t)
    axial_tabs = _axial_tables(t)
    depth = norm_pre.shape[0]
    for layer in range(depth):
        i = layer // 2
        if layer % 2 == 0:
            x = _even_layer(x, b, t, norm_pre[layer], norm_post[layer], w_in_ab[i], mla_q_norm[i],
                            w_uq[i], mla_kv_norm[i], w_uk[i], w_uv[i], w_out_ab[i], bias, mla_tabs)
        else:
            x = _odd_layer(x, b, t, norm_pre[layer], norm_post[layer], w_in_c[i], c_q_norm[i],
                           c_k_norm[i], w_out_c[i], bd, axial_tabs)
    return x.reshape(b, t, d)


def kernel(x_prompt, x_sample, norm_pre, norm_post, w_in_ab, mla_q_norm, w_uq, mla_kv_norm, w_ukv,
           w_out_ab, w_in_c, c_q_norm, c_k_norm, w_out_c):
    n_even, n_odd = w_in_ab.shape[0], w_in_c.shape[0]
    w_in_ab_p = jnp.stack([_arrange_w_in_ab(w_in_ab[i]) for i in range(n_even)])
    w_uq_p = jnp.stack([_arrange_w_uq(w_uq[i]) for i in range(n_even)])
    w_ukv_p = [_arrange_w_ukv(w_ukv[i]) for i in range(n_even)]
    w_uk_p = jnp.stack([p[0] for p in w_ukv_p])
    w_uv_p = jnp.stack([p[1] for p in w_ukv_p])
    w_in_c_p = jnp.stack([_arrange_w_in_c(w_in_c[i]) for i in range(n_odd)])
    weights = (norm_pre, norm_post, w_in_ab_p, mla_q_norm, w_uq_p, mla_kv_norm, w_uk_p, w_uv_p,
               w_out_ab.astype(BF16), w_in_c_p, c_q_norm, c_k_norm, w_out_c.astype(BF16))
    consts = (_dilated_bias(), _head_block_diag())
    return (_trunk(x_prompt, weights, consts), _trunk(x_sample, weights, consts))
```

```python
import functools
import math

import numpy as np
import jax
import jax.numpy as jnp
from jax import lax
from jax.experimental import pallas as pl
from jax.experimental.pallas import tpu as pltpu

F32 = jnp.float32
BF16 = jnp.bfloat16

D_MODEL = 1024
EPS = 1e-6
ROPE_THETA = 10000.0
NEG_INF = -1e30
LOG2E = math.log2(math.e)
GRID_W = 64
LANES = 128
A_HEADS = 8
A_HEAD_DIM = 64
DILATED_CONFIGS = ((128, 1), (512, 4), (2048, 16))
A_RADIUS = 64
A_REACH = max(w // 2 for w, _ in DILATED_CONFIGS)
MLA_HEADS = 8
MLA_Q_RANK = 256
MLA_KV_RANK = 128
MLA_NOPE_DIM = 64
MLA_ROPE_DIM = 32
MLA_V_DIM = 64
C_HEADS = 16
C_KV_HEADS = 4
C_GROUPS = C_HEADS // C_KV_HEADS
C_HEAD_DIM = 64
A_WIDTH = A_HEADS * A_HEAD_DIM
B_WIDTH = MLA_HEADS * MLA_V_DIM
AB_WIDTH = A_WIDTH + B_WIDTH
C_WIDTH = C_HEADS * C_HEAD_DIM
MLA_WIDE = MLA_HEADS * LANES
COL_QKV_A, COL_LATENT, COL_GATE_AB = 0, 1536, 2048
IN_AB_PAD = 3072
KR_LANE = MLA_NOPE_DIM
COL_GATE_C, COL_QC, COL_KC, COL_VC = 0, 1024, 2048, 2560
IN_C_PAD = 3072

VMEM_LIMIT = 56 * 1024 * 1024
ROW_TILE = 512


def _params(sem):
    return pltpu.CompilerParams(dimension_semantics=sem, vmem_limit_bytes=VMEM_LIMIT)


def _rms(x, g):
    ms = jnp.mean(x * x, axis=-1, keepdims=True)
    return x * lax.rsqrt(ms + EPS) * g


def _rope_lanes(x, c, sa, sb):
    return x * c + pltpu.roll(x, 16, 1) * sa + pltpu.roll(x, 112, 1) * sb


def _ones_lane_row(width, lane):
    idx = lax.broadcasted_iota(jnp.int32, (1, width), 1)
    return jnp.where(idx % LANES == lane, 1.0, 0.0).astype(F32)


def _pipelined(jobs):
    nxt = jobs[0][0]()
    for i, (_, consume) in enumerate(jobs):
        cur = nxt
        nxt = jobs[i + 1][0]() if i + 1 < len(jobs) else None
        consume(cur)


def _in_ab_kernel(x_ref, g_ref, w_ref, gq_ref, gkv_ref, wuq_ref, wuk_ref, wuv_ref,
                  c_ref, sa_ref, sb_ref, qkv_ref, gate_ref, qb_ref, kb_ref, vb_ref, *, q_scale):
    h = _rms(x_ref[...], g_ref[...]).astype(BF16)
    c, sa, sb = c_ref[...], sa_ref[...], sb_ref[...]

    def proj(c0):
        return lambda: jnp.dot(h, w_ref[:, c0:c0 + 512], preferred_element_type=F32)

    def store(ref, c0):
        def consume(v):
            ref[:, c0:c0 + 512] = v.astype(BF16)
        return consume

    latent = {}

    def up_project(lat):
        cq = _rms(lat[:, :MLA_Q_RANK], gq_ref[...]).astype(BF16)
        ckv = _rms(lat[:, MLA_Q_RANK:MLA_Q_RANK + MLA_KV_RANK], gkv_ref[...]).astype(BF16)
        latent["kr"] = _rope_lanes(lat[:, MLA_Q_RANK + MLA_KV_RANK:], c, sa, sb)
        latent["qf"] = jnp.dot(cq, wuq_ref[...], preferred_element_type=F32)
        latent["kf"] = jnp.dot(ckv, wuk_ref[...], preferred_element_type=F32)
        vf = jnp.dot(ckv, wuv_ref[...], preferred_element_type=F32)
        vb_ref[...] = (vf + _ones_lane_row(MLA_WIDE, MLA_V_DIM)).astype(BF16)

    def rope_heads(ref, c0, heads):
        store_plain = store(ref, c0)

        def consume(v):
            store_plain(v)
            for hd in heads:
                sl = slice(hd * LANES, (hd + 1) * LANES)
                qb_ref[:, sl] = (_rope_lanes(latent["qf"][:, sl], c, sa, sb) * q_scale).astype(BF16)
                kb_ref[:, sl] = (latent["kf"][:, sl] + latent["kr"]).astype(BF16)
        return consume

    half = MLA_HEADS // 2
    _pipelined([
        (proj(COL_LATENT), up_project),
        (proj(COL_QKV_A), store(qkv_ref, 0)),
        (proj(COL_QKV_A + 512), store(qkv_ref, 512)),
        (proj(COL_QKV_A + 1024), rope_heads(qkv_ref, 1024, range(0, half))),
        (proj(COL_GATE_AB), rope_heads(gate_ref, 0, range(half, MLA_HEADS))),
        (proj(COL_GATE_AB + 512), store(gate_ref, 512)),
    ])


def in_proj_ab(x, seq, g, w, gq, gkv, wuq, wuk, wuv, tabs):
    m, d = x.shape
    tiles_per_seq = seq // ROW_TILE
    row = lambda i: (i, 0)
    const = lambda i: (0, 0)
    pos = lambda i: (i % tiles_per_seq, 0)
    out_w = (COL_LATENT, AB_WIDTH, MLA_WIDE, MLA_WIDE, MLA_WIDE)
    return pl.pallas_call(
        functools.partial(_in_ab_kernel, q_scale=(MLA_NOPE_DIM + MLA_ROPE_DIM) ** -0.5 * LOG2E),
        grid=(m // ROW_TILE,),
        in_specs=[pl.BlockSpec((ROW_TILE, d), row),
                  pl.BlockSpec((1, d), const),
                  pl.BlockSpec(w.shape, const),
                  pl.BlockSpec((1, MLA_Q_RANK), const),
                  pl.BlockSpec((1, MLA_KV_RANK), const),
                  pl.BlockSpec(wuq.shape, const),
                  pl.BlockSpec(wuk.shape, const),
                  pl.BlockSpec(wuv.shape, const)] + [pl.BlockSpec((ROW_TILE, LANES), pos)] * 3,
        out_specs=[pl.BlockSpec((ROW_TILE, n), row) for n in out_w],
        out_shape=[jax.ShapeDtypeStruct((m, n), BF16) for n in out_w],
        compiler_params=_params(("parallel",)),
        name="in_proj_ab",
    )(x, g.reshape(1, d), w, gq.reshape(1, -1), gkv.reshape(1, -1), wuq, wuk, wuv, *tabs)


def _in_c_kernel(x_ref, g_ref, w_ref, gq_ref, gk_ref, bd_ref, c_ref, sa_ref, sb_ref,
                 gate_ref, q_ref, kk_ref, vv_ref, *, q_scale):
    h = _rms(x_ref[...], g_ref[...]).astype(BF16)
    c, sa, sb = c_ref[...], sa_ref[...], sb_ref[...]
    bd = bd_ref[...]

    def proj(c0):
        return lambda: jnp.dot(h, w_ref[:, c0:c0 + 512], preferred_element_type=F32)

    def head_norm_rope(x, gain, scale):
        sq = x * x
        hi = sq.astype(BF16)
        lo = (sq - hi.astype(F32)).astype(BF16)
        ss = (jnp.dot(hi, bd, preferred_element_type=F32) +
              jnp.dot(lo, bd, preferred_element_type=F32))
        xn = x * lax.rsqrt(ss * (1.0 / C_HEAD_DIM) + EPS) * gain
        halves = [_rope_lanes(xn[:, s:s + LANES], c, sa, sb) for s in (0, LANES)]
        return (jnp.concatenate(halves, axis=1) * scale).astype(BF16)

    def store(ref, c0):
        def consume(v):
            ref[:, c0:c0 + 512] = v.astype(BF16)
        return consume

    def store_heads(ref, c0, gain_ref, scale):
        def consume(v):
            for s in (0, 256):
                ref[:, c0 + s:c0 + s + 256] = head_norm_rope(v[:, s:s + 256], gain_ref[...], scale)
        return consume

    def store_values(v):
        vv_ref[...] = (v + _ones_lane_row(v.shape[1], C_HEAD_DIM)).astype(BF16)

    _pipelined([
        (proj(COL_QC), store_heads(q_ref, 0, gq_ref, q_scale)),
        (proj(COL_GATE_C), store(gate_ref, 0)),
        (proj(COL_QC + 512), store_heads(q_ref, 512, gq_ref, q_scale)),
        (proj(COL_GATE_C + 512), store(gate_ref, 512)),
        (proj(COL_KC), store_heads(kk_ref, 0, gk_ref, 1.0)),
        (proj(COL_VC), store_values),
    ])


def in_proj_c(x, seq, g, w, gq, gk, bd, tabs):
    m, d = x.shape
    tiles_per_seq = seq // ROW_TILE
    row = lambda i: (i, 0)
    const = lambda i: (0, 0)
    pos = lambda i: (i % tiles_per_seq, 0)
    out_w = (C_WIDTH, C_WIDTH, C_KV_HEADS * LANES, C_KV_HEADS * LANES)
    tile4 = lambda v: jnp.tile(v.reshape(1, -1), (1, 4))
    return pl.pallas_call(
        functools.partial(_in_c_kernel, q_scale=C_HEAD_DIM ** -0.5 * LOG2E),
        grid=(m // ROW_TILE,),
        in_specs=[pl.BlockSpec((ROW_TILE, d), row),
                  pl.BlockSpec((1, d), const),
                  pl.BlockSpec(w.shape, const),
                  pl.BlockSpec((1, 256), const),
                  pl.BlockSpec((1, 256), const),
                  pl.BlockSpec((256, 256), const)] + [pl.BlockSpec((ROW_TILE, LANES), pos)] * 3,
        out_specs=[pl.BlockSpec((ROW_TILE, n), row) for n in out_w],
        out_shape=[jax.ShapeDtypeStruct((m, n), BF16) for n in out_w],
        compiler_params=_params(("parallel",)),
        name="in_proj_c",
    )(x, g.reshape(1, d), w, tile4(gq), tile4(gk), bd, *tabs)


def _online_softmax(q, k_ref, v_ref, m_ref, acc_ref, *, tk, unroll, v_groups):
    m_ref[...] = jnp.full(m_ref.shape, NEG_INF, F32)
    acc_ref[...] = jnp.zeros(acc_ref.shape, F32)

    def body(j, carry):
        start = pl.multiple_of(j * tk, tk)
        k = k_ref[0, pl.ds(start, tk), :]
        s = lax.dot_general(q, k, (((1,), (1,)), ((), ())), preferred_element_type=F32)
        m_old = m_ref[...]
        m_new = jnp.maximum(m_old, jnp.max(s, axis=-1, keepdims=True))
        alpha = jnp.exp2(m_old - m_new)
        p = jnp.exp2(s - jnp.tile(m_new, (1, tk // LANES))).astype(BF16)
        for r0, r1, l0 in v_groups:
            v = v_ref[0, pl.ds(start, tk), l0:l0 + LANES]
            acc_ref[r0:r1] = alpha[r0:r1] * acc_ref[r0:r1] + jnp.dot(
                p[r0:r1], v, preferred_element_type=F32)
        m_ref[...] = m_new
        return carry

    lax.fori_loop(0, k_ref.shape[1] // tk, body, 0, unroll=unroll)


def _normalised(acc, dv):
    return acc[:, :dv] / acc[:, dv:dv + 1]


def _gqa_kernel(q_ref, k_ref, v_ref, o_ref, m_ref, acc_ref, *, tk, unroll):
    tq = q_ref.shape[1]
    q = q_ref[0].astype(F32)
    low = lax.broadcasted_iota(jnp.int32, (tq, LANES), 1) < C_HEAD_DIM
    parts = []
    for half in (q[:, :LANES], q[:, LANES:]):
        parts += [jnp.where(low, half, 0.0), jnp.where(low, 0.0, half)]
    qs = jnp.concatenate(parts, axis=0).astype(BF16)
    _online_softmax(qs, k_ref, v_ref, m_ref, acc_ref, tk=tk, unroll=unroll,
                    v_groups=((0, C_GROUPS * tq, 0),))
    acc = acc_ref[...]
    o_ref[0] = jnp.concatenate(
        [_normalised(acc[g * tq:(g + 1) * tq], C_HEAD_DIM) for g in range(C_GROUPS)],
        axis=1).astype(o_ref.dtype)


def gqa_attention(q, kk, vv, *, tq, tk, unroll):
    b, t, _ = q.shape
    rows = C_GROUPS * tq
    return pl.pallas_call(
        functools.partial(_gqa_kernel, tk=tk, unroll=unroll),
        grid=(b, C_KV_HEADS, t // tq),
        in_specs=[pl.BlockSpec((1, tq, 256), lambda i, j, n: (i, n, j)),
                  pl.BlockSpec((1, t, LANES), lambda i, j, n: (i, 0, j)),
                  pl.BlockSpec((1, t, LANES), lambda i, j, n: (i, 0, j))],
        out_specs=pl.BlockSpec((1, tq, 256), lambda i, j, n: (i, n, j)),
        out_shape=jax.ShapeDtypeStruct((b, t, C_WIDTH), BF16),
        scratch_shapes=[pltpu.VMEM((rows, LANES), F32), pltpu.VMEM((rows, LANES), F32)],
        compiler_params=_params(("parallel", "parallel", "arbitrary")),
        name="gqa_attention",
    )(q, kk, vv)


def _mla_kernel(q_ref, k_ref, v_ref, o_ref, m_ref, acc_ref, *, tk, unroll):
    tq = q_ref.shape[1]
    q = q_ref[0]
    zeros = jnp.zeros((tq, LANES), q.dtype)
    qs = jnp.concatenate([jnp.concatenate([q[:, :LANES], zeros], axis=1),
                          jnp.concatenate([zeros, q[:, LANES:]], axis=1)], axis=0)
    _online_softmax(qs, k_ref, v_ref, m_ref, acc_ref, tk=tk, unroll=unroll,
                    v_groups=((0, tq, 0), (tq, 2 * tq, LANES)))
    acc = acc_ref[...]
    o_ref[0] = jnp.concatenate(
        [_normalised(acc[g * tq:(g + 1) * tq], MLA_V_DIM) for g in range(2)],
        axis=1).astype(o_ref.dtype)


def mla_attention(qb, kb, vb, *, tq, tk, unroll):
    b, t, _ = qb.shape
    rows = 2 * tq
    return pl.pallas_call(
        functools.partial(_mla_kernel, tk=tk, unroll=unroll),
        grid=(b, MLA_HEADS // 2, t // tq),
        in_specs=[pl.BlockSpec((1, tq, 256), lambda i, j, n: (i, n, j)),
                  pl.BlockSpec((1, t, 256), lambda i, j, n: (i, 0, j)),
                  pl.BlockSpec((1, t, 256), lambda i, j, n: (i, 0, j))],
        out_specs=pl.BlockSpec((1, tq, LANES), lambda i, j, n: (i, n, j)),
        out_shape=jax.ShapeDtypeStruct((b, t, B_WIDTH), BF16),
        scratch_shapes=[pltpu.VMEM((rows, LANES), F32), pltpu.VMEM((rows, LANES), F32)],
        compiler_params=_params(("parallel", "parallel", "arbitrary")),
        name="mla_attention",
    )(qb, kb, vb)


A_TQ = 2048
A_SUB = 128
A_BAND = A_SUB + 2 * A_RADIUS
A_CLASSES = 16
A_LOCKSTEP = (4, 4, 4)
A_PAD = A_RADIUS


def _class_runs(cfg, sub):
    if cfg == 2:
        return [(pl.multiple_of(sub * A_SUB, A_SUB), A_SUB)]
    if cfg == 1:
        r4, blk = sub & 3, sub >> 2
        return [(pl.multiple_of((r4 + 4 * a) * A_SUB + blk * 32, 32), 32) for a in range(4)]
    return [(pl.multiple_of(r * A_SUB + sub * 8, 8), 8) for r in range(A_CLASSES)]


def _dilated_kernel(*refs):
    n_cls = A_CLASSES
    q16 = refs[0:n_cls]
    k_in = (refs[16:17], refs[17:21], refs[21:37])
    v_in = (refs[37:38], refs[38:42], refs[42:58])
    b_ref = refs[58]
    o_ref = refs[59]
    qf_ref, kp1, kp4, kp16, vp1, vp4, vp16, oacc_ref, m_ref, l_ref = refs[60:]
    k_pad, v_pad = (kp1, kp4, kp16), (vp1, vp4, vp16)
    qi = pl.program_id(2)

    @pl.when(qi == 0)
    def _():
        zeros = jnp.zeros((A_PAD, LANES), BF16)
        for pads, srcs in ((k_pad, k_in), (v_pad, v_in)):
            for pad_ref, cls_refs in zip(pads, srcs):
                units = cls_refs[0].shape[1]
                for r, src in enumerate(cls_refs):
                    pad_ref[r, 0:A_PAD] = zeros
                    pad_ref[r, A_PAD + units:] = zeros
                    pad_ref[r, A_PAD:A_PAD + units] = src[0]

    for r in range(n_cls):
        qf_ref[r * A_SUB:(r + 1) * A_SUB] = q16[r][0].astype(F32)
    low = lax.broadcasted_iota(jnp.int32, (A_SUB, LANES), 1) < A_HEAD_DIM
    key_col = lax.broadcasted_iota(jnp.int32, (1, A_BAND), 1)

    for ci, (_, dil) in enumerate(DILATED_CONFIGS):
        units = k_in[ci][0].shape[1]
        per_step = A_TQ // dil

        def body(it, carry, ci=ci, units=units, per_step=per_step):
            subs = [it * A_LOCKSTEP[ci] + u for u in range(A_LOCKSTEP[ci])]
            if ci == 2:
                cls, first = subs, [qi * per_step] * len(subs)
            elif ci == 1:
                cls = [s_ & 3 for s_ in subs]
                first = [qi * per_step + (s_ >> 2) * A_SUB for s_ in subs]
            else:
                cls, first = [0] * len(subs), [qi * per_step + s_ * A_SUB for s_ in subs]
            runs = [_class_runs(ci, s_) for s_ in subs]

            def gather(ref, sub_runs):
                parts = [ref[pl.ds(start, size), :] for start, size in sub_runs]
                return parts[0] if len(parts) == 1 else jnp.concatenate(parts, axis=0)

            def scatter(ref, sub_runs, val):
                row = 0
                for start, size in sub_runs:
                    ref[pl.ds(start, size), :] = val[row:row + size]
                    row += size

            qs = []
            for sub_runs in runs:
                q = gather(qf_ref, sub_runs)
                qs.append(jnp.concatenate([jnp.where(low, q, 0.0), jnp.where(low, 0.0, q)],
                                          axis=0).astype(BF16))
            kw = [k_pad[ci][c, pl.ds(pl.multiple_of(f, A_SUB), A_BAND), :] for c, f in zip(cls, first)]
            vw = [v_pad[ci][c, pl.ds(pl.multiple_of(f, A_SUB), A_BAND), :] for c, f in zip(cls, first)]
            s = [lax.dot_general(a, b_, (((1,), (1,)), ((), ())), preferred_element_type=F32)
                 for a, b_ in zip(qs, kw)]
            bias = b_ref[0, ci]
            masked = []
            for s_u, f in zip(s, first):
                unit = f - A_RADIUS + key_col
                valid = jnp.where((unit >= 0) & (unit < units), 0.0, NEG_INF)
                masked.append(s_u + bias + valid)
            m_blk = [jnp.max(x, axis=-1, keepdims=True) for x in masked]
            p = [jnp.exp2(x - m) for x, m in zip(masked, m_blk)]
            l_blk = [jnp.sum(x, axis=-1, keepdims=True) for x in p]
            o_blk = [jnp.dot(x.astype(BF16), v, preferred_element_type=F32) for x, v in zip(p, vw)]
            for sub_runs, o_u, m_u, l_u in zip(runs, o_blk, m_blk, l_blk):
                o2 = jnp.where(low, o_u[:A_SUB], o_u[A_SUB:])
                m2 = jnp.where(low, m_u[:A_SUB], m_u[A_SUB:])
                l2 = jnp.where(low, l_u[:A_SUB], l_u[A_SUB:])
                if ci == 0:
                    scatter(oacc_ref, sub_runs, o2)
                    scatter(m_ref, sub_runs, m2)
                    scatter(l_ref, sub_runs, l2)
                else:
                    m_old = gather(m_ref, sub_runs)
                    m_new = jnp.maximum(m_old, m2)
                    a_old = jnp.exp2(m_old - m_new)
                    a_blk = jnp.exp2(m2 - m_new)
                    scatter(oacc_ref, sub_runs, gather(oacc_ref, sub_runs) * a_old + o2 * a_blk)
                    scatter(l_ref, sub_runs, gather(l_ref, sub_runs) * a_old + l2 * a_blk)
                    scatter(m_ref, sub_runs, m_new)
            return carry

        lax.fori_loop(0, A_TQ // A_SUB // A_LOCKSTEP[ci], body, 0)

    for r in range(n_cls):
        rows = slice(r * A_SUB, (r + 1) * A_SUB)
        o_ref[0, r] = (oacc_ref[rows] / l_ref[rows]).astype(o_ref.dtype)


def dilated_attention(qkv, bias):
    b, t, width = qkv.shape
    pairs = A_HEADS // 2
    slabs = width // LANES
    views = {d: qkv.reshape(b, t // d, d * width) for d in (1, 4, A_CLASSES)}

    def cls_spec(d, r, rows, col0, per_step):
        if per_step:
            return pl.BlockSpec((1, rows, LANES), lambda i, j, n: (i, n, r * slabs + col0 + j))
        return pl.BlockSpec((1, rows, LANES), lambda i, j, n: (i, 0, r * slabs + col0 + j))

    in_specs, operands = [], []
    for r in range(A_CLASSES):
        in_specs.append(cls_spec(A_CLASSES, r, A_TQ // A_CLASSES, 0, True))
        operands.append(views[A_CLASSES])
    for col0 in (pairs, 2 * pairs):
        for _, d in DILATED_CONFIGS:
            for r in range(d):
                in_specs.append(cls_spec(d, r, t // d, col0, False))
                operands.append(views[d])
    in_specs.append(pl.BlockSpec((1, len(DILATED_CONFIGS), 2 * A_SUB, A_BAND), lambda i, j, n: (j, 0, 0, 0)))
    operands.append(bias)

    state = pltpu.VMEM((A_TQ, LANES), F32)
    padded = [pltpu.VMEM((d, t // d + 2 * A_PAD, LANES), BF16) for _, d in DILATED_CONFIGS]
    return pl.pallas_call(
        _dilated_kernel,
        grid=(b, pairs, t // A_TQ),
        in_specs=in_specs,
        out_specs=pl.BlockSpec((1, A_CLASSES, A_SUB, LANES), lambda i, j, n: (i, 0, n, j)),
        out_shape=jax.ShapeDtypeStruct((b, A_CLASSES, t // A_CLASSES, A_WIDTH), BF16),
        scratch_shapes=[state] + padded + padded + [state, state, state],
        compiler_params=_params(("parallel", "parallel", "arbitrary")),
        name="dilated_attention",
    )(*operands)


def _dilated_bias():
    i = jnp.arange(A_SUB, dtype=jnp.int32)[:, None]
    c = jnp.arange(A_BAND, dtype=jnp.int32)[None, :]
    u_rel = {1: 16 * (i % 8) + i // 8, 4: 4 * (i % 32) + i // 32, 16: i}
    slopes = 2.0 ** (-8.0 * jnp.arange(1, A_HEADS + 1, dtype=F32) / A_HEADS)
    per_cfg = []
    for _, dil in DILATED_CONFIGS:
        hops = jnp.abs(c - A_RADIUS - u_rel[dil])
        dist = (dil * hops).astype(F32)
        bias = jnp.where(hops <= A_RADIUS, -slopes[:, None, None] * dist[None] * LOG2E, NEG_INF)
        per_cfg.append(bias.reshape(A_HEADS // 2, 2 * A_SUB, A_BAND))
    return jnp.stack(per_cfg, axis=1)


def _gate_out_kernel(*refs, n_parts):
    parts = refs[:n_parts]
    gate_ref, x_ref, w_ref, g_ref, o_ref = refs[n_parts:]
    o = jnp.concatenate([r[...] for r in parts], axis=-1) if n_parts > 1 else parts[0][...]
    gate = gate_ref[...].astype(F32)
    y = (o.astype(F32) * (gate * jax.nn.sigmoid(gate))).astype(BF16)
    y = jnp.dot(y, w_ref[...], preferred_element_type=F32)
    o_ref[...] = x_ref[...] + _rms(y, g_ref[...])


def gate_out(parts, gate, x, w, g):
    m, d = x.shape
    row = lambda i: (i, 0)
    const = lambda i: (0, 0)
    in_specs = [pl.BlockSpec((ROW_TILE, p.shape[1]), row) for p in parts]
    in_specs += [pl.BlockSpec((ROW_TILE, gate.shape[1]), row),
                 pl.BlockSpec((ROW_TILE, d), row),
                 pl.BlockSpec(w.shape, const),
                 pl.BlockSpec((1, d), const)]
    return pl.pallas_call(
        functools.partial(_gate_out_kernel, n_parts=len(parts)),
        grid=(m // ROW_TILE,),
        in_specs=in_specs,
        out_specs=pl.BlockSpec((ROW_TILE, d), row),
        out_shape=jax.ShapeDtypeStruct((m, d), F32),
        compiler_params=_params(("parallel",)),
        name="gate_out",
    )(*parts, gate, x, w, g.reshape(1, d))


def _gate_out_class_kernel(a_ref, b_ref, gate_ref, x_ref, w_ref, g_ref, o_ref):
    o = jnp.concatenate([a_ref[0], b_ref[0]], axis=-1)
    gate = gate_ref[0].astype(F32)
    y = (o.astype(F32) * (gate * jax.nn.sigmoid(gate))).astype(BF16)
    y = jnp.dot(y, w_ref[...], preferred_element_type=F32)
    o_ref[0] = x_ref[0] + _rms(y, g_ref[...])


def gate_out_by_class(out_a, out_b, gate, x, w, g, b, t):
    d = x.shape[1]
    units = t // A_CLASSES
    tile = min(ROW_TILE, units)
    view = lambda a: a.reshape(b, units, A_CLASSES * a.shape[-1])
    by_class = lambda width: pl.BlockSpec((1, tile, width), lambda i, r, n: (i, n, r))
    const = lambda i, r, n: (0, 0)
    out = pl.pallas_call(
        _gate_out_class_kernel,
        grid=(b, A_CLASSES, units // tile),
        in_specs=[pl.BlockSpec((1, None, tile, out_a.shape[-1]), lambda i, r, n: (i, r, n, 0)),
                  by_class(out_b.shape[-1]), by_class(gate.shape[-1]), by_class(d),
                  pl.BlockSpec(w.shape, const), pl.BlockSpec((1, d), const)],
        out_specs=by_class(d),
        out_shape=jax.ShapeDtypeStruct((b, units, A_CLASSES * d), F32),
        compiler_params=_params(("parallel", "parallel", "parallel")),
        name="gate_out_by_class",
    )(out_a, view(out_b), view(gate), view(x), w, g.reshape(1, d))
    return out.reshape(b * t, d)


def _rope_freqs():
    half = MLA_ROPE_DIM // 2
    return ROPE_THETA ** (-jnp.arange(half, dtype=F32) / half)


def _mla_tables(seq):
    ang = jnp.arange(seq, dtype=F32)[:, None] * _rope_freqs()[None, :]
    cos, sin = jnp.cos(ang), jnp.sin(ang)
    zeros = jnp.zeros_like(cos)
    ones_nope = jnp.ones((seq, MLA_NOPE_DIM), F32)
    zeros_nope = jnp.zeros((seq, MLA_NOPE_DIM), F32)
    tail = jnp.zeros((seq, LANES - MLA_NOPE_DIM - MLA_ROPE_DIM), F32)
    c = jnp.concatenate([ones_nope, cos, cos, tail], axis=1)
    sa = jnp.concatenate([zeros_nope, zeros, sin, tail], axis=1)
    sb = jnp.concatenate([zeros_nope, -sin, zeros, tail], axis=1)
    return c, sa, sb


def _axial_tables(seq):
    t = jnp.arange(seq, dtype=jnp.int32)
    f = _rope_freqs()[None, :]
    ar = (t // GRID_W).astype(F32)[:, None] * f
    ac = (t % GRID_W).astype(F32)[:, None] * f
    zeros = jnp.zeros_like(ar)
    c = jnp.concatenate([jnp.cos(ar), jnp.cos(ar), jnp.cos(ac), jnp.cos(ac)], axis=1)
    sa = jnp.concatenate([zeros, jnp.sin(ar), zeros, jnp.sin(ac)], axis=1)
    sb = jnp.concatenate([-jnp.sin(ar), zeros, -jnp.sin(ac), zeros], axis=1)
    return tuple(jnp.tile(x, (1, 2)) for x in (c, sa, sb))


def _head_block_diag():
    idx = np.arange(256) // C_HEAD_DIM
    return jnp.asarray(idx[:, None] == idx[None, :], BF16)


def _arrange_w_in_ab(w):
    qa, ka, va, cq, ckv, kr, gate = jnp.split(
        w, list(np.cumsum([A_WIDTH, A_WIDTH, A_WIDTH, MLA_Q_RANK, MLA_KV_RANK, MLA_ROPE_DIM])), axis=1)
    d = w.shape[0]
    kr_blk = jnp.concatenate([jnp.zeros((d, KR_LANE), F32), kr,
                              jnp.zeros((d, LANES - KR_LANE - MLA_ROPE_DIM), F32)], axis=1)
    qa = qa * (A_HEAD_DIM ** -0.5 * LOG2E)
    return jnp.concatenate([qa, ka, va, cq, ckv, kr_blk, gate], axis=1).astype(BF16)


def _arrange_w_in_c(w):
    d = w.shape[0]
    q, k, v, gate = jnp.split(w, list(np.cumsum([C_WIDTH, C_KV_HEADS * C_HEAD_DIM,
                                                 C_KV_HEADS * C_HEAD_DIM])), axis=1)
    k = k.reshape(d, C_KV_HEADS, 1, C_HEAD_DIM)
    kk = jnp.broadcast_to(k, (d, C_KV_HEADS, 2, C_HEAD_DIM)).reshape(d, C_KV_HEADS * LANES)
    v = v.reshape(d, C_KV_HEADS, C_HEAD_DIM)
    vv = jnp.concatenate([v, jnp.zeros_like(v)], axis=2).reshape(d, C_KV_HEADS * LANES)
    return jnp.concatenate([gate, q, kk, vv], axis=1).astype(BF16)


def _arrange_w_uq(w):
    r = w.shape[0]
    w = w.reshape(r, MLA_HEADS, MLA_NOPE_DIM + MLA_ROPE_DIM)
    pad = jnp.zeros((r, MLA_HEADS, LANES - MLA_NOPE_DIM - MLA_ROPE_DIM), F32)
    return jnp.concatenate([w, pad], axis=2).reshape(r, MLA_WIDE).astype(BF16)


def _arrange_w_ukv(w):
    r = w.shape[0]
    w = w.reshape(r, MLA_HEADS, MLA_NOPE_DIM + MLA_V_DIM)
    pad = jnp.zeros((r, MLA_HEADS, LANES - MLA_NOPE_DIM), F32)
    wk = jnp.concatenate([w[:, :, :MLA_NOPE_DIM], pad], axis=2).reshape(r, MLA_WIDE)
    wv = jnp.concatenate([w[:, :, MLA_NOPE_DIM:], pad], axis=2).reshape(r, MLA_WIDE)
    return wk.astype(BF16), wv.astype(BF16)


def _even_layer(x, b, t, g_pre, g_post, w_in, gq, wuq, gkv, wuk, wuv, w_out, bias, tabs):
    qkv, gate, qb, kb, vb = in_proj_ab(x, t, g_pre, w_in, gq, gkv, wuq, wuk, wuv, tabs)
    three = lambda a: a.reshape(b, t, a.shape[1])
    out_a = dilated_attention(three(qkv), bias)
    out_b = mla_attention(three(qb), three(kb), three(vb), tq=1024, tk=1024, unroll=4)
    return gate_out_by_class(out_a, out_b, gate, x, w_out, g_post, b, t)


def _odd_layer(x, b, t, g_pre, g_post, w_in, q_norm, k_norm, w_out, bd, tabs):
    gate, q, kk, vv = in_proj_c(x, t, g_pre, w_in, q_norm, k_norm, bd, tabs)
    three = lambda a: a.reshape(b, t, a.shape[1])
    o = gqa_attention(three(q), three(kk), three(vv), tq=512, tk=1024, unroll=4)
    return gate_out([o.reshape(b * t, C_WIDTH)], gate, x, w_out, g_post)


def _trunk(x3, weights, consts):
    (norm_pre, norm_post, w_in_ab, mla_q_norm, w_uq, mla_kv_norm, w_uk, w_uv, w_out_ab,
     w_in_c, c_q_norm, c_k_norm, w_out_c) = weights
    b, t, d = x3.shape
    x = x3.reshape(b * t, d)
    bias, bd = consts
    mla_tabs = _mla_tables(t)
    axial_tabs = _axial_tables(t)
    depth = norm_pre.shape[0]
    for layer in range(depth):
        i = layer // 2
        if layer % 2 == 0:
            x = _even_layer(x, b, t, norm_pre[layer], norm_post[layer], w_in_ab[i], mla_q_norm[i],
                            w_uq[i], mla_kv_norm[i], w_uk[i], w_uv[i], w_out_ab[i], bias, mla_tabs)
        else:
            x = _odd_layer(x, b, t, norm_pre[layer], norm_post[layer], w_in_c[i], c_q_norm[i],
                           c_k_norm[i], w_out_c[i], bd, axial_tabs)
    return x.reshape(b, t, d)


def kernel(x_prompt, x_sample, norm_pre, norm_post, w_in_ab, mla_q_norm, w_uq, mla_kv_norm, w_ukv,
           w_out_ab, w_in_c, c_q_norm, c_k_norm, w_out_c):
    n_even, n_odd = w_in_ab.shape[0], w_in_c.shape[0]
    w_in_ab_p = jnp.stack([_arrange_w_in_ab(w_in_ab[i]) for i in range(n_even)])
    w_uq_p = jnp.stack([_arrange_w_uq(w_uq[i]) for i in range(n_even)])
    w_ukv_p = [_arrange_w_ukv(w_ukv[i]) for i in range(n_even)]
    w_uk_p = jnp.stack([p[0] for p in w_ukv_p])
    w_uv_p = jnp.stack([p[1] for p in w_ukv_p])
    w_in_c_p = jnp.stack([_arrange_w_in_c(w_in_c[i]) for i in range(n_odd)])
    weights = (norm_pre, norm_post, w_in_ab_p, mla_q_norm, w_uq_p, mla_kv_norm, w_uk_p, w_uv_p,
               w_out_ab.astype(BF16), w_in_c_p, c_q_norm, c_k_norm, w_out_c.astype(BF16))
    consts = (_dilated_bias(), _head_block_diag())
    return (_trunk(x_prompt, weights, consts), _trunk(x_sample, weights, consts))
```

```python
import functools
import math

import numpy as np
import jax
import jax.numpy as jnp
from jax import lax
from jax.experimental import pallas as pl
from jax.experimental.pallas import tpu as pltpu

F32 = jnp.float32
BF16 = jnp.bfloat16

D_MODEL = 1024
EPS = 1e-6
ROPE_THETA = 10000.0
NEG_INF = -1e30
LOG2E = math.log2(math.e)
GRID_W = 64
LANES = 128
A_HEADS = 8
A_HEAD_DIM = 64
DILATED_CONFIGS = ((128, 1), (512, 4), (2048, 16))
A_RADIUS = 64
A_REACH = max(w // 2 for w, _ in DILATED_CONFIGS)
MLA_HEADS = 8
MLA_Q_RANK = 256
MLA_KV_RANK = 128
MLA_NOPE_DIM = 64
MLA_ROPE_DIM = 32
MLA_V_DIM = 64
C_HEADS = 16
C_KV_HEADS = 4
C_GROUPS = C_HEADS // C_KV_HEADS
C_HEAD_DIM = 64
A_WIDTH = A_HEADS * A_HEAD_DIM
B_WIDTH = MLA_HEADS * MLA_V_DIM
AB_WIDTH = A_WIDTH + B_WIDTH
C_WIDTH = C_HEADS * C_HEAD_DIM
MLA_WIDE = MLA_HEADS * LANES
COL_QKV_A, COL_LATENT, COL_GATE_AB = 0, 1536, 2048
IN_AB_PAD = 3072
KR_LANE = MLA_NOPE_DIM
COL_GATE_C, COL_QC, COL_KC, COL_VC = 0, 1024, 2048, 2560
IN_C_PAD = 3072

VMEM_LIMIT = 56 * 1024 * 1024
ROW_TILE = 512


def _params(sem):
    return pltpu.CompilerParams(dimension_semantics=sem, vmem_limit_bytes=VMEM_LIMIT)


def _rms(x, g):
    ms = jnp.mean(x * x, axis=-1, keepdims=True)
    return x * lax.rsqrt(ms + EPS) * g


def _rope_lanes(x, c, sa, sb):
    return x * c + pltpu.roll(x, 16, 1) * sa + pltpu.roll(x, 112, 1) * sb


def _ones_lane_row(width, lane):
    idx = lax.broadcasted_iota(jnp.int32, (1, width), 1)
    return jnp.where(idx % LANES == lane, 1.0, 0.0).astype(F32)


def _pipelined(jobs):
    nxt = jobs[0][0]()
    for i, (_, consume) in enumerate(jobs):
        cur = nxt
        nxt = jobs[i + 1][0]() if i + 1 < len(jobs) else None
        consume(cur)


def _in_ab_kernel(x_ref, g_ref, w_ref, gq_ref, gkv_ref, wuq_ref, wuk_ref, wuv_ref,
                  c_ref, sa_ref, sb_ref, qkv_ref, gate_ref, qb_ref, kb_ref, vb_ref, *, q_scale):
    h = _rms(x_ref[...], g_ref[...]).astype(BF16)
    c, sa, sb = c_ref[...], sa_ref[...], sb_ref[...]

    def proj(c0):
        return lambda: jnp.dot(h, w_ref[:, c0:c0 + 512], preferred_element_type=F32)

    def store(ref, c0):
        def consume(v):
            ref[:, c0:c0 + 512] = v.astype(BF16)
        return consume

    latent = {}

    def up_project(lat):
        cq = _rms(lat[:, :MLA_Q_RANK], gq_ref[...]).astype(BF16)
        ckv = _rms(lat[:, MLA_Q_RANK:MLA_Q_RANK + MLA_KV_RANK], gkv_ref[...]).astype(BF16)
        latent["kr"] = _rope_lanes(lat[:, MLA_Q_RANK + MLA_KV_RANK:], c, sa, sb)
        latent["qf"] = jnp.dot(cq, wuq_ref[...], preferred_element_type=F32)
        latent["kf"] = jnp.dot(ckv, wuk_ref[...], preferred_element_type=F32)
        vf = jnp.dot(ckv, wuv_ref[...], preferred_element_type=F32)
        vb_ref[...] = (vf + _ones_lane_row(MLA_WIDE, MLA_V_DIM)).astype(BF16)

    def rope_heads(ref, c0, heads):
        store_plain = store(ref, c0)

        def consume(v):
            store_plain(v)
            for hd in heads:
                sl = slice(hd * LANES, (hd + 1) * LANES)
                qb_ref[:, sl] = (_rope_lanes(latent["qf"][:, sl], c, sa, sb) * q_scale).astype(BF16)
                kb_ref[:, sl] = (latent["kf"][:, sl] + latent["kr"]).astype(BF16)
        return consume

    half = MLA_HEADS // 2
    _pipelined([
        (proj(COL_LATENT), up_project),
        (proj(COL_QKV_A), store(qkv_ref, 0)),
        (proj(COL_QKV_A + 512), store(qkv_ref, 512)),
        (proj(COL_QKV_A + 1024), rope_heads(qkv_ref, 1024, range(0, half))),
        (proj(COL_GATE_AB), rope_heads(gate_ref, 0, range(half, MLA_HEADS))),
        (proj(COL_GATE_AB + 512), store(gate_ref, 512)),
    ])


def in_proj_ab(x, seq, g, w, gq, gkv, wuq, wuk, wuv, tabs):
    m, d = x.shape
    tiles_per_seq = seq // ROW_TILE
    row = lambda i: (i, 0)
    const = lambda i: (0, 0)
    pos = lambda i: (i % tiles_per_seq, 0)
    out_w = (COL_LATENT, AB_WIDTH, MLA_WIDE, MLA_WIDE, MLA_WIDE)
    return pl.pallas_call(
        functools.partial(_in_ab_kernel, q_scale=(MLA_NOPE_DIM + MLA_ROPE_DIM) ** -0.5 * LOG2E),
        grid=(m // ROW_TILE,),
        in_specs=[pl.BlockSpec((ROW_TILE, d), row),
                  pl.BlockSpec((1, d), const),
                  pl.BlockSpec(w.shape, const),
                  pl.BlockSpec((1, MLA_Q_RANK), const),
                  pl.BlockSpec((1, MLA_KV_RANK), const),
                  pl.BlockSpec(wuq.shape, const),
                  pl.BlockSpec(wuk.shape, const),
                  pl.BlockSpec(wuv.shape, const)] + [pl.BlockSpec((ROW_TILE, LANES), pos)] * 3,
        out_specs=[pl.BlockSpec((ROW_TILE, n), row) for n in out_w],
        out_shape=[jax.ShapeDtypeStruct((m, n), BF16) for n in out_w],
        compiler_params=_params(("parallel",)),
        name="in_proj_ab",
    )(x, g.reshape(1, d), w, gq.reshape(1, -1), gkv.reshape(1, -1), wuq, wuk, wuv, *tabs)


def _in_c_kernel(x_ref, g_ref, w_ref, gq_ref, gk_ref, bd_ref, c_ref, sa_ref, sb_ref,
                 gate_ref, q_ref, kk_ref, vv_ref, *, q_scale):
    h = _rms(x_ref[...], g_ref[...]).astype(BF16)
    c, sa, sb = c_ref[...], sa_ref[...], sb_ref[...]
    bd = bd_ref[...]

    def proj(c0):
        return lambda: jnp.dot(h, w_ref[:, c0:c0 + 512], preferred_element_type=F32)

    def head_norm_rope(x, gain, scale):
        sq = x * x
        hi = sq.astype(BF16)
        lo = (sq - hi.astype(F32)).astype(BF16)
        ss = (jnp.dot(hi, bd, preferred_element_type=F32) +
              jnp.dot(lo, bd, preferred_element_type=F32))
        xn = x * lax.rsqrt(ss * (1.0 / C_HEAD_DIM) + EPS) * gain
        halves = [_rope_lanes(xn[:, s:s + LANES], c, sa, sb) for s in (0, LANES)]
        return (jnp.concatenate(halves, axis=1) * scale).astype(BF16)

    def store(ref, c0):
        def consume(v):
            ref[:, c0:c0 + 512] = v.astype(BF16)
        return consume

    def store_heads(ref, c0, gain_ref, scale):
        def consume(v):
            for s in (0, 256):
                ref[:, c0 + s:c0 + s + 256] = head_norm_rope(v[:, s:s + 256], gain_ref[...], scale)
        return consume

    def store_values(v):
        vv_ref[...] = (v + _ones_lane_row(v.shape[1], C_HEAD_DIM)).astype(BF16)

    _pipelined([
        (proj(COL_QC), store_heads(q_ref, 0, gq_ref, q_scale)),
        (proj(COL_GATE_C), store(gate_ref, 0)),
        (proj(COL_QC + 512), store_heads(q_ref, 512, gq_ref, q_scale)),
        (proj(COL_GATE_C + 512), store(gate_ref, 512)),
        (proj(COL_KC), store_heads(kk_ref, 0, gk_ref, 1.0)),
        (proj(COL_VC), store_values),
    ])


def in_proj_c(x, seq, g, w, gq, gk, bd, tabs):
    m, d = x.shape
    tiles_per_seq = seq // ROW_TILE
    row = lambda i: (i, 0)
    const = lambda i: (0, 0)
    pos = lambda i: (i % tiles_per_seq, 0)
    out_w = (C_WIDTH, C_WIDTH, C_KV_HEADS * LANES, C_KV_HEADS * LANES)
    tile4 = lambda v: jnp.tile(v.reshape(1, -1), (1, 4))
    return pl.pallas_call(
        functools.partial(_in_c_kernel, q_scale=C_HEAD_DIM ** -0.5 * LOG2E),
        grid=(m // ROW_TILE,),
        in_specs=[pl.BlockSpec((ROW_TILE, d), row),
                  pl.BlockSpec((1, d), const),
                  pl.BlockSpec(w.shape, const),
                  pl.BlockSpec((1, 256), const),
                  pl.BlockSpec((1, 256), const),
                  pl.BlockSpec((256, 256), const)] + [pl.BlockSpec((ROW_TILE, LANES), pos)] * 3,
        out_specs=[pl.BlockSpec((ROW_TILE, n), row) for n in out_w],
        out_shape=[jax.ShapeDtypeStruct((m, n), BF16) for n in out_w],
        compiler_params=_params(("parallel",)),
        name="in_proj_c",
    )(x, g.reshape(1, d), w, tile4(gq), tile4(gk), bd, *tabs)


def _online_softmax(q, k_ref, v_ref, m_ref, acc_ref, *, tk, unroll, v_groups):
    m_ref[...] = jnp.full(m_ref.shape, NEG_INF, F32)
    acc_ref[...] = jnp.zeros(acc_ref.shape, F32)

    def body(j, carry):
        start = pl.multiple_of(j * tk, tk)
        k = k_ref[0, pl.ds(start, tk), :]
        s = lax.dot_general(q, k, (((1,), (1,)), ((), ())), preferred_element_type=F32)
        m_old = m_ref[...]
        m_new = jnp.maximum(m_old, jnp.max(s, axis=-1, keepdims=True))
        alpha = jnp.exp2(m_old - m_new)
        p = jnp.exp2(s - jnp.tile(m_new, (1, tk // LANES))).astype(BF16)
        for r0, r1, l0 in v_groups:
            v = v_ref[0, pl.ds(start, tk), l0:l0 + LANES]
            acc_ref[r0:r1] = alpha[r0:r1] * acc_ref[r0:r1] + jnp.dot(
                p[r0:r1], v, preferred_element_type=F32)
        m_ref[...] = m_new
        return carry

    lax.fori_loop(0, k_ref.shape[1] // tk, body, 0, unroll=unroll)


def _normalised(acc, dv):
    return acc[:, :dv] / acc[:, dv:dv + 1]


def _gqa_kernel(q_ref, k_ref, v_ref, o_ref, m_ref, acc_ref, *, tk, unroll):
    tq = q_ref.shape[1]
    q = q_ref[0].astype(F32)
    low = lax.broadcasted_iota(jnp.int32, (tq, LANES), 1) < C_HEAD_DIM
    parts = []
    for half in (q[:, :LANES], q[:, LANES:]):
        parts += [jnp.where(low, half, 0.0), jnp.where(low, 0.0, half)]
    qs = jnp.concatenate(parts, axis=0).astype(BF16)
    _online_softmax(qs, k_ref, v_ref, m_ref, acc_ref, tk=tk, unroll=unroll,
                    v_groups=((0, C_GROUPS * tq, 0),))
    acc = acc_ref[...]
    o_ref[0] = jnp.concatenate(
        [_normalised(acc[g * tq:(g + 1) * tq], C_HEAD_DIM) for g in range(C_GROUPS)],
        axis=1).astype(o_ref.dtype)


def gqa_attention(q, kk, vv, *, tq, tk, unroll):
    b, t, _ = q.shape
    rows = C_GROUPS * tq
    return pl.pallas_call(
        functools.partial(_gqa_kernel, tk=tk, unroll=unroll),
        grid=(b, C_KV_HEADS, t // tq),
        in_specs=[pl.BlockSpec((1, tq, 256), lambda i, j, n: (i, n, j)),
                  pl.BlockSpec((1, t, LANES), lambda i, j, n: (i, 0, j)),
                  pl.BlockSpec((1, t, LANES), lambda i, j, n: (i, 0, j))],
        out_specs=pl.BlockSpec((1, tq, 256), lambda i, j, n: (i, n, j)),
        out_shape=jax.ShapeDtypeStruct((b, t, C_WIDTH), BF16),
        scratch_shapes=[pltpu.VMEM((rows, LANES), F32), pltpu.VMEM((rows, LANES), F32)],
        compiler_params=_params(("parallel", "parallel", "arbitrary")),
        name="gqa_attention",
    )(q, kk, vv)


def _mla_kernel(q_ref, k_ref, v_ref, o_ref, m_ref, acc_ref, *, tk, unroll):
    tq = q_ref.shape[1]
    q = q_ref[0]
    zeros = jnp.zeros((tq, LANES), q.dtype)
    qs = jnp.concatenate([jnp.concatenate([q[:, :LANES], zeros], axis=1),
                          jnp.concatenate([zeros, q[:, LANES:]], axis=1)], axis=0)
    _online_softmax(qs, k_ref, v_ref, m_ref, acc_ref, tk=tk, unroll=unroll,
                    v_groups=((0, tq, 0), (tq, 2 * tq, LANES)))
    acc = acc_ref[...]
    o_ref[0] = jnp.concatenate(
        [_normalised(acc[g * tq:(g + 1) * tq], MLA_V_DIM) for g in range(2)],
        axis=1).astype(o_ref.dtype)


def mla_attention(qb, kb, vb, *, tq, tk, unroll):
    b, t, _ = qb.shape
    rows = 2 * tq
    return pl.pallas_call(
        functools.partial(_mla_kernel, tk=tk, unroll=unroll),
        grid=(b, MLA_HEADS // 2, t // tq),
        in_specs=[pl.BlockSpec((1, tq, 256), lambda i, j, n: (i, n, j)),
                  pl.BlockSpec((1, t, 256), lambda i, j, n: (i, 0, j)),
                  pl.BlockSpec((1, t, 256), lambda i, j, n: (i, 0, j))],
        out_specs=pl.BlockSpec((1, tq, LANES), lambda i, j, n: (i, n, j)),
        out_shape=jax.ShapeDtypeStruct((b, t, B_WIDTH), BF16),
        scratch_shapes=[pltpu.VMEM((rows, LANES), F32), pltpu.VMEM((rows, LANES), F32)],
        compiler_params=_params(("parallel", "parallel", "arbitrary")),
        name="mla_attention",
    )(qb, kb, vb)


A_TQ = 2048
A_SUB = 128
A_BAND = A_SUB + 2 * A_RADIUS
A_CLASSES = 16
A_LOCKSTEP = 4
A_PAD = A_RADIUS


def _class_runs(cfg, sub):
    if cfg == 2:
        return [(pl.multiple_of(sub * A_SUB, A_SUB), A_SUB)]
    if cfg == 1:
        r4, blk = sub & 3, sub >> 2
        return [(pl.multiple_of((r4 + 4 * a) * A_SUB + blk * 32, 32), 32) for a in range(4)]
    return [(pl.multiple_of(r * A_SUB + sub * 8, 8), 8) for r in range(A_CLASSES)]


def _dilated_kernel(q_ref, k_ref, v_ref, b_ref, o_ref,
                    nat_ref, qc_ref, kp1, kp16, vp1, vp16, oacc_ref, m_ref, l_ref):
    seq = k_ref.shape[1]
    units16 = seq // A_CLASSES
    qi = pl.program_id(2)

    @pl.when(qi == 0)
    def _():
        zeros = jnp.zeros((A_PAD, LANES), BF16)
        for src, nat_pad, cls_pad in ((k_ref, kp1, kp16), (v_ref, vp1, vp16)):
            nat_pad[0:A_PAD] = zeros
            nat_pad[A_PAD + seq:] = zeros
            nat_pad[A_PAD:A_PAD + seq] = src[0]
            nat_ref[...] = src[0].astype(F32)
            for r in range(A_CLASSES):
                cls_pad[r, 0:A_PAD] = zeros
                cls_pad[r, A_PAD + units16:] = zeros
                cls_pad[r, A_PAD:A_PAD + units16] = nat_ref[pl.ds(r, units16, stride=A_CLASSES), :].astype(BF16)

    nat_ref[0:A_TQ] = q_ref[0].astype(F32)
    for r in range(A_CLASSES):
        qc_ref[r * A_SUB:(r + 1) * A_SUB] = nat_ref[pl.ds(r, A_SUB, stride=A_CLASSES), :]
    low = lax.broadcasted_iota(jnp.int32, (A_SUB, LANES), 1) < A_HEAD_DIM
    col = lax.broadcasted_iota(jnp.int32, (1, A_BAND), 1)
    col_unit = (col, 4 * (col % 64) + col // 64, col)

    for ci, (_, dil) in enumerate(DILATED_CONFIGS):
        units = seq // dil
        per_step = A_TQ // dil

        def body(it, carry, ci=ci, units=units, per_step=per_step):
            subs = [it * A_LOCKSTEP + u for u in range(A_LOCKSTEP)]
            if ci == 2:
                first = [qi * per_step] * len(subs)
            elif ci == 1:
                first = [qi * per_step + (s_ >> 2) * A_SUB for s_ in subs]
            else:
                first = [qi * per_step + s_ * A_SUB for s_ in subs]
            runs = [_class_runs(ci, s_) for s_ in subs]

            def window(nat_pad, cls_pad, sub, f):
                if ci == 0:
                    return nat_pad[pl.ds(pl.multiple_of(f, A_SUB), A_BAND), :]
                if ci == 2:
                    return cls_pad[sub, pl.ds(pl.multiple_of(f, A_SUB), A_BAND), :]
                row = pl.multiple_of((f >> 2) + A_PAD - A_RADIUS // 4, 16)
                return jnp.concatenate([cls_pad[(sub & 3) + 4 * a, pl.ds(row, A_BAND // 4), :]
                                        for a in range(4)], axis=0)

            def gather(ref, sub_runs):
                parts = [ref[pl.ds(start, size), :] for start, size in sub_runs]
                return parts[0] if len(parts) == 1 else jnp.concatenate(parts, axis=0)

            def scatter(ref, sub_runs, val):
                row = 0
                for start, size in sub_runs:
                    ref[pl.ds(start, size), :] = val[row:row + size]
                    row += size

            qs = []
            for sub_runs in runs:
                q = gather(qc_ref, sub_runs)
                qs.append(jnp.concatenate([jnp.where(low, q, 0.0), jnp.where(low, 0.0, q)],
                                          axis=0).astype(BF16))
            kw = [window(kp1, kp16, s_, f) for s_, f in zip(subs, first)]
            vw = [window(vp1, vp16, s_, f) for s_, f in zip(subs, first)]
            s = [lax.dot_general(a, b_, (((1,), (1,)), ((), ())), preferred_element_type=F32)
                 for a, b_ in zip(qs, kw)]
            bias = b_ref[0, ci]
            masked = []
            for s_u, f in zip(s, first):
                unit = f - A_RADIUS + col_unit[ci]
                valid = jnp.where((unit >= 0) & (unit < units), 0.0, NEG_INF)
                masked.append(s_u + bias + valid)
            m_blk = [jnp.max(x, axis=-1, keepdims=True) for x in masked]
            p = [jnp.exp2(x - m) for x, m in zip(masked, m_blk)]
            l_blk = [jnp.sum(x, axis=-1, keepdims=True) for x in p]
            o_blk = [jnp.dot(x.astype(BF16), v, preferred_element_type=F32) for x, v in zip(p, vw)]
            for sub_runs, o_u, m_u, l_u in zip(runs, o_blk, m_blk, l_blk):
                o2 = jnp.where(low, o_u[:A_SUB], o_u[A_SUB:])
                m2 = jnp.where(low, m_u[:A_SUB], m_u[A_SUB:])
                l2 = jnp.where(low, l_u[:A_SUB], l_u[A_SUB:])
                if ci == 0:
                    scatter(oacc_ref, sub_runs, o2)
                    scatter(m_ref, sub_runs, m2)
                    scatter(l_ref, sub_runs, l2)
                else:
                    m_old = gather(m_ref, sub_runs)
                    m_new = jnp.maximum(m_old, m2)
                    a_old = jnp.exp2(m_old - m_new)
                    a_blk = jnp.exp2(m2 - m_new)
                    scatter(oacc_ref, sub_runs, gather(oacc_ref, sub_runs) * a_old + o2 * a_blk)
                    scatter(l_ref, sub_runs, gather(l_ref, sub_runs) * a_old + l2 * a_blk)
                    scatter(m_ref, sub_runs, m_new)
            return carry

        lax.fori_loop(0, A_TQ // A_SUB // A_LOCKSTEP, body, 0)

    for r in range(A_CLASSES):
        rows = slice(r * A_SUB, (r + 1) * A_SUB)
        nat_ref[pl.ds(r, A_SUB, stride=A_CLASSES), :] = oacc_ref[rows] / l_ref[rows]
    o_ref[0] = nat_ref[0:A_TQ].astype(o_ref.dtype)


def dilated_attention(qkv, bias):
    b, t, _ = qkv.shape
    pairs = A_HEADS // 2
    state = pltpu.VMEM((A_TQ, LANES), F32)
    nat_pad = pltpu.VMEM((t + 2 * A_PAD, LANES), BF16)
    cls_pad = pltpu.VMEM((A_CLASSES, t // A_CLASSES + 2 * A_PAD, LANES), BF16)
    return pl.pallas_call(
        _dilated_kernel,
        grid=(b, pairs, t // A_TQ),
        in_specs=[pl.BlockSpec((1, A_TQ, LANES), lambda i, j, n: (i, n, j)),
                  pl.BlockSpec((1, t, LANES), lambda i, j, n: (i, 0, pairs + j)),
                  pl.BlockSpec((1, t, LANES), lambda i, j, n: (i, 0, 2 * pairs + j)),
                  pl.BlockSpec((1, len(DILATED_CONFIGS), 2 * A_SUB, A_BAND), lambda i, j, n: (j, 0, 0, 0))],
        out_specs=pl.BlockSpec((1, A_TQ, LANES), lambda i, j, n: (i, n, j)),
        out_shape=jax.ShapeDtypeStruct((b, t, A_WIDTH), BF16),
        scratch_shapes=[pltpu.VMEM((t, LANES), F32), state, nat_pad, cls_pad, nat_pad, cls_pad,
                        state, state, state],
        compiler_params=_params(("parallel", "parallel", "arbitrary")),
        name="dilated_attention",
    )(qkv, qkv, qkv, bias)


def _dilated_bias():
    i = jnp.arange(A_SUB, dtype=jnp.int32)[:, None]
    c = jnp.arange(A_BAND, dtype=jnp.int32)[None, :]
    u_rel = {1: 16 * (i % 8) + i // 8, 4: 4 * (i % 32) + i // 32, 16: i}
    k_rel = {1: c, 4: 4 * (c % 64) + c // 64, 16: c}
    slopes = 2.0 ** (-8.0 * jnp.arange(1, A_HEADS + 1, dtype=F32) / A_HEADS)
    per_cfg = []
    for _, dil in DILATED_CONFIGS:
        hops = jnp.abs(k_rel[dil] - A_RADIUS - u_rel[dil])
        dist = (dil * hops).astype(F32)
        bias = jnp.where(hops <= A_RADIUS, -slopes[:, None, None] * dist[None] * LOG2E, NEG_INF)
        per_cfg.append(bias.reshape(A_HEADS // 2, 2 * A_SUB, A_BAND))
    return jnp.stack(per_cfg, axis=1)


def _gate_out_kernel(*refs, n_parts):
    parts = refs[:n_parts]
    gate_ref, x_ref, w_ref, g_ref, o_ref = refs[n_parts:]
    o = jnp.concatenate([r[...] for r in parts], axis=-1) if n_parts > 1 else parts[0][...]
    gate = gate_ref[...].astype(F32)
    y = (o.astype(F32) * (gate * jax.nn.sigmoid(gate))).astype(BF16)
    y = jnp.dot(y, w_ref[...], preferred_element_type=F32)
    o_ref[...] = x_ref[...] + _rms(y, g_ref[...])


def gate_out(parts, gate, x, w, g):
    m, d = x.shape
    row = lambda i: (i, 0)
    const = lambda i: (0, 0)
    in_specs = [pl.BlockSpec((ROW_TILE, p.shape[1]), row) for p in parts]
    in_specs += [pl.BlockSpec((ROW_TILE, gate.shape[1]), row),
                 pl.BlockSpec((ROW_TILE, d), row),
                 pl.BlockSpec(w.shape, const),
                 pl.BlockSpec((1, d), const)]
    return pl.pallas_call(
        functools.partial(_gate_out_kernel, n_parts=len(parts)),
        grid=(m // ROW_TILE,),
        in_specs=in_specs,
        out_specs=pl.BlockSpec((ROW_TILE, d), row),
        out_shape=jax.ShapeDtypeStruct((m, d), F32),
        compiler_params=_params(("parallel",)),
        name="gate_out",
    )(*parts, gate, x, w, g.reshape(1, d))


def _rope_freqs():
    half = MLA_ROPE_DIM // 2
    return ROPE_THETA ** (-jnp.arange(half, dtype=F32) / half)


def _mla_tables(seq):
    ang = jnp.arange(seq, dtype=F32)[:, None] * _rope_freqs()[None, :]
    cos, sin = jnp.cos(ang), jnp.sin(ang)
    zeros = jnp.zeros_like(cos)
    ones_nope = jnp.ones((seq, MLA_NOPE_DIM), F32)
    zeros_nope = jnp.zeros((seq, MLA_NOPE_DIM), F32)
    tail = jnp.zeros((seq, LANES - MLA_NOPE_DIM - MLA_ROPE_DIM), F32)
    c = jnp.concatenate([ones_nope, cos, cos, tail], axis=1)
    sa = jnp.concatenate([zeros_nope, zeros, sin, tail], axis=1)
    sb = jnp.concatenate([zeros_nope, -sin, zeros, tail], axis=1)
    return c, sa, sb


def _axial_tables(seq):
    t = jnp.arange(seq, dtype=jnp.int32)
    f = _rope_freqs()[None, :]
    ar = (t // GRID_W).astype(F32)[:, None] * f
    ac = (t % GRID_W).astype(F32)[:, None] * f
    zeros = jnp.zeros_like(ar)
    c = jnp.concatenate([jnp.cos(ar), jnp.cos(ar), jnp.cos(ac), jnp.cos(ac)], axis=1)
    sa = jnp.concatenate([zeros, jnp.sin(ar), zeros, jnp.sin(ac)], axis=1)
    sb = jnp.concatenate([-jnp.sin(ar), zeros, -jnp.sin(ac), zeros], axis=1)
    return tuple(jnp.tile(x, (1, 2)) for x in (c, sa, sb))


def _head_block_diag():
    idx = np.arange(256) // C_HEAD_DIM
    return jnp.asarray(idx[:, None] == idx[None, :], BF16)


def _arrange_w_in_ab(w):
    qa, ka, va, cq, ckv, kr, gate = jnp.split(
        w, list(np.cumsum([A_WIDTH, A_WIDTH, A_WIDTH, MLA_Q_RANK, MLA_KV_RANK, MLA_ROPE_DIM])), axis=1)
    d = w.shape[0]
    kr_blk = jnp.concatenate([jnp.zeros((d, KR_LANE), F32), kr,
                              jnp.zeros((d, LANES - KR_LANE - MLA_ROPE_DIM), F32)], axis=1)
    qa = qa * (A_HEAD_DIM ** -0.5 * LOG2E)
    return jnp.concatenate([qa, ka, va, cq, ckv, kr_blk, gate], axis=1).astype(BF16)


def _arrange_w_in_c(w):
    d = w.shape[0]
    q, k, v, gate = jnp.split(w, list(np.cumsum([C_WIDTH, C_KV_HEADS * C_HEAD_DIM,
                                                 C_KV_HEADS * C_HEAD_DIM])), axis=1)
    k = k.reshape(d, C_KV_HEADS, 1, C_HEAD_DIM)
    kk = jnp.broadcast_to(k, (d, C_KV_HEADS, 2, C_HEAD_DIM)).reshape(d, C_KV_HEADS * LANES)
    v = v.reshape(d, C_KV_HEADS, C_HEAD_DIM)
    vv = jnp.concatenate([v, jnp.zeros_like(v)], axis=2).reshape(d, C_KV_HEADS * LANES)
    return jnp.concatenate([gate, q, kk, vv], axis=1).astype(BF16)


def _arrange_w_uq(w):
    r = w.shape[0]
    w = w.reshape(r, MLA_HEADS, MLA_NOPE_DIM + MLA_ROPE_DIM)
    pad = jnp.zeros((r, MLA_HEADS, LANES - MLA_NOPE_DIM - MLA_ROPE_DIM), F32)
    return jnp.concatenate([w, pad], axis=2).reshape(r, MLA_WIDE).astype(BF16)


def _arrange_w_ukv(w):
    r = w.shape[0]
    w = w.reshape(r, MLA_HEADS, MLA_NOPE_DIM + MLA_V_DIM)
    pad = jnp.zeros((r, MLA_HEADS, LANES - MLA_NOPE_DIM), F32)
    wk = jnp.concatenate([w[:, :, :MLA_NOPE_DIM], pad], axis=2).reshape(r, MLA_WIDE)
    wv = jnp.concatenate([w[:, :, MLA_NOPE_DIM:], pad], axis=2).reshape(r, MLA_WIDE)
    return wk.astype(BF16), wv.astype(BF16)


def _even_layer(x, b, t, g_pre, g_post, w_in, gq, wuq, gkv, wuk, wuv, w_out, bias, tabs):
    qkv, gate, qb, kb, vb = in_proj_ab(x, t, g_pre, w_in, gq, gkv, wuq, wuk, wuv, tabs)
    three = lambda a: a.reshape(b, t, a.shape[1])
    out_a = dilated_attention(three(qkv), bias)
    out_b = mla_attention(three(qb), three(kb), three(vb), tq=1024, tk=1024, unroll=4)
    return gate_out([out_a.reshape(b * t, A_WIDTH), out_b.reshape(b * t, B_WIDTH)], gate, x, w_out, g_post)


def _odd_layer(x, b, t, g_pre, g_post, w_in, q_norm, k_norm, w_out, bd, tabs):
    gate, q, kk, vv = in_proj_c(x, t, g_pre, w_in, q_norm, k_norm, bd, tabs)
    three = lambda a: a.reshape(b, t, a.shape[1])
    o = gqa_attention(three(q), three(kk), three(vv), tq=512, tk=1024, unroll=4)
    return gate_out([o.reshape(b * t, C_WIDTH)], gate, x, w_out, g_post)


def _trunk(x3, weights, consts):
    (norm_pre, norm_post, w_in_ab, mla_q_norm, w_uq, mla_kv_norm, w_uk, w_uv, w_out_ab,
     w_in_c, c_q_norm, c_k_norm, w_out_c) = weights
    b, t, d = x3.shape
    x = x3.reshape(b * t, d)
    bias, bd = consts
    mla_tabs = _mla_tables(t)
    axial_tabs = _axial_tables(t)
    depth = norm_pre.shape[0]
    for layer in range(depth):
        i = layer // 2
        if layer % 2 == 0:
            x = _even_layer(x, b, t, norm_pre[layer], norm_post[layer], w_in_ab[i], mla_q_norm[i],
                            w_uq[i], mla_kv_norm[i], w_uk[i], w_uv[i], w_out_ab[i], bias, mla_tabs)
        else:
            x = _odd_layer(x, b, t, norm_pre[layer], norm_post[layer], w_in_c[i], c_q_norm[i],
                           c_k_norm[i], w_out_c[i], bd, axial_tabs)
    return x.reshape(b, t, d)


def kernel(x_prompt, x_sample, norm_pre, norm_post, w_in_ab, mla_q_norm, w_uq, mla_kv_norm, w_ukv,
           w_out_ab, w_in_c, c_q_norm, c_k_norm, w_out_c):
    n_even, n_odd = w_in_ab.shape[0], w_in_c.shape[0]
    w_in_ab_p = jnp.stack([_arrange_w_in_ab(w_in_ab[i]) for i in range(n_even)])
    w_uq_p = jnp.stack([_arrange_w_uq(w_uq[i]) for i in range(n_even)])
    w_ukv_p = [_arrange_w_ukv(w_ukv[i]) for i in range(n_even)]
    w_uk_p = jnp.stack([p[0] for p in w_ukv_p])
    w_uv_p = jnp.stack([p[1] for p in w_ukv_p])
    w_in_c_p = jnp.stack([_arrange_w_in_c(w_in_c[i]) for i in range(n_odd)])
    weights = (norm_pre, norm_post, w_in_ab_p, mla_q_norm, w_uq_p, mla_kv_norm, w_uk_p, w_uv_p,
               w_out_ab.astype(BF16), w_in_c_p, c_q_norm, c_k_norm, w_out_c.astype(BF16))
    consts = (_dilated_bias(), _head_block_diag())
    return (_trunk(x_prompt, weights, consts), _trunk(x_sample, weights, consts))
```

```python
import functools
import math

import numpy as np
import jax
import jax.numpy as jnp
from jax import lax
from jax.experimental import pallas as pl
from jax.experimental.pallas import tpu as pltpu

F32 = jnp.float32
BF16 = jnp.bfloat16

D_MODEL = 1024
EPS = 1e-6
ROPE_THETA = 10000.0
NEG_INF = -1e30
LOG2E = math.log2(math.e)
GRID_W = 64
LANES = 128
A_HEADS = 8
A_HEAD_DIM = 64
DILATED_CONFIGS = ((128, 1), (512, 4), (2048, 16))
A_RADIUS = 64
A_REACH = max(w // 2 for w, _ in DILATED_CONFIGS)
MLA_HEADS = 8
MLA_Q_RANK = 256
MLA_KV_RANK = 128
MLA_NOPE_DIM = 64
MLA_ROPE_DIM = 32
MLA_V_DIM = 64
C_HEADS = 16
C_KV_HEADS = 4
C_GROUPS = C_HEADS // C_KV_HEADS
C_HEAD_DIM = 64
A_WIDTH = A_HEADS * A_HEAD_DIM
B_WIDTH = MLA_HEADS * MLA_V_DIM
AB_WIDTH = A_WIDTH + B_WIDTH
C_WIDTH = C_HEADS * C_HEAD_DIM
MLA_WIDE = MLA_HEADS * LANES
COL_QKV_A, COL_LATENT, COL_GATE_AB = 0, 1536, 2048
IN_AB_PAD = 3072
KR_LANE = MLA_NOPE_DIM
COL_GATE_C, COL_QC, COL_KC, COL_VC = 0, 1024, 2048, 2560
IN_C_PAD = 3072

VMEM_LIMIT = 56 * 1024 * 1024
ROW_TILE = 512


def _params(sem):
    return pltpu.CompilerParams(dimension_semantics=sem, vmem_limit_bytes=VMEM_LIMIT)


def _rms(x, g):
    ms = jnp.mean(x * x, axis=-1, keepdims=True)
    return x * lax.rsqrt(ms + EPS) * g


def _rope_lanes(x, c, sa, sb):
    return x * c + pltpu.roll(x, 16, 1) * sa + pltpu.roll(x, 112, 1) * sb


def _ones_lane_row(width, lane):
    idx = lax.broadcasted_iota(jnp.int32, (1, width), 1)
    return jnp.where(idx % LANES == lane, 1.0, 0.0).astype(F32)


def _pipelined(jobs):
    nxt = jobs[0][0]()
    for i, (_, consume) in enumerate(jobs):
        cur = nxt
        nxt = jobs[i + 1][0]() if i + 1 < len(jobs) else None
        consume(cur)


def _in_ab_kernel(x_ref, g_ref, w_ref, gq_ref, gkv_ref, wuq_ref, wuk_ref, wuv_ref,
                  c_ref, sa_ref, sb_ref, qkv_ref, gate_ref, qb_ref, kb_ref, vb_ref, *, q_scale):
    h = _rms(x_ref[...], g_ref[...]).astype(BF16)
    c, sa, sb = c_ref[...], sa_ref[...], sb_ref[...]

    def proj(c0):
        return lambda: jnp.dot(h, w_ref[:, c0:c0 + 512], preferred_element_type=F32)

    def store(ref, c0):
        def consume(v):
            ref[:, c0:c0 + 512] = v.astype(BF16)
        return consume

    latent = {}

    def up_project(lat):
        cq = _rms(lat[:, :MLA_Q_RANK], gq_ref[...]).astype(BF16)
        ckv = _rms(lat[:, MLA_Q_RANK:MLA_Q_RANK + MLA_KV_RANK], gkv_ref[...]).astype(BF16)
        latent["kr"] = _rope_lanes(lat[:, MLA_Q_RANK + MLA_KV_RANK:], c, sa, sb)
        latent["qf"] = jnp.dot(cq, wuq_ref[...], preferred_element_type=F32)
        latent["kf"] = jnp.dot(ckv, wuk_ref[...], preferred_element_type=F32)
        vf = jnp.dot(ckv, wuv_ref[...], preferred_element_type=F32)
        vb_ref[...] = (vf + _ones_lane_row(MLA_WIDE, MLA_V_DIM)).astype(BF16)

    def rope_heads(ref, c0, heads):
        store_plain = store(ref, c0)

        def consume(v):
            store_plain(v)
            for hd in heads:
                sl = slice(hd * LANES, (hd + 1) * LANES)
                qb_ref[:, sl] = (_rope_lanes(latent["qf"][:, sl], c, sa, sb) * q_scale).astype(BF16)
                kb_ref[:, sl] = (latent["kf"][:, sl] + latent["kr"]).astype(BF16)
        return consume

    half = MLA_HEADS // 2
    _pipelined([
        (proj(COL_LATENT), up_project),
        (proj(COL_QKV_A), store(qkv_ref, 0)),
        (proj(COL_QKV_A + 512), store(qkv_ref, 512)),
        (proj(COL_QKV_A + 1024), rope_heads(qkv_ref, 1024, range(0, half))),
        (proj(COL_GATE_AB), rope_heads(gate_ref, 0, range(half, MLA_HEADS))),
        (proj(COL_GATE_AB + 512), store(gate_ref, 512)),
    ])


def in_proj_ab(x, seq, g, w, gq, gkv, wuq, wuk, wuv, tabs):
    m, d = x.shape
    tiles_per_seq = seq // ROW_TILE
    row = lambda i: (i, 0)
    const = lambda i: (0, 0)
    pos = lambda i: (i % tiles_per_seq, 0)
    out_w = (COL_LATENT, AB_WIDTH, MLA_WIDE, MLA_WIDE, MLA_WIDE)
    return pl.pallas_call(
        functools.partial(_in_ab_kernel, q_scale=(MLA_NOPE_DIM + MLA_ROPE_DIM) ** -0.5 * LOG2E),
        grid=(m // ROW_TILE,),
        in_specs=[pl.BlockSpec((ROW_TILE, d), row),
                  pl.BlockSpec((1, d), const),
                  pl.BlockSpec(w.shape, const),
                  pl.BlockSpec((1, MLA_Q_RANK), const),
                  pl.BlockSpec((1, MLA_KV_RANK), const),
                  pl.BlockSpec(wuq.shape, const),
                  pl.BlockSpec(wuk.shape, const),
                  pl.BlockSpec(wuv.shape, const)] + [pl.BlockSpec((ROW_TILE, LANES), pos)] * 3,
        out_specs=[pl.BlockSpec((ROW_TILE, n), row) for n in out_w],
        out_shape=[jax.ShapeDtypeStruct((m, n), BF16) for n in out_w],
        compiler_params=_params(("parallel",)),
        name="in_proj_ab",
    )(x, g.reshape(1, d), w, gq.reshape(1, -1), gkv.reshape(1, -1), wuq, wuk, wuv, *tabs)


def _in_c_kernel(x_ref, g_ref, w_ref, gq_ref, gk_ref, bd_ref, c_ref, sa_ref, sb_ref,
                 gate_ref, q_ref, kk_ref, vv_ref, *, q_scale):
    h = _rms(x_ref[...], g_ref[...]).astype(BF16)
    c, sa, sb = c_ref[...], sa_ref[...], sb_ref[...]
    bd = bd_ref[...]

    def proj(c0):
        return lambda: jnp.dot(h, w_ref[:, c0:c0 + 512], preferred_element_type=F32)

    def head_norm_rope(x, gain, scale):
        sq = x * x
        hi = sq.astype(BF16)
        lo = (sq - hi.astype(F32)).astype(BF16)
        ss = (jnp.dot(hi, bd, preferred_element_type=F32) +
              jnp.dot(lo, bd, preferred_element_type=F32))
        xn = x * lax.rsqrt(ss * (1.0 / C_HEAD_DIM) + EPS) * gain
        halves = [_rope_lanes(xn[:, s:s + LANES], c, sa, sb) for s in (0, LANES)]
        return (jnp.concatenate(halves, axis=1) * scale).astype(BF16)

    def store(ref, c0):
        def consume(v):
            ref[:, c0:c0 + 512] = v.astype(BF16)
        return consume

    def store_heads(ref, c0, gain_ref, scale):
        def consume(v):
            for s in (0, 256):
                ref[:, c0 + s:c0 + s + 256] = head_norm_rope(v[:, s:s + 256], gain_ref[...], scale)
        return consume

    def store_values(v):
        vv_ref[...] = (v + _ones_lane_row(v.shape[1], C_HEAD_DIM)).astype(BF16)

    _pipelined([
        (proj(COL_QC), store_heads(q_ref, 0, gq_ref, q_scale)),
        (proj(COL_GATE_C), store(gate_ref, 0)),
        (proj(COL_QC + 512), store_heads(q_ref, 512, gq_ref, q_scale)),
        (proj(COL_GATE_C + 512), store(gate_ref, 512)),
        (proj(COL_KC), store_heads(kk_ref, 0, gk_ref, 1.0)),
        (proj(COL_VC), store_values),
    ])


def in_proj_c(x, seq, g, w, gq, gk, bd, tabs):
    m, d = x.shape
    tiles_per_seq = seq // ROW_TILE
    row = lambda i: (i, 0)
    const = lambda i: (0, 0)
    pos = lambda i: (i % tiles_per_seq, 0)
    out_w = (C_WIDTH, C_WIDTH, C_KV_HEADS * LANES, C_KV_HEADS * LANES)
    tile4 = lambda v: jnp.tile(v.reshape(1, -1), (1, 4))
    return pl.pallas_call(
        functools.partial(_in_c_kernel, q_scale=C_HEAD_DIM ** -0.5 * LOG2E),
        grid=(m // ROW_TILE,),
        in_specs=[pl.BlockSpec((ROW_TILE, d), row),
                  pl.BlockSpec((1, d), const),
                  pl.BlockSpec(w.shape, const),
                  pl.BlockSpec((1, 256), const),
                  pl.BlockSpec((1, 256), const),
                  pl.BlockSpec((256, 256), const)] + [pl.BlockSpec((ROW_TILE, LANES), pos)] * 3,
        out_specs=[pl.BlockSpec((ROW_TILE, n), row) for n in out_w],
        out_shape=[jax.ShapeDtypeStruct((m, n), BF16) for n in out_w],
        compiler_params=_params(("parallel",)),
        name="in_proj_c",
    )(x, g.reshape(1, d), w, tile4(gq), tile4(gk), bd, *tabs)


def _online_softmax(q, k_ref, v_ref, m_ref, acc_ref, *, tk, unroll, v_groups):
    m_ref[...] = jnp.full(m_ref.shape, NEG_INF, F32)
    acc_ref[...] = jnp.zeros(acc_ref.shape, F32)

    def body(j, carry):
        start = pl.multiple_of(j * tk, tk)
        k = k_ref[0, pl.ds(start, tk), :]
        s = lax.dot_general(q, k, (((1,), (1,)), ((), ())), preferred_element_type=F32)
        m_old = m_ref[...]
        m_new = jnp.maximum(m_old, jnp.max(s, axis=-1, keepdims=True))
        alpha = jnp.exp2(m_old - m_new)
        p = jnp.exp2(s - jnp.tile(m_new, (1, tk // LANES))).astype(BF16)
        for r0, r1, l0 in v_groups:
            v = v_ref[0, pl.ds(start, tk), l0:l0 + LANES]
            acc_ref[r0:r1] = alpha[r0:r1] * acc_ref[r0:r1] + jnp.dot(
                p[r0:r1], v, preferred_element_type=F32)
        m_ref[...] = m_new
        return carry

    lax.fori_loop(0, k_ref.shape[1] // tk, body, 0, unroll=unroll)


def _normalised(acc, dv):
    return acc[:, :dv] / acc[:, dv:dv + 1]


def _gqa_kernel(q_ref, k_ref, v_ref, o_ref, m_ref, acc_ref, *, tk, unroll):
    tq = q_ref.shape[1]
    q = q_ref[0].astype(F32)
    low = lax.broadcasted_iota(jnp.int32, (tq, LANES), 1) < C_HEAD_DIM
    parts = []
    for half in (q[:, :LANES], q[:, LANES:]):
        parts += [jnp.where(low, half, 0.0), jnp.where(low, 0.0, half)]
    qs = jnp.concatenate(parts, axis=0).astype(BF16)
    _online_softmax(qs, k_ref, v_ref, m_ref, acc_ref, tk=tk, unroll=unroll,
                    v_groups=((0, C_GROUPS * tq, 0),))
    acc = acc_ref[...]
    o_ref[0] = jnp.concatenate(
        [_normalised(acc[g * tq:(g + 1) * tq], C_HEAD_DIM) for g in range(C_GROUPS)],
        axis=1).astype(o_ref.dtype)


def gqa_attention(q, kk, vv, *, tq, tk, unroll):
    b, t, _ = q.shape
    rows = C_GROUPS * tq
    return pl.pallas_call(
        functools.partial(_gqa_kernel, tk=tk, unroll=unroll),
        grid=(b, C_KV_HEADS, t // tq),
        in_specs=[pl.BlockSpec((1, tq, 256), lambda i, j, n: (i, n, j)),
                  pl.BlockSpec((1, t, LANES), lambda i, j, n: (i, 0, j)),
                  pl.BlockSpec((1, t, LANES), lambda i, j, n: (i, 0, j))],
        out_specs=pl.BlockSpec((1, tq, 256), lambda i, j, n: (i, n, j)),
        out_shape=jax.ShapeDtypeStruct((b, t, C_WIDTH), BF16),
        scratch_shapes=[pltpu.VMEM((rows, LANES), F32), pltpu.VMEM((rows, LANES), F32)],
        compiler_params=_params(("parallel", "parallel", "arbitrary")),
        name="gqa_attention",
    )(q, kk, vv)


def _mla_kernel(q_ref, k_ref, v_ref, o_ref, m_ref, acc_ref, *, tk, unroll):
    tq = q_ref.shape[1]
    q = q_ref[0]
    zeros = jnp.zeros((tq, LANES), q.dtype)
    qs = jnp.concatenate([jnp.concatenate([q[:, :LANES], zeros], axis=1),
                          jnp.concatenate([zeros, q[:, LANES:]], axis=1)], axis=0)
    _online_softmax(qs, k_ref, v_ref, m_ref, acc_ref, tk=tk, unroll=unroll,
                    v_groups=((0, tq, 0), (tq, 2 * tq, LANES)))
    acc = acc_ref[...]
    o_ref[0] = jnp.concatenate(
        [_normalised(acc[g * tq:(g + 1) * tq], MLA_V_DIM) for g in range(2)],
        axis=1).astype(o_ref.dtype)


def mla_attention(qb, kb, vb, *, tq, tk, unroll):
    b, t, _ = qb.shape
    rows = 2 * tq
    return pl.pallas_call(
        functools.partial(_mla_kernel, tk=tk, unroll=unroll),
        grid=(b, MLA_HEADS // 2, t // tq),
        in_specs=[pl.BlockSpec((1, tq, 256), lambda i, j, n: (i, n, j)),
                  pl.BlockSpec((1, t, 256), lambda i, j, n: (i, 0, j)),
                  pl.BlockSpec((1, t, 256), lambda i, j, n: (i, 0, j))],
        out_specs=pl.BlockSpec((1, tq, LANES), lambda i, j, n: (i, n, j)),
        out_shape=jax.ShapeDtypeStruct((b, t, B_WIDTH), BF16),
        scratch_shapes=[pltpu.VMEM((rows, LANES), F32), pltpu.VMEM((rows, LANES), F32)],
        compiler_params=_params(("parallel", "parallel", "arbitrary")),
        name="mla_attention",
    )(qb, kb, vb)


A_TQ = 2048
A_SUB = 128
A_BAND = A_SUB + 2 * A_RADIUS
A_CLASSES = 16
A_LOCKSTEP = 8
A_PAD = A_RADIUS


def _class_runs(cfg, sub):
    if cfg == 2:
        return [(pl.multiple_of(sub * A_SUB, A_SUB), A_SUB)]
    if cfg == 1:
        r4, blk = sub & 3, sub >> 2
        return [(pl.multiple_of((r4 + 4 * a) * A_SUB + blk * 32, 32), 32) for a in range(4)]
    return [(pl.multiple_of(r * A_SUB + sub * 8, 8), 8) for r in range(A_CLASSES)]


def _dilated_kernel(q_ref, k_ref, v_ref, b_ref, o_ref,
                    nat_ref, mid_ref, qc_ref, kp1, kp16, vp1, vp16, oacc_ref, m_ref, l_ref):
    seq = k_ref.shape[1]
    units16 = seq // A_CLASSES
    qi = pl.program_id(2)

    def split_classes(n_rows, emit):
        quarter = n_rows // 4
        for r4 in range(4):
            mid_ref[r4 * quarter:(r4 + 1) * quarter] = nat_ref[pl.ds(r4, quarter, stride=4), :]
        for r4 in range(4):
            for a in range(4):
                emit(r4 + 4 * a, mid_ref[pl.ds(r4 * quarter + a, quarter // 4, stride=4), :])

    @pl.when(qi == 0)
    def _():
        zeros = jnp.zeros((A_PAD, LANES), BF16)
        for src, nat_pad, cls_pad in ((k_ref, kp1, kp16), (v_ref, vp1, vp16)):
            nat_pad[0:A_PAD] = zeros
            nat_pad[A_PAD + seq:] = zeros
            nat_pad[A_PAD:A_PAD + seq] = src[0]
            nat_ref[...] = src[0].astype(F32)

            def emit(r, rows, cls_pad=cls_pad):
                cls_pad[r, 0:A_PAD] = zeros
                cls_pad[r, A_PAD + units16:] = zeros
                cls_pad[r, A_PAD:A_PAD + units16] = rows.astype(BF16)

            split_classes(seq, emit)

    nat_ref[0:A_TQ] = q_ref[0].astype(F32)

    def emit_q(r, rows):
        qc_ref[r * A_SUB:(r + 1) * A_SUB] = rows

    split_classes(A_TQ, emit_q)
    low = lax.broadcasted_iota(jnp.int32, (A_SUB, LANES), 1) < A_HEAD_DIM
    col = lax.broadcasted_iota(jnp.int32, (1, A_BAND), 1)
    col_unit = (col, 4 * (col % 64) + col // 64, col)

    for ci, (_, dil) in enumerate(DILATED_CONFIGS):
        units = seq // dil
        per_step = A_TQ // dil

        def body(it, carry, ci=ci, units=units, per_step=per_step):
            subs = [it * A_LOCKSTEP + u for u in range(A_LOCKSTEP)]
            if ci == 2:
                first = [qi * per_step] * len(subs)
            elif ci == 1:
                first = [qi * per_step + (s_ >> 2) * A_SUB for s_ in subs]
            else:
                first = [qi * per_step + s_ * A_SUB for s_ in subs]
            runs = [_class_runs(ci, s_) for s_ in subs]

            def window(nat_pad, cls_pad, sub, f):
                if ci == 0:
                    return nat_pad[pl.ds(pl.multiple_of(f, A_SUB), A_BAND), :]
                if ci == 2:
                    return cls_pad[sub, pl.ds(pl.multiple_of(f, A_SUB), A_BAND), :]
                row = pl.multiple_of((f >> 2) + A_PAD - A_RADIUS // 4, 16)
                return jnp.concatenate([cls_pad[(sub & 3) + 4 * a, pl.ds(row, A_BAND // 4), :]
                                        for a in range(4)], axis=0)

            def gather(ref, sub_runs):
                parts = [ref[pl.ds(start, size), :] for start, size in sub_runs]
                return parts[0] if len(parts) == 1 else jnp.concatenate(parts, axis=0)

            def scatter(ref, sub_runs, val):
                row = 0
                for start, size in sub_runs:
                    ref[pl.ds(start, size), :] = val[row:row + size]
                    row += size

            qs = []
            for sub_runs in runs:
                q = gather(qc_ref, sub_runs)
                qs.append(jnp.concatenate([jnp.where(low, q, 0.0), jnp.where(low, 0.0, q)],
                                          axis=0).astype(BF16))
            kw = [window(kp1, kp16, s_, f) for s_, f in zip(subs, first)]
            vw = [window(vp1, vp16, s_, f) for s_, f in zip(subs, first)]
            s = [lax.dot_general(a, b_, (((1,), (1,)), ((), ())), preferred_element_type=F32)
                 for a, b_ in zip(qs, kw)]
            bias = b_ref[0, ci]
            masked = []
            for s_u, f in zip(s, first):
                unit = f - A_RADIUS + col_unit[ci]
                valid = jnp.where((unit >= 0) & (unit < units), 0.0, NEG_INF)
                masked.append(s_u + bias + valid)
            m_blk = [jnp.max(x, axis=-1, keepdims=True) for x in masked]
            p = [jnp.exp2(x - m) for x, m in zip(masked, m_blk)]
            l_blk = [jnp.sum(x, axis=-1, keepdims=True) for x in p]
            o_blk = [jnp.dot(x.astype(BF16), v, preferred_element_type=F32) for x, v in zip(p, vw)]
            for sub_runs, o_u, m_u, l_u in zip(runs, o_blk, m_blk, l_blk):
                o2 = jnp.where(low, o_u[:A_SUB], o_u[A_SUB:])
                m2 = jnp.where(low, m_u[:A_SUB], m_u[A_SUB:])
                l2 = jnp.where(low, l_u[:A_SUB], l_u[A_SUB:])
                if ci == 0:
                    scatter(oacc_ref, sub_runs, o2)
                    scatter(m_ref, sub_runs, m2)
                    scatter(l_ref, sub_runs, l2)
                else:
                    m_old = gather(m_ref, sub_runs)
                    m_new = jnp.maximum(m_old, m2)
                    a_old = jnp.exp2(m_old - m_new)
                    a_blk = jnp.exp2(m2 - m_new)
                    scatter(oacc_ref, sub_runs, gather(oacc_ref, sub_runs) * a_old + o2 * a_blk)
                    scatter(l_ref, sub_runs, gather(l_ref, sub_runs) * a_old + l2 * a_blk)
                    scatter(m_ref, sub_runs, m_new)
            return carry

        lax.fori_loop(0, A_TQ // A_SUB // A_LOCKSTEP, body, 0)

    quarter = A_TQ // 4
    for r4 in range(4):
        for a in range(4):
            rows = slice((r4 + 4 * a) * A_SUB, (r4 + 4 * a + 1) * A_SUB)
            mid_ref[pl.ds(r4 * quarter + a, A_SUB, stride=4), :] = oacc_ref[rows] / l_ref[rows]
    for r4 in range(4):
        nat_ref[pl.ds(r4, quarter, stride=4), :] = mid_ref[r4 * quarter:(r4 + 1) * quarter]
    o_ref[0] = nat_ref[0:A_TQ].astype(o_ref.dtype)


def dilated_attention(qkv, bias):
    b, t, _ = qkv.shape
    pairs = A_HEADS // 2
    state = pltpu.VMEM((A_TQ, LANES), F32)
    nat_pad = pltpu.VMEM((t + 2 * A_PAD, LANES), BF16)
    cls_pad = pltpu.VMEM((A_CLASSES, t // A_CLASSES + 2 * A_PAD, LANES), BF16)
    return pl.pallas_call(
        _dilated_kernel,
        grid=(b, pairs, t // A_TQ),
        in_specs=[pl.BlockSpec((1, A_TQ, LANES), lambda i, j, n: (i, n, j)),
                  pl.BlockSpec((1, t, LANES), lambda i, j, n: (i, 0, pairs + j)),
                  pl.BlockSpec((1, t, LANES), lambda i, j, n: (i, 0, 2 * pairs + j)),
                  pl.BlockSpec((1, len(DILATED_CONFIGS), 2 * A_SUB, A_BAND), lambda i, j, n: (j, 0, 0, 0))],
        out_specs=pl.BlockSpec((1, A_TQ, LANES), lambda i, j, n: (i, n, j)),
        out_shape=jax.ShapeDtypeStruct((b, t, A_WIDTH), BF16),
        scratch_shapes=[pltpu.VMEM((t, LANES), F32), pltpu.VMEM((t, LANES), F32), state,
                        nat_pad, cls_pad, nat_pad, cls_pad, state, state, state],
        compiler_params=_params(("parallel", "parallel", "arbitrary")),
        name="dilated_attention",
    )(qkv, qkv, qkv, bias)


def _dilated_bias():
    i = jnp.arange(A_SUB, dtype=jnp.int32)[:, None]
    c = jnp.arange(A_BAND, dtype=jnp.int32)[None, :]
    u_rel = {1: 16 * (i % 8) + i // 8, 4: 4 * (i % 32) + i // 32, 16: i}
    k_rel = {1: c, 4: 4 * (c % 64) + c // 64, 16: c}
    slopes = 2.0 ** (-8.0 * jnp.arange(1, A_HEADS + 1, dtype=F32) / A_HEADS)
    per_cfg = []
    for _, dil in DILATED_CONFIGS:
        hops = jnp.abs(k_rel[dil] - A_RADIUS - u_rel[dil])
        dist = (dil * hops).astype(F32)
        bias = jnp.where(hops <= A_RADIUS, -slopes[:, None, None] * dist[None] * LOG2E, NEG_INF)
        per_cfg.append(bias.reshape(A_HEADS // 2, 2 * A_SUB, A_BAND))
    return jnp.stack(per_cfg, axis=1)


def _gate_out_kernel(*refs, n_parts):
    parts = refs[:n_parts]
    gate_ref, x_ref, w_ref, g_ref, o_ref = refs[n_parts:]
    o = jnp.concatenate([r[...] for r in parts], axis=-1) if n_parts > 1 else parts[0][...]
    gate = gate_ref[...].astype(F32)
    y = (o.astype(F32) * (gate * jax.nn.sigmoid(gate))).astype(BF16)
    y = jnp.dot(y, w_ref[...], preferred_element_type=F32)
    o_ref[...] = x_ref[...] + _rms(y, g_ref[...])


def gate_out(parts, gate, x, w, g):
    m, d = x.shape
    row = lambda i: (i, 0)
    const = lambda i: (0, 0)
    in_specs = [pl.BlockSpec((ROW_TILE, p.shape[1]), row) for p in parts]
    in_specs += [pl.BlockSpec((ROW_TILE, gate.shape[1]), row),
                 pl.BlockSpec((ROW_TILE, d), row),
                 pl.BlockSpec(w.shape, const),
                 pl.BlockSpec((1, d), const)]
    return pl.pallas_call(
        functools.partial(_gate_out_kernel, n_parts=len(parts)),
        grid=(m // ROW_TILE,),
        in_specs=in_specs,
        out_specs=pl.BlockSpec((ROW_TILE, d), row),
        out_shape=jax.ShapeDtypeStruct((m, d), F32),
        compiler_params=_params(("parallel",)),
        name="gate_out",
    )(*parts, gate, x, w, g.reshape(1, d))


def _rope_freqs():
    half = MLA_ROPE_DIM // 2
    return ROPE_THETA ** (-jnp.arange(half, dtype=F32) / half)


def _mla_tables(seq):
    ang = jnp.arange(seq, dtype=F32)[:, None] * _rope_freqs()[None, :]
    cos, sin = jnp.cos(ang), jnp.sin(ang)
    zeros = jnp.zeros_like(cos)
    ones_nope = jnp.ones((seq, MLA_NOPE_DIM), F32)
    zeros_nope = jnp.zeros((seq, MLA_NOPE_DIM), F32)
    tail = jnp.zeros((seq, LANES - MLA_NOPE_DIM - MLA_ROPE_DIM), F32)
    c = jnp.concatenate([ones_nope, cos, cos, tail], axis=1)
    sa = jnp.concatenate([zeros_nope, zeros, sin, tail], axis=1)
    sb = jnp.concatenate([zeros_nope, -sin, zeros, tail], axis=1)
    return c, sa, sb


def _axial_tables(seq):
    t = jnp.arange(seq, dtype=jnp.int32)
    f = _rope_freqs()[None, :]
    ar = (t // GRID_W).astype(F32)[:, None] * f
    ac = (t % GRID_W).astype(F32)[:, None] * f
    zeros = jnp.zeros_like(ar)
    c = jnp.concatenate([jnp.cos(ar), jnp.cos(ar), jnp.cos(ac), jnp.cos(ac)], axis=1)
    sa = jnp.concatenate([zeros, jnp.sin(ar), zeros, jnp.sin(ac)], axis=1)
    sb = jnp.concatenate([-jnp.sin(ar), zeros, -jnp.sin(ac), zeros], axis=1)
    return tuple(jnp.tile(x, (1, 2)) for x in (c, sa, sb))


def _head_block_diag():
    idx = np.arange(256) // C_HEAD_DIM
    return jnp.asarray(idx[:, None] == idx[None, :], BF16)


def _arrange_w_in_ab(w):
    qa, ka, va, cq, ckv, kr, gate = jnp.split(
        w, list(np.cumsum([A_WIDTH, A_WIDTH, A_WIDTH, MLA_Q_RANK, MLA_KV_RANK, MLA_ROPE_DIM])), axis=1)
    d = w.shape[0]
    kr_blk = jnp.concatenate([jnp.zeros((d, KR_LANE), F32), kr,
                              jnp.zeros((d, LANES - KR_LANE - MLA_ROPE_DIM), F32)], axis=1)
    qa = qa * (A_HEAD_DIM ** -0.5 * LOG2E)
    return jnp.concatenate([qa, ka, va, cq, ckv, kr_blk, gate], axis=1).astype(BF16)


def _arrange_w_in_c(w):
    d = w.shape[0]
    q, k, v, gate = jnp.split(w, list(np.cumsum([C_WIDTH, C_KV_HEADS * C_HEAD_DIM,
                                                 C_KV_HEADS * C_HEAD_DIM])), axis=1)
    k = k.reshape(d, C_KV_HEADS, 1, C_HEAD_DIM)
    kk = jnp.broadcast_to(k, (d, C_KV_HEADS, 2, C_HEAD_DIM)).reshape(d, C_KV_HEADS * LANES)
    v = v.reshape(d, C_KV_HEADS, C_HEAD_DIM)
    vv = jnp.concatenate([v, jnp.zeros_like(v)], axis=2).reshape(d, C_KV_HEADS * LANES)
    return jnp.concatenate([gate, q, kk, vv], axis=1).astype(BF16)


def _arrange_w_uq(w):
    r = w.shape[0]
    w = w.reshape(r, MLA_HEADS, MLA_NOPE_DIM + MLA_ROPE_DIM)
    pad = jnp.zeros((r, MLA_HEADS, LANES - MLA_NOPE_DIM - MLA_ROPE_DIM), F32)
    return jnp.concatenate([w, pad], axis=2).reshape(r, MLA_WIDE).astype(BF16)


def _arrange_w_ukv(w):
    r = w.shape[0]
    w = w.reshape(r, MLA_HEADS, MLA_NOPE_DIM + MLA_V_DIM)
    pad = jnp.zeros((r, MLA_HEADS, LANES - MLA_NOPE_DIM), F32)
    wk = jnp.concatenate([w[:, :, :MLA_NOPE_DIM], pad], axis=2).reshape(r, MLA_WIDE)
    wv = jnp.concatenate([w[:, :, MLA_NOPE_DIM:], pad], axis=2).reshape(r, MLA_WIDE)
    return wk.astype(BF16), wv.astype(BF16)


def _even_layer(x, b, t, g_pre, g_post, w_in, gq, wuq, gkv, wuk, wuv, w_out, bias, tabs):
    qkv, gate, qb, kb, vb = in_proj_ab(x, t, g_pre, w_in, gq, gkv, wuq, wuk, wuv, tabs)
    three = lambda a: a.reshape(b, t, a.shape[1])
    out_a = dilated_attention(three(qkv), bias)
    out_b = mla_attention(three(qb), three(kb), three(vb), tq=1024, tk=1024, unroll=4)
    return gate_out([out_a.reshape(b * t, A_WIDTH), out_b.reshape(b * t, B_WIDTH)], gate, x, w_out, g_post)


def _odd_layer(x, b, t, g_pre, g_post, w_in, q_norm, k_norm, w_out, bd, tabs):
    gate, q, kk, vv = in_proj_c(x, t, g_pre, w_in, q_norm, k_norm, bd, tabs)
    three = lambda a: a.reshape(b, t, a.shape[1])
    o = gqa_attention(three(q), three(kk), three(vv), tq=512, tk=1024, unroll=4)
    return gate_out([o.reshape(b * t, C_WIDTH)], gate, x, w_out, g_post)


def _trunk(x3, weights, consts):
    (norm_pre, norm_post, w_in_ab, mla_q_norm, w_uq, mla_kv_norm, w_uk, w_uv, w_out_ab,
     w_in_c, c_q_norm, c_k_norm, w_out_c) = weights
    b, t, d = x3.shape
    x = x3.reshape(b * t, d)
    bias, bd = consts
    mla_tabs = _mla_tables(t)
    axial_tabs = _axial_tables(t)
    depth = norm_pre.shape[0]
    for layer in range(depth):
        i = layer // 2
        if layer % 2 == 0:
            x = _even_layer(x, b, t, norm_pre[layer], norm_post[layer], w_in_ab[i], mla_q_norm[i],
                            w_uq[i], mla_kv_norm[i], w_uk[i], w_uv[i], w_out_ab[i], bias, mla_tabs)
        else:
            x = _odd_layer(x, b, t, norm_pre[layer], norm_post[layer], w_in_c[i], c_q_norm[i],
                           c_k_norm[i], w_out_c[i], bd, axial_tabs)
    return x.reshape(b, t, d)


def kernel(x_prompt, x_sample, norm_pre, norm_post, w_in_ab, mla_q_norm, w_uq, mla_kv_norm, w_ukv,
           w_out_ab, w_in_c, c_q_norm, c_k_norm, w_out_c):
    n_even, n_odd = w_in_ab.shape[0], w_in_c.shape[0]
    w_in_ab_p = jnp.stack([_arrange_w_in_ab(w_in_ab[i]) for i in range(n_even)])
    w_uq_p = jnp.stack([_arrange_w_uq(w_uq[i]) for i in range(n_even)])
    w_ukv_p = [_arrange_w_ukv(w_ukv[i]) for i in range(n_even)]
    w_uk_p = jnp.stack([p[0] for p in w_ukv_p])
    w_uv_p = jnp.stack([p[1] for p in w_ukv_p])
    w_in_c_p = jnp.stack([_arrange_w_in_c(w_in_c[i]) for i in range(n_odd)])
    weights = (norm_pre, norm_post, w_in_ab_p, mla_q_norm, w_uq_p, mla_kv_norm, w_uk_p, w_uv_p,
               w_out_ab.astype(BF16), w_in_c_p, c_q_norm, c_k_norm, w_out_c.astype(BF16))
    consts = (_dilated_bias(), _head_block_diag())
    return (_trunk(x_prompt, weights, consts), _trunk(x_sample, weights, consts))
```

```python
import functools
import math

import numpy as np
import jax
import jax.numpy as jnp
from jax import lax
from jax.experimental import pallas as pl
from jax.experimental.pallas import tpu as pltpu

F32 = jnp.float32
BF16 = jnp.bfloat16

D_MODEL = 1024
EPS = 1e-6
ROPE_THETA = 10000.0
NEG_INF = -1e30
LOG2E = math.log2(math.e)
GRID_W = 64
LANES = 128
A_HEADS = 8
A_HEAD_DIM = 64
DILATED_CONFIGS = ((128, 1), (512, 4), (2048, 16))
A_RADIUS = 64
A_REACH = max(w // 2 for w, _ in DILATED_CONFIGS)
MLA_HEADS = 8
MLA_Q_RANK = 256
MLA_KV_RANK = 128
MLA_NOPE_DIM = 64
MLA_ROPE_DIM = 32
MLA_V_DIM = 64
C_HEADS = 16
C_KV_HEADS = 4
C_GROUPS = C_HEADS // C_KV_HEADS
C_HEAD_DIM = 64
A_WIDTH = A_HEADS * A_HEAD_DIM
B_WIDTH = MLA_HEADS * MLA_V_DIM
AB_WIDTH = A_WIDTH + B_WIDTH
C_WIDTH = C_HEADS * C_HEAD_DIM
MLA_WIDE = MLA_HEADS * LANES
COL_QKV_A, COL_LATENT, COL_GATE_AB = 0, 1536, 2048
IN_AB_PAD = 3072
KR_LANE = MLA_NOPE_DIM
COL_GATE_C, COL_QC, COL_KC, COL_VC = 0, 1024, 2048, 2560
IN_C_PAD = 3072

VMEM_LIMIT = 56 * 1024 * 1024
ROW_TILE = 512


def _params(sem):
    return pltpu.CompilerParams(dimension_semantics=sem, vmem_limit_bytes=VMEM_LIMIT)


def _rms(x, g):
    ms = jnp.mean(x * x, axis=-1, keepdims=True)
    return x * lax.rsqrt(ms + EPS) * g


def _rope_lanes(x, c, sa, sb):
    return x * c + pltpu.roll(x, 16, 1) * sa + pltpu.roll(x, 112, 1) * sb


def _ones_lane_row(width, lane):
    idx = lax.broadcasted_iota(jnp.int32, (1, width), 1)
    return jnp.where(idx % LANES == lane, 1.0, 0.0).astype(F32)


def _pipelined(jobs):
    nxt = jobs[0][0]()
    for i, (_, consume) in enumerate(jobs):
        cur = nxt
        nxt = jobs[i + 1][0]() if i + 1 < len(jobs) else None
        consume(cur)


def _in_ab_kernel(x_ref, g_ref, w_ref, gq_ref, gkv_ref, wuq_ref, wuk_ref, wuv_ref,
                  c_ref, sa_ref, sb_ref, qkv_ref, gate_ref, qb_ref, kb_ref, vb_ref, *, q_scale):
    h = _rms(x_ref[...], g_ref[...]).astype(BF16)
    c, sa, sb = c_ref[...], sa_ref[...], sb_ref[...]

    def proj(c0):
        return lambda: jnp.dot(h, w_ref[:, c0:c0 + 512], preferred_element_type=F32)

    def store(ref, c0):
        def consume(v):
            ref[:, c0:c0 + 512] = v.astype(BF16)
        return consume

    latent = {}

    def up_project(lat):
        cq = _rms(lat[:, :MLA_Q_RANK], gq_ref[...]).astype(BF16)
        ckv = _rms(lat[:, MLA_Q_RANK:MLA_Q_RANK + MLA_KV_RANK], gkv_ref[...]).astype(BF16)
        latent["kr"] = _rope_lanes(lat[:, MLA_Q_RANK + MLA_KV_RANK:], c, sa, sb)
        latent["qf"] = jnp.dot(cq, wuq_ref[...], preferred_element_type=F32)
        latent["kf"] = jnp.dot(ckv, wuk_ref[...], preferred_element_type=F32)
        vf = jnp.dot(ckv, wuv_ref[...], preferred_element_type=F32)
        vb_ref[...] = (vf + _ones_lane_row(MLA_WIDE, MLA_V_DIM)).astype(BF16)

    def rope_heads(ref, c0, heads):
        store_plain = store(ref, c0)

        def consume(v):
            store_plain(v)
            for hd in heads:
                sl = slice(hd * LANES, (hd + 1) * LANES)
                qb_ref[:, sl] = (_rope_lanes(latent["qf"][:, sl], c, sa, sb) * q_scale).astype(BF16)
                kb_ref[:, sl] = (latent["kf"][:, sl] + latent["kr"]).astype(BF16)
        return consume

    half = MLA_HEADS // 2
    _pipelined([
        (proj(COL_LATENT), up_project),
        (proj(COL_QKV_A), store(qkv_ref, 0)),
        (proj(COL_QKV_A + 512), store(qkv_ref, 512)),
        (proj(COL_QKV_A + 1024), rope_heads(qkv_ref, 1024, range(0, half))),
        (proj(COL_GATE_AB), rope_heads(gate_ref, 0, range(half, MLA_HEADS))),
        (proj(COL_GATE_AB + 512), store(gate_ref, 512)),
    ])


def in_proj_ab(x, seq, g, w, gq, gkv, wuq, wuk, wuv, tabs):
    m, d = x.shape
    tiles_per_seq = seq // ROW_TILE
    row = lambda i: (i, 0)
    const = lambda i: (0, 0)
    pos = lambda i: (i % tiles_per_seq, 0)
    out_w = (COL_LATENT, AB_WIDTH, MLA_WIDE, MLA_WIDE, MLA_WIDE)
    return pl.pallas_call(
        functools.partial(_in_ab_kernel, q_scale=(MLA_NOPE_DIM + MLA_ROPE_DIM) ** -0.5 * LOG2E),
        grid=(m // ROW_TILE,),
        in_specs=[pl.BlockSpec((ROW_TILE, d), row),
                  pl.BlockSpec((1, d), const),
                  pl.BlockSpec(w.shape, const),
                  pl.BlockSpec((1, MLA_Q_RANK), const),
                  pl.BlockSpec((1, MLA_KV_RANK), const),
                  pl.BlockSpec(wuq.shape, const),
                  pl.BlockSpec(wuk.shape, const),
                  pl.BlockSpec(wuv.shape, const)] + [pl.BlockSpec((ROW_TILE, LANES), pos)] * 3,
        out_specs=[pl.BlockSpec((ROW_TILE, n), row) for n in out_w],
        out_shape=[jax.ShapeDtypeStruct((m, n), BF16) for n in out_w],
        compiler_params=_params(("parallel",)),
        name="in_proj_ab",
    )(x, g.reshape(1, d), w, gq.reshape(1, -1), gkv.reshape(1, -1), wuq, wuk, wuv, *tabs)


def _in_c_body(x, g_ref, w_ref, gq_ref, gk_ref, bd_ref, c_ref, sa_ref, sb_ref,
               gate_ref, q_ref, kk_ref, vv_ref, q_scale):
    h = _rms(x, g_ref[...]).astype(BF16)
    c, sa, sb = c_ref[...], sa_ref[...], sb_ref[...]
    bd = bd_ref[...]

    def proj(c0):
        return lambda: jnp.dot(h, w_ref[:, c0:c0 + 512], preferred_element_type=F32)

    def head_norm_rope(x, gain, scale):
        sq = x * x
        hi = sq.astype(BF16)
        lo = (sq - hi.astype(F32)).astype(BF16)
        ss = (jnp.dot(hi, bd, preferred_element_type=F32) +
              jnp.dot(lo, bd, preferred_element_type=F32))
        xn = x * lax.rsqrt(ss * (1.0 / C_HEAD_DIM) + EPS) * gain
        halves = [_rope_lanes(xn[:, s:s + LANES], c, sa, sb) for s in (0, LANES)]
        return (jnp.concatenate(halves, axis=1) * scale).astype(BF16)

    def store(ref, c0):
        def consume(v):
            ref[:, c0:c0 + 512] = v.astype(BF16)
        return consume

    def store_heads(ref, c0, gain_ref, scale):
        def consume(v):
            for s in (0, 256):
                ref[:, c0 + s:c0 + s + 256] = head_norm_rope(v[:, s:s + 256], gain_ref[...], scale)
        return consume

    def store_values(v):
        vv_ref[...] = (v + _ones_lane_row(v.shape[1], C_HEAD_DIM)).astype(BF16)

    _pipelined([
        (proj(COL_QC), store_heads(q_ref, 0, gq_ref, q_scale)),
        (proj(COL_GATE_C), store(gate_ref, 0)),
        (proj(COL_QC + 512), store_heads(q_ref, 512, gq_ref, q_scale)),
        (proj(COL_GATE_C + 512), store(gate_ref, 512)),
        (proj(COL_KC), store_heads(kk_ref, 0, gk_ref, 1.0)),
        (proj(COL_VC), store_values),
    ])


def _in_c_kernel(x_ref, *refs, q_scale):
    _in_c_body(x_ref[...], *refs, q_scale)


def _in_c_specs(d, w, tiles_per_seq):
    row = lambda i: (i, 0)
    const = lambda i: (0, 0)
    pos = lambda i: (i % tiles_per_seq, 0)
    out_w = (C_WIDTH, C_WIDTH, C_KV_HEADS * LANES, C_KV_HEADS * LANES)
    in_specs = [pl.BlockSpec((1, d), const),
                pl.BlockSpec(w.shape, const),
                pl.BlockSpec((1, 256), const),
                pl.BlockSpec((1, 256), const),
                pl.BlockSpec((256, 256), const)] + [pl.BlockSpec((ROW_TILE, LANES), pos)] * 3
    out_specs = [pl.BlockSpec((ROW_TILE, n), row) for n in out_w]
    return in_specs, out_specs, out_w


def in_proj_c(x, seq, g, w, gq, gk, bd, tabs):
    m, d = x.shape
    in_specs, out_specs, out_w = _in_c_specs(d, w, seq // ROW_TILE)
    tile4 = lambda v: jnp.tile(v.reshape(1, -1), (1, 4))
    return pl.pallas_call(
        functools.partial(_in_c_kernel, q_scale=C_HEAD_DIM ** -0.5 * LOG2E),
        grid=(m // ROW_TILE,),
        in_specs=[pl.BlockSpec((ROW_TILE, d), lambda i: (i, 0))] + in_specs,
        out_specs=out_specs,
        out_shape=[jax.ShapeDtypeStruct((m, n), BF16) for n in out_w],
        compiler_params=_params(("parallel",)),
        name="in_proj_c",
    )(x, g.reshape(1, d), w, tile4(gq), tile4(gk), bd, *tabs)


def _online_softmax(q, k_ref, v_ref, m_ref, acc_ref, *, tk, unroll, v_groups):
    m_ref[...] = jnp.full(m_ref.shape, NEG_INF, F32)
    acc_ref[...] = jnp.zeros(acc_ref.shape, F32)

    def body(j, carry):
        start = pl.multiple_of(j * tk, tk)
        k = k_ref[0, pl.ds(start, tk), :]
        s = lax.dot_general(q, k, (((1,), (1,)), ((), ())), preferred_element_type=F32)
        m_old = m_ref[...]
        m_new = jnp.maximum(m_old, jnp.max(s, axis=-1, keepdims=True))
        alpha = jnp.exp2(m_old - m_new)
        p = jnp.exp2(s - jnp.tile(m_new, (1, tk // LANES))).astype(BF16)
        for r0, r1, l0 in v_groups:
            v = v_ref[0, pl.ds(start, tk), l0:l0 + LANES]
            acc_ref[r0:r1] = alpha[r0:r1] * acc_ref[r0:r1] + jnp.dot(
                p[r0:r1], v, preferred_element_type=F32)
        m_ref[...] = m_new
        return carry

    lax.fori_loop(0, k_ref.shape[1] // tk, body, 0, unroll=unroll)


def _normalised(acc, dv):
    return acc[:, :dv] / acc[:, dv:dv + 1]


def _gqa_kernel(q_ref, k_ref, v_ref, o_ref, m_ref, acc_ref, *, tk, unroll):
    tq = q_ref.shape[1]
    q = q_ref[0].astype(F32)
    low = lax.broadcasted_iota(jnp.int32, (tq, LANES), 1) < C_HEAD_DIM
    parts = []
    for half in (q[:, :LANES], q[:, LANES:]):
        parts += [jnp.where(low, half, 0.0), jnp.where(low, 0.0, half)]
    qs = jnp.concatenate(parts, axis=0).astype(BF16)
    _online_softmax(qs, k_ref, v_ref, m_ref, acc_ref, tk=tk, unroll=unroll,
                    v_groups=((0, C_GROUPS * tq, 0),))
    acc = acc_ref[...]
    o_ref[0] = jnp.concatenate(
        [_normalised(acc[g * tq:(g + 1) * tq], C_HEAD_DIM) for g in range(C_GROUPS)],
        axis=1).astype(o_ref.dtype)


def gqa_attention(q, kk, vv, *, tq, tk, unroll):
    b, t, _ = q.shape
    rows = C_GROUPS * tq
    return pl.pallas_call(
        functools.partial(_gqa_kernel, tk=tk, unroll=unroll),
        grid=(b, C_KV_HEADS, t // tq),
        in_specs=[pl.BlockSpec((1, tq, 256), lambda i, j, n: (i, n, j)),
                  pl.BlockSpec((1, t, LANES), lambda i, j, n: (i, 0, j)),
                  pl.BlockSpec((1, t, LANES), lambda i, j, n: (i, 0, j))],
        out_specs=pl.BlockSpec((1, tq, 256), lambda i, j, n: (i, n, j)),
        out_shape=jax.ShapeDtypeStruct((b, t, C_WIDTH), BF16),
        scratch_shapes=[pltpu.VMEM((rows, LANES), F32), pltpu.VMEM((rows, LANES), F32)],
        compiler_params=_params(("parallel", "parallel", "arbitrary")),
        name="gqa_attention",
    )(q, kk, vv)


def _mla_kernel(q_ref, k_ref, v_ref, o_ref, m_ref, acc_ref, *, tk, unroll):
    tq = q_ref.shape[1]
    q = q_ref[0]
    zeros = jnp.zeros((tq, LANES), q.dtype)
    qs = jnp.concatenate([jnp.concatenate([q[:, :LANES], zeros], axis=1),
                          jnp.concatenate([zeros, q[:, LANES:]], axis=1)], axis=0)
    _online_softmax(qs, k_ref, v_ref, m_ref, acc_ref, tk=tk, unroll=unroll,
                    v_groups=((0, tq, 0), (tq, 2 * tq, LANES)))
    acc = acc_ref[...]
    o_ref[0] = jnp.concatenate(
        [_normalised(acc[g * tq:(g + 1) * tq], MLA_V_DIM) for g in range(2)],
        axis=1).astype(o_ref.dtype)


def mla_attention(qb, kb, vb, *, tq, tk, unroll):
    b, t, _ = qb.shape
    rows = 2 * tq
    return pl.pallas_call(
        functools.partial(_mla_kernel, tk=tk, unroll=unroll),
        grid=(b, MLA_HEADS // 2, t // tq),
        in_specs=[pl.BlockSpec((1, tq, 256), lambda i, j, n: (i, n, j)),
                  pl.BlockSpec((1, t, 256), lambda i, j, n: (i, 0, j)),
                  pl.BlockSpec((1, t, 256), lambda i, j, n: (i, 0, j))],
        out_specs=pl.BlockSpec((1, tq, LANES), lambda i, j, n: (i, n, j)),
        out_shape=jax.ShapeDtypeStruct((b, t, B_WIDTH), BF16),
        scratch_shapes=[pltpu.VMEM((rows, LANES), F32), pltpu.VMEM((rows, LANES), F32)],
        compiler_params=_params(("parallel", "parallel", "arbitrary")),
        name="mla_attention",
    )(qb, kb, vb)


A_TQ = 2048
A_SUB = 128
A_BAND = A_SUB + 2 * A_RADIUS
A_CLASSES = 16
A_LOCKSTEP = 8
A_PAD = A_RADIUS


def _class_runs(cfg, sub):
    if cfg == 2:
        return [(pl.multiple_of(sub * A_SUB, A_SUB), A_SUB)]
    if cfg == 1:
        r4, blk = sub & 3, sub >> 2
        return [(pl.multiple_of((r4 + 4 * a) * A_SUB + blk * 32, 32), 32) for a in range(4)]
    return [(pl.multiple_of(r * A_SUB + sub * 8, 8), 8) for r in range(A_CLASSES)]


def _dilated_kernel(q_ref, k_ref, v_ref, b_ref, o_ref,
                    nat_ref, mid_ref, qc_ref, kp1, kp16, vp1, vp16, oacc_ref, m_ref, l_ref):
    seq = k_ref.shape[1]
    units16 = seq // A_CLASSES
    qi = pl.program_id(2)

    def split_classes(n_rows, emit):
        quarter = n_rows // 4
        for r4 in range(4):
            mid_ref[r4 * quarter:(r4 + 1) * quarter] = nat_ref[pl.ds(r4, quarter, stride=4), :]
        for r4 in range(4):
            for a in range(4):
                emit(r4 + 4 * a, mid_ref[pl.ds(r4 * quarter + a, quarter // 4, stride=4), :])

    @pl.when(qi == 0)
    def _():
        zeros = jnp.zeros((A_PAD, LANES), BF16)
        for src, nat_pad, cls_pad in ((k_ref, kp1, kp16), (v_ref, vp1, vp16)):
            nat_pad[0:A_PAD] = zeros
            nat_pad[A_PAD + seq:] = zeros
            nat_pad[A_PAD:A_PAD + seq] = src[0]
            nat_ref[...] = src[0].astype(F32)

            def emit(r, rows, cls_pad=cls_pad):
                cls_pad[r, 0:A_PAD] = zeros
                cls_pad[r, A_PAD + units16:] = zeros
                cls_pad[r, A_PAD:A_PAD + units16] = rows.astype(BF16)

            split_classes(seq, emit)

    nat_ref[0:A_TQ] = q_ref[0].astype(F32)

    def emit_q(r, rows):
        qc_ref[r * A_SUB:(r + 1) * A_SUB] = rows

    split_classes(A_TQ, emit_q)
    low = lax.broadcasted_iota(jnp.int32, (A_SUB, LANES), 1) < A_HEAD_DIM
    col = lax.broadcasted_iota(jnp.int32, (1, A_BAND), 1)
    col_unit = (col, 4 * (col % 64) + col // 64, col)

    for ci, (_, dil) in enumerate(DILATED_CONFIGS):
        units = seq // dil
        per_step = A_TQ // dil

        def body(it, carry, ci=ci, units=units, per_step=per_step):
            subs = [it * A_LOCKSTEP + u for u in range(A_LOCKSTEP)]
            if ci == 2:
                first = [qi * per_step] * len(subs)
            elif ci == 1:
                first = [qi * per_step + (s_ >> 2) * A_SUB for s_ in subs]
            else:
                first = [qi * per_step + s_ * A_SUB for s_ in subs]
            runs = [_class_runs(ci, s_) for s_ in subs]

            def window(nat_pad, cls_pad, sub, f):
                if ci == 0:
                    return nat_pad[pl.ds(pl.multiple_of(f, A_SUB), A_BAND), :]
                if ci == 2:
                    return cls_pad[sub, pl.ds(pl.multiple_of(f, A_SUB), A_BAND), :]
                row = pl.multiple_of((f >> 2) + A_PAD - A_RADIUS // 4, 16)
                return jnp.concatenate([cls_pad[(sub & 3) + 4 * a, pl.ds(row, A_BAND // 4), :]
                                        for a in range(4)], axis=0)

            def gather(ref, sub_runs):
                parts = [ref[pl.ds(start, size), :] for start, size in sub_runs]
                return parts[0] if len(parts) == 1 else jnp.concatenate(parts, axis=0)

            def scatter(ref, sub_runs, val):
                row = 0
                for start, size in sub_runs:
                    ref[pl.ds(start, size), :] = val[row:row + size]
                    row += size

            qs = []
            for sub_runs in runs:
                q = gather(qc_ref, sub_runs)
                qs.append(jnp.concatenate([jnp.where(low, q, 0.0), jnp.where(low, 0.0, q)],
                                          axis=0).astype(BF16))
            kw = [window(kp1, kp16, s_, f) for s_, f in zip(subs, first)]
            vw = [window(vp1, vp16, s_, f) for s_, f in zip(subs, first)]
            s = [lax.dot_general(a, b_, (((1,), (1,)), ((), ())), preferred_element_type=F32)
                 for a, b_ in zip(qs, kw)]
            bias = b_ref[0, ci]
            masked = []
            for s_u, f in zip(s, first):
                unit = f - A_RADIUS + col_unit[ci]
                valid = jnp.where((unit >= 0) & (unit < units), 0.0, NEG_INF)
                masked.append(s_u + bias + valid)
            m_blk = [jnp.max(x, axis=-1, keepdims=True) for x in masked]
            p = [jnp.exp2(x - m) for x, m in zip(masked, m_blk)]
            l_blk = [jnp.sum(x, axis=-1, keepdims=True) for x in p]
            o_blk = [jnp.dot(x.astype(BF16), v, preferred_element_type=F32) for x, v in zip(p, vw)]
            for sub_runs, o_u, m_u, l_u in zip(runs, o_blk, m_blk, l_blk):
                o2 = jnp.where(low, o_u[:A_SUB], o_u[A_SUB:])
                m2 = jnp.where(low, m_u[:A_SUB], m_u[A_SUB:])
                l2 = jnp.where(low, l_u[:A_SUB], l_u[A_SUB:])
                if ci == 0:
                    scatter(oacc_ref, sub_runs, o2)
                    scatter(m_ref, sub_runs, m2)
                    scatter(l_ref, sub_runs, l2)
                else:
                    m_old = gather(m_ref, sub_runs)
                    m_new = jnp.maximum(m_old, m2)
                    a_old = jnp.exp2(m_old - m_new)
                    a_blk = jnp.exp2(m2 - m_new)
                    scatter(oacc_ref, sub_runs, gather(oacc_ref, sub_runs) * a_old + o2 * a_blk)
                    scatter(l_ref, sub_runs, gather(l_ref, sub_runs) * a_old + l2 * a_blk)
                    scatter(m_ref, sub_runs, m_new)
            return carry

        lax.fori_loop(0, A_TQ // A_SUB // A_LOCKSTEP, body, 0)

    quarter = A_TQ // 4
    for r4 in range(4):
        for a in range(4):
            rows = slice((r4 + 4 * a) * A_SUB, (r4 + 4 * a + 1) * A_SUB)
            mid_ref[pl.ds(r4 * quarter + a, A_SUB, stride=4), :] = oacc_ref[rows] / l_ref[rows]
    for r4 in range(4):
        nat_ref[pl.ds(r4, quarter, stride=4), :] = mid_ref[r4 * quarter:(r4 + 1) * quarter]
    o_ref[0] = nat_ref[0:A_TQ].astype(o_ref.dtype)


def dilated_attention(qkv, bias):
    b, t, _ = qkv.shape
    pairs = A_HEADS // 2
    state = pltpu.VMEM((A_TQ, LANES), F32)
    nat_pad = pltpu.VMEM((t + 2 * A_PAD, LANES), BF16)
    cls_pad = pltpu.VMEM((A_CLASSES, t // A_CLASSES + 2 * A_PAD, LANES), BF16)
    return pl.pallas_call(
        _dilated_kernel,
        grid=(b, pairs, t // A_TQ),
        in_specs=[pl.BlockSpec((1, A_TQ, LANES), lambda i, j, n: (i, n, j)),
                  pl.BlockSpec((1, t, LANES), lambda i, j, n: (i, 0, pairs + j)),
                  pl.BlockSpec((1, t, LANES), lambda i, j, n: (i, 0, 2 * pairs + j)),
                  pl.BlockSpec((1, len(DILATED_CONFIGS), 2 * A_SUB, A_BAND), lambda i, j, n: (j, 0, 0, 0))],
        out_specs=pl.BlockSpec((1, A_TQ, LANES), lambda i, j, n: (i, n, j)),
        out_shape=jax.ShapeDtypeStruct((b, t, A_WIDTH), BF16),
        scratch_shapes=[pltpu.VMEM((t, LANES), F32), pltpu.VMEM((t, LANES), F32), state,
                        nat_pad, cls_pad, nat_pad, cls_pad, state, state, state],
        compiler_params=_params(("parallel", "parallel", "arbitrary")),
        name="dilated_attention",
    )(qkv, qkv, qkv, bias)


def _dilated_bias():
    i = jnp.arange(A_SUB, dtype=jnp.int32)[:, None]
    c = jnp.arange(A_BAND, dtype=jnp.int32)[None, :]
    u_rel = {1: 16 * (i % 8) + i // 8, 4: 4 * (i % 32) + i // 32, 16: i}
    k_rel = {1: c, 4: 4 * (c % 64) + c // 64, 16: c}
    slopes = 2.0 ** (-8.0 * jnp.arange(1, A_HEADS + 1, dtype=F32) / A_HEADS)
    per_cfg = []
    for _, dil in DILATED_CONFIGS:
        hops = jnp.abs(k_rel[dil] - A_RADIUS - u_rel[dil])
        dist = (dil * hops).astype(F32)
        bias = jnp.where(hops <= A_RADIUS, -slopes[:, None, None] * dist[None] * LOG2E, NEG_INF)
        per_cfg.append(bias.reshape(A_HEADS // 2, 2 * A_SUB, A_BAND))
    return jnp.stack(per_cfg, axis=1)


def _gate_out_value(parts, gate_ref, x_ref, w_ref, g_ref):
    o = jnp.concatenate([r[...] for r in parts], axis=-1) if len(parts) > 1 else parts[0][...]
    gate = gate_ref[...].astype(F32)
    y = (o.astype(F32) * (gate * jax.nn.sigmoid(gate))).astype(BF16)
    y = jnp.dot(y, w_ref[...], preferred_element_type=F32)
    return x_ref[...] + _rms(y, g_ref[...])


def _gate_out_kernel(*refs, n_parts):
    gate_ref, x_ref, w_ref, g_ref, o_ref = refs[n_parts:]
    o_ref[...] = _gate_out_value(refs[:n_parts], gate_ref, x_ref, w_ref, g_ref)


def _gate_out_in_c_kernel(*refs, n_parts, q_scale):
    gate_ref, x_ref, w_ref, g_ref = refs[n_parts:n_parts + 4]
    in_c_inputs = refs[n_parts + 4:n_parts + 12]
    x_out_ref = refs[n_parts + 12]
    in_c_outputs = refs[n_parts + 13:]
    x_new = _gate_out_value(refs[:n_parts], gate_ref, x_ref, w_ref, g_ref)
    x_out_ref[...] = x_new
    _in_c_body(x_new, *in_c_inputs, *in_c_outputs, q_scale)


def _gate_out_specs(parts, gate, d, w):
    row = lambda i: (i, 0)
    const = lambda i: (0, 0)
    return ([pl.BlockSpec((ROW_TILE, p.shape[1]), row) for p in parts] +
            [pl.BlockSpec((ROW_TILE, gate.shape[1]), row),
             pl.BlockSpec((ROW_TILE, d), row),
             pl.BlockSpec(w.shape, const),
             pl.BlockSpec((1, d), const)])


def gate_out(parts, gate, x, w, g):
    m, d = x.shape
    return pl.pallas_call(
        functools.partial(_gate_out_kernel, n_parts=len(parts)),
        grid=(m // ROW_TILE,),
        in_specs=_gate_out_specs(parts, gate, d, w),
        out_specs=pl.BlockSpec((ROW_TILE, d), lambda i: (i, 0)),
        out_shape=jax.ShapeDtypeStruct((m, d), F32),
        compiler_params=_params(("parallel",)),
        name="gate_out",
    )(*parts, gate, x, w, g.reshape(1, d))


def gate_out_in_proj_c(parts, gate, x, w_out, g_post, seq, g_pre, w_in, gq, gk, bd, tabs):
    m, d = x.shape
    in_specs, out_specs, out_w = _in_c_specs(d, w_in, seq // ROW_TILE)
    tile4 = lambda v: jnp.tile(v.reshape(1, -1), (1, 4))
    return pl.pallas_call(
        functools.partial(_gate_out_in_c_kernel, n_parts=len(parts), q_scale=C_HEAD_DIM ** -0.5 * LOG2E),
        grid=(m // ROW_TILE,),
        in_specs=_gate_out_specs(parts, gate, d, w_out) + in_specs,
        out_specs=[pl.BlockSpec((ROW_TILE, d), lambda i: (i, 0))] + out_specs,
        out_shape=[jax.ShapeDtypeStruct((m, d), F32)] + [jax.ShapeDtypeStruct((m, n), BF16) for n in out_w],
        compiler_params=_params(("parallel",)),
        name="gate_out_in_proj_c",
    )(*parts, gate, x, w_out, g_post.reshape(1, d), g_pre.reshape(1, d), w_in, tile4(gq), tile4(gk), bd, *tabs)


def _rope_freqs():
    half = MLA_ROPE_DIM // 2
    return ROPE_THETA ** (-jnp.arange(half, dtype=F32) / half)


def _mla_tables(seq):
    ang = jnp.arange(seq, dtype=F32)[:, None] * _rope_freqs()[None, :]
    cos, sin = jnp.cos(ang), jnp.sin(ang)
    zeros = jnp.zeros_like(cos)
    ones_nope = jnp.ones((seq, MLA_NOPE_DIM), F32)
    zeros_nope = jnp.zeros((seq, MLA_NOPE_DIM), F32)
    tail = jnp.zeros((seq, LANES - MLA_NOPE_DIM - MLA_ROPE_DIM), F32)
    c = jnp.concatenate([ones_nope, cos, cos, tail], axis=1)
    sa = jnp.concatenate([zeros_nope, zeros, sin, tail], axis=1)
    sb = jnp.concatenate([zeros_nope, -sin, zeros, tail], axis=1)
    return c, sa, sb


def _axial_tables(seq):
    t = jnp.arange(seq, dtype=jnp.int32)
    f = _rope_freqs()[None, :]
    ar = (t // GRID_W).astype(F32)[:, None] * f
    ac = (t % GRID_W).astype(F32)[:, None] * f
    zeros = jnp.zeros_like(ar)
    c = jnp.concatenate([jnp.cos(ar), jnp.cos(ar), jnp.cos(ac), jnp.cos(ac)], axis=1)
    sa = jnp.concatenate([zeros, jnp.sin(ar), zeros, jnp.sin(ac)], axis=1)
    sb = jnp.concatenate([-jnp.sin(ar), zeros, -jnp.sin(ac), zeros], axis=1)
    return tuple(jnp.tile(x, (1, 2)) for x in (c, sa, sb))


def _head_block_diag():
    idx = np.arange(256) // C_HEAD_DIM
    return jnp.asarray(idx[:, None] == idx[None, :], BF16)


def _arrange_w_in_ab(w):
    qa, ka, va, cq, ckv, kr, gate = jnp.split(
        w, list(np.cumsum([A_WIDTH, A_WIDTH, A_WIDTH, MLA_Q_RANK, MLA_KV_RANK, MLA_ROPE_DIM])), axis=1)
    d = w.shape[0]
    kr_blk = jnp.concatenate([jnp.zeros((d, KR_LANE), F32), kr,
                              jnp.zeros((d, LANES - KR_LANE - MLA_ROPE_DIM), F32)], axis=1)
    qa = qa * (A_HEAD_DIM ** -0.5 * LOG2E)
    return jnp.concatenate([qa, ka, va, cq, ckv, kr_blk, gate], axis=1).astype(BF16)


def _arrange_w_in_c(w):
    d = w.shape[0]
    q, k, v, gate = jnp.split(w, list(np.cumsum([C_WIDTH, C_KV_HEADS * C_HEAD_DIM,
                                                 C_KV_HEADS * C_HEAD_DIM])), axis=1)
    k = k.reshape(d, C_KV_HEADS, 1, C_HEAD_DIM)
    kk = jnp.broadcast_to(k, (d, C_KV_HEADS, 2, C_HEAD_DIM)).reshape(d, C_KV_HEADS * LANES)
    v = v.reshape(d, C_KV_HEADS, C_HEAD_DIM)
    vv = jnp.concatenate([v, jnp.zeros_like(v)], axis=2).reshape(d, C_KV_HEADS * LANES)
    return jnp.concatenate([gate, q, kk, vv], axis=1).astype(BF16)


def _arrange_w_uq(w):
    r = w.shape[0]
    w = w.reshape(r, MLA_HEADS, MLA_NOPE_DIM + MLA_ROPE_DIM)
    pad = jnp.zeros((r, MLA_HEADS, LANES - MLA_NOPE_DIM - MLA_ROPE_DIM), F32)
    return jnp.concatenate([w, pad], axis=2).reshape(r, MLA_WIDE).astype(BF16)


def _arrange_w_ukv(w):
    r = w.shape[0]
    w = w.reshape(r, MLA_HEADS, MLA_NOPE_DIM + MLA_V_DIM)
    pad = jnp.zeros((r, MLA_HEADS, LANES - MLA_NOPE_DIM), F32)
    wk = jnp.concatenate([w[:, :, :MLA_NOPE_DIM], pad], axis=2).reshape(r, MLA_WIDE)
    wv = jnp.concatenate([w[:, :, MLA_NOPE_DIM:], pad], axis=2).reshape(r, MLA_WIDE)
    return wk.astype(BF16), wv.astype(BF16)


def _even_mixers(x, b, t, g_pre, w_in, gq, wuq, gkv, wuk, wuv, bias, tabs):
    qkv, gate, qb, kb, vb = in_proj_ab(x, t, g_pre, w_in, gq, gkv, wuq, wuk, wuv, tabs)
    three = lambda a: a.reshape(b, t, a.shape[1])
    out_a = dilated_attention(three(qkv), bias)
    out_b = mla_attention(three(qb), three(kb), three(vb), tq=1024, tk=1024, unroll=4)
    return [out_a.reshape(b * t, A_WIDTH), out_b.reshape(b * t, B_WIDTH)], gate


def _odd_mixer(b, t, q, kk, vv):
    three = lambda a: a.reshape(b, t, a.shape[1])
    o = gqa_attention(three(q), three(kk), three(vv), tq=512, tk=1024, unroll=4)
    return [o.reshape(b * t, C_WIDTH)]


def _trunk(x3, weights, consts):
    (norm_pre, norm_post, w_in_ab, mla_q_norm, w_uq, mla_kv_norm, w_uk, w_uv, w_out_ab,
     w_in_c, c_q_norm, c_k_norm, w_out_c) = weights
    b, t, d = x3.shape
    x = x3.reshape(b * t, d)
    bias, bd = consts
    mla_tabs = _mla_tables(t)
    axial_tabs = _axial_tables(t)
    depth = norm_pre.shape[0]
    open_layer = None
    for layer in range(depth):
        i = layer // 2
        if layer % 2 == 0:
            if open_layer is not None:
                x = gate_out(open_layer[0], open_layer[1], x, open_layer[2], open_layer[3])
            parts, gate = _even_mixers(x, b, t, norm_pre[layer], w_in_ab[i], mla_q_norm[i], w_uq[i],
                                       mla_kv_norm[i], w_uk[i], w_uv[i], bias, mla_tabs)
            open_layer = (parts, gate, w_out_ab[i], norm_post[layer])
        else:
            odd_in = (t, norm_pre[layer], w_in_c[i], c_q_norm[i], c_k_norm[i], bd, axial_tabs)
            if open_layer is not None:
                x, gate, q, kk, vv = gate_out_in_proj_c(open_layer[0], open_layer[1], x, open_layer[2],
                                                        open_layer[3], *odd_in)
            else:
                gate, q, kk, vv = in_proj_c(x, *odd_in)
            open_layer = (_odd_mixer(b, t, q, kk, vv), gate, w_out_c[i], norm_post[layer])
    x = gate_out(open_layer[0], open_layer[1], x, open_layer[2], open_layer[3])
    return x.reshape(b, t, d)


def kernel(x_prompt, x_sample, norm_pre, norm_post, w_in_ab, mla_q_norm, w_uq, mla_kv_norm, w_ukv,
           w_out_ab, w_in_c, c_q_norm, c_k_norm, w_out_c):
    n_even, n_odd = w_in_ab.shape[0], w_in_c.shape[0]
    w_in_ab_p = jnp.stack([_arrange_w_in_ab(w_in_ab[i]) for i in range(n_even)])
    w_uq_p = jnp.stack([_arrange_w_uq(w_uq[i]) for i in range(n_even)])
    w_ukv_p = [_arrange_w_ukv(w_ukv[i]) for i in range(n_even)]
    w_uk_p = jnp.stack([p[0] for p in w_ukv_p])
    w_uv_p = jnp.stack([p[1] for p in w_ukv_p])
    w_in_c_p = jnp.stack([_arrange_w_in_c(w_in_c[i]) for i in range(n_odd)])
    weights = (norm_pre, norm_post, w_in_ab_p, mla_q_norm, w_uq_p, mla_kv_norm, w_uk_p, w_uv_p,
               w_out_ab.astype(BF16), w_in_c_p, c_q_norm, c_k_norm, w_out_c.astype(BF16))
    consts = (_dilated_bias(), _head_block_diag())
    return (_trunk(x_prompt, weights, consts), _trunk(x_sample, weights, consts))
```

```python
import functools
import math

import numpy as np
import jax
import jax.numpy as jnp
from jax import lax
from jax.experimental import pallas as pl
from jax.experimental.pallas import tpu as pltpu

F32 = jnp.float32
BF16 = jnp.bfloat16

D_MODEL = 1024
EPS = 1e-6
ROPE_THETA = 10000.0
NEG_INF = -1e30
LOG2E = math.log2(math.e)
GRID_W = 64
LANES = 128
A_HEADS = 8
A_HEAD_DIM = 64
DILATED_CONFIGS = ((128, 1), (512, 4), (2048, 16))
A_RADIUS = 64
A_REACH = max(w // 2 for w, _ in DILATED_CONFIGS)
MLA_HEADS = 8
MLA_Q_RANK = 256
MLA_KV_RANK = 128
MLA_NOPE_DIM = 64
MLA_ROPE_DIM = 32
MLA_V_DIM = 64
C_HEADS = 16
C_KV_HEADS = 4
C_GROUPS = C_HEADS // C_KV_HEADS
C_HEAD_DIM = 64
A_WIDTH = A_HEADS * A_HEAD_DIM
B_WIDTH = MLA_HEADS * MLA_V_DIM
AB_WIDTH = A_WIDTH + B_WIDTH
C_WIDTH = C_HEADS * C_HEAD_DIM
MLA_WIDE = MLA_HEADS * LANES
COL_QKV_A, COL_LATENT, COL_GATE_AB = 0, 1536, 2048
IN_AB_PAD = 3072
KR_LANE = MLA_NOPE_DIM
COL_GATE_C, COL_QC, COL_KC, COL_VC = 0, 1024, 2048, 2560
IN_C_PAD = 3072

VMEM_LIMIT = 56 * 1024 * 1024
ROW_TILE = 512


def _params(sem):
    return pltpu.CompilerParams(dimension_semantics=sem, vmem_limit_bytes=VMEM_LIMIT)


def _rms(x, g):
    ms = jnp.mean(x * x, axis=-1, keepdims=True)
    return x * lax.rsqrt(ms + EPS) * g


def _rope_lanes(x, c, sa, sb):
    return x * c + pltpu.roll(x, 16, 1) * sa + pltpu.roll(x, 112, 1) * sb


def _ones_lane_row(width, lane):
    idx = lax.broadcasted_iota(jnp.int32, (1, width), 1)
    return jnp.where(idx % LANES == lane, 1.0, 0.0).astype(F32)


def _pipelined(jobs):
    nxt = jobs[0][0]()
    for i, (_, consume) in enumerate(jobs):
        cur = nxt
        nxt = jobs[i + 1][0]() if i + 1 < len(jobs) else None
        consume(cur)


def _in_ab_kernel(x_ref, g_ref, w_ref, gq_ref, gkv_ref, wuq_ref, wuk_ref, wuv_ref,
                  c_ref, sa_ref, sb_ref, qkv_ref, gate_ref, qb_ref, kb_ref, vb_ref, *, q_scale):
    h = _rms(x_ref[...], g_ref[...]).astype(BF16)
    c, sa, sb = c_ref[...], sa_ref[...], sb_ref[...]

    def proj(c0):
        return lambda: jnp.dot(h, w_ref[:, c0:c0 + 512], preferred_element_type=F32)

    def store(ref, c0):
        def consume(v):
            ref[:, c0:c0 + 512] = v.astype(BF16)
        return consume

    latent = {}

    def up_project(lat):
        cq = _rms(lat[:, :MLA_Q_RANK], gq_ref[...]).astype(BF16)
        ckv = _rms(lat[:, MLA_Q_RANK:MLA_Q_RANK + MLA_KV_RANK], gkv_ref[...]).astype(BF16)
        latent["kr"] = _rope_lanes(lat[:, MLA_Q_RANK + MLA_KV_RANK:], c, sa, sb)
        latent["qf"] = jnp.dot(cq, wuq_ref[...], preferred_element_type=F32)
        latent["kf"] = jnp.dot(ckv, wuk_ref[...], preferred_element_type=F32)
        vf = jnp.dot(ckv, wuv_ref[...], preferred_element_type=F32)
        vb_ref[...] = (vf + _ones_lane_row(MLA_WIDE, MLA_V_DIM)).astype(BF16)

    def rope_heads(ref, c0, heads):
        store_plain = store(ref, c0)

        def consume(v):
            store_plain(v)
            for hd in heads:
                sl = slice(hd * LANES, (hd + 1) * LANES)
                qb_ref[:, sl] = (_rope_lanes(latent["qf"][:, sl], c, sa, sb) * q_scale).astype(BF16)
                kb_ref[:, sl] = (latent["kf"][:, sl] + latent["kr"]).astype(BF16)
        return consume

    half = MLA_HEADS // 2
    _pipelined([
        (proj(COL_LATENT), up_project),
        (proj(COL_QKV_A), store(qkv_ref, 0)),
        (proj(COL_QKV_A + 512), store(qkv_ref, 512)),
        (proj(COL_QKV_A + 1024), rope_heads(qkv_ref, 1024, range(0, half))),
        (proj(COL_GATE_AB), rope_heads(gate_ref, 0, range(half, MLA_HEADS))),
        (proj(COL_GATE_AB + 512), store(gate_ref, 512)),
    ])


def in_proj_ab(x, seq, g, w, gq, gkv, wuq, wuk, wuv, tabs):
    m, d = x.shape
    tiles_per_seq = seq // ROW_TILE
    row = lambda i: (i, 0)
    const = lambda i: (0, 0)
    pos = lambda i: (i % tiles_per_seq, 0)
    out_w = (COL_LATENT, AB_WIDTH, MLA_WIDE, MLA_WIDE, MLA_WIDE)
    return pl.pallas_call(
        functools.partial(_in_ab_kernel, q_scale=(MLA_NOPE_DIM + MLA_ROPE_DIM) ** -0.5 * LOG2E),
        grid=(m // ROW_TILE,),
        in_specs=[pl.BlockSpec((ROW_TILE, d), row),
                  pl.BlockSpec((1, d), const),
                  pl.BlockSpec(w.shape, const),
                  pl.BlockSpec((1, MLA_Q_RANK), const),
                  pl.BlockSpec((1, MLA_KV_RANK), const),
                  pl.BlockSpec(wuq.shape, const),
                  pl.BlockSpec(wuk.shape, const),
                  pl.BlockSpec(wuv.shape, const)] + [pl.BlockSpec((ROW_TILE, LANES), pos)] * 3,
        out_specs=[pl.BlockSpec((ROW_TILE, n), row) for n in out_w],
        out_shape=[jax.ShapeDtypeStruct((m, n), BF16) for n in out_w],
        compiler_params=_params(("parallel",)),
        name="in_proj_ab",
    )(x, g.reshape(1, d), w, gq.reshape(1, -1), gkv.reshape(1, -1), wuq, wuk, wuv, *tabs)


def _in_c_body(x, g_ref, w_ref, gq_ref, gk_ref, bd_ref, c_ref, sa_ref, sb_ref,
               gate_ref, q_ref, kk_ref, vv_ref, q_scale):
    h = _rms(x, g_ref[...]).astype(BF16)
    c, sa, sb = c_ref[...], sa_ref[...], sb_ref[...]
    bd = bd_ref[...]

    def proj(c0):
        return lambda: jnp.dot(h, w_ref[:, c0:c0 + 512], preferred_element_type=F32)

    def head_norm_rope(x, gain, scale):
        sq = x * x
        hi = sq.astype(BF16)
        lo = (sq - hi.astype(F32)).astype(BF16)
        ss = (jnp.dot(hi, bd, preferred_element_type=F32) +
              jnp.dot(lo, bd, preferred_element_type=F32))
        xn = x * lax.rsqrt(ss * (1.0 / C_HEAD_DIM) + EPS) * gain
        halves = [_rope_lanes(xn[:, s:s + LANES], c, sa, sb) for s in (0, LANES)]
        return (jnp.concatenate(halves, axis=1) * scale).astype(BF16)

    def store(ref, c0):
        def consume(v):
            ref[:, c0:c0 + 512] = v.astype(BF16)
        return consume

    def store_heads(ref, c0, gain_ref, scale):
        def consume(v):
            for s in (0, 256):
                ref[:, c0 + s:c0 + s + 256] = head_norm_rope(v[:, s:s + 256], gain_ref[...], scale)
        return consume

    def store_values(v):
        vv_ref[...] = (v + _ones_lane_row(v.shape[1], C_HEAD_DIM)).astype(BF16)

    _pipelined([
        (proj(COL_QC), store_heads(q_ref, 0, gq_ref, q_scale)),
        (proj(COL_GATE_C), store(gate_ref, 0)),
        (proj(COL_QC + 512), store_heads(q_ref, 512, gq_ref, q_scale)),
        (proj(COL_GATE_C + 512), store(gate_ref, 512)),
        (proj(COL_KC), store_heads(kk_ref, 0, gk_ref, 1.0)),
        (proj(COL_VC), store_values),
    ])


def _in_c_kernel(x_ref, *refs, q_scale):
    _in_c_body(x_ref[...], *refs, q_scale)


def _in_c_specs(d, w, tiles_per_seq):
    row = lambda i: (i, 0)
    const = lambda i: (0, 0)
    pos = lambda i: (i % tiles_per_seq, 0)
    out_w = (C_WIDTH, C_WIDTH, C_KV_HEADS * LANES, C_KV_HEADS * LANES)
    in_specs = [pl.BlockSpec((1, d), const),
                pl.BlockSpec(w.shape, const),
                pl.BlockSpec((1, 256), const),
                pl.BlockSpec((1, 256), const),
                pl.BlockSpec((256, 256), const)] + [pl.BlockSpec((ROW_TILE, LANES), pos)] * 3
    out_specs = [pl.BlockSpec((ROW_TILE, n), row) for n in out_w]
    return in_specs, out_specs, out_w


def in_proj_c(x, seq, g, w, gq, gk, bd, tabs):
    m, d = x.shape
    in_specs, out_specs, out_w = _in_c_specs(d, w, seq // ROW_TILE)
    tile4 = lambda v: jnp.tile(v.reshape(1, -1), (1, 4))
    return pl.pallas_call(
        functools.partial(_in_c_kernel, q_scale=C_HEAD_DIM ** -0.5 * LOG2E),
        grid=(m // ROW_TILE,),
        in_specs=[pl.BlockSpec((ROW_TILE, d), lambda i: (i, 0))] + in_specs,
        out_specs=out_specs,
        out_shape=[jax.ShapeDtypeStruct((m, n), BF16) for n in out_w],
        compiler_params=_params(("parallel",)),
        name="in_proj_c",
    )(x, g.reshape(1, d), w, tile4(gq), tile4(gk), bd, *tabs)


def _online_softmax(q, k_ref, v_ref, m_ref, acc_ref, *, tk, unroll, v_groups):
    def scores(start):
        k = k_ref[0, pl.ds(start, tk), :]
        return lax.dot_general(q, k, (((1,), (1,)), ((), ())), preferred_element_type=F32)

    def products(p, start):
        return [(r0, r1, jnp.dot(p[r0:r1], v_ref[0, pl.ds(start, tk), l0:l0 + LANES],
                                 preferred_element_type=F32)) for r0, r1, l0 in v_groups]

    s = scores(0)
    m_new = jnp.broadcast_to(jnp.max(s, axis=-1, keepdims=True), m_ref.shape)
    p = jnp.exp2(s - jnp.tile(m_new, (1, tk // LANES))).astype(BF16)
    for r0, r1, pv in products(p, 0):
        acc_ref[r0:r1] = pv
    m_ref[...] = m_new

    def body(j, carry):
        start = pl.multiple_of(j * tk, tk)
        s = scores(start)
        m_old = m_ref[...]
        m_new = jnp.maximum(m_old, jnp.max(s, axis=-1, keepdims=True))
        alpha = jnp.exp2(m_old - m_new)
        p = jnp.exp2(s - jnp.tile(m_new, (1, tk // LANES))).astype(BF16)
        for r0, r1, pv in products(p, start):
            acc_ref[r0:r1] = alpha[r0:r1] * acc_ref[r0:r1] + pv
        m_ref[...] = m_new
        return carry

    lax.fori_loop(1, k_ref.shape[1] // tk, body, 0, unroll=unroll)


def _normalised(acc, dv):
    return acc[:, :dv] / acc[:, dv:dv + 1]


def _gqa_kernel(q_ref, k_ref, v_ref, o_ref, m_ref, acc_ref, *, tk, unroll):
    tq = q_ref.shape[1]
    q = q_ref[0].astype(F32)
    low = lax.broadcasted_iota(jnp.int32, (tq, LANES), 1) < C_HEAD_DIM
    parts = []
    for half in (q[:, :LANES], q[:, LANES:]):
        parts += [jnp.where(low, half, 0.0), jnp.where(low, 0.0, half)]
    qs = jnp.concatenate(parts, axis=0).astype(BF16)
    _online_softmax(qs, k_ref, v_ref, m_ref, acc_ref, tk=tk, unroll=unroll,
                    v_groups=((0, C_GROUPS * tq, 0),))
    acc = acc_ref[...]
    o_ref[0] = jnp.concatenate(
        [_normalised(acc[g * tq:(g + 1) * tq], C_HEAD_DIM) for g in range(C_GROUPS)],
        axis=1).astype(o_ref.dtype)


def gqa_attention(q, kk, vv, *, tq, tk, unroll):
    b, t, _ = q.shape
    rows = C_GROUPS * tq
    return pl.pallas_call(
        functools.partial(_gqa_kernel, tk=tk, unroll=unroll),
        grid=(b, C_KV_HEADS, t // tq),
        in_specs=[pl.BlockSpec((1, tq, 256), lambda i, j, n: (i, n, j)),
                  pl.BlockSpec((1, t, LANES), lambda i, j, n: (i, 0, j)),
                  pl.BlockSpec((1, t, LANES), lambda i, j, n: (i, 0, j))],
        out_specs=pl.BlockSpec((1, tq, 256), lambda i, j, n: (i, n, j)),
        out_shape=jax.ShapeDtypeStruct((b, t, C_WIDTH), BF16),
        scratch_shapes=[pltpu.VMEM((rows, LANES), F32), pltpu.VMEM((rows, LANES), F32)],
        compiler_params=_params(("parallel", "parallel", "arbitrary")),
        name="gqa_attention",
    )(q, kk, vv)


def _mla_kernel(q_ref, k_ref, v_ref, o_ref, m_ref, acc_ref, *, tk, unroll):
    tq = q_ref.shape[1]
    q = q_ref[0]
    zeros = jnp.zeros((tq, LANES), q.dtype)
    qs = jnp.concatenate([jnp.concatenate([q[:, :LANES], zeros], axis=1),
                          jnp.concatenate([zeros, q[:, LANES:]], axis=1)], axis=0)
    _online_softmax(qs, k_ref, v_ref, m_ref, acc_ref, tk=tk, unroll=unroll,
                    v_groups=((0, tq, 0), (tq, 2 * tq, LANES)))
    acc = acc_ref[...]
    o_ref[0] = jnp.concatenate(
        [_normalised(acc[g * tq:(g + 1) * tq], MLA_V_DIM) for g in range(2)],
        axis=1).astype(o_ref.dtype)


def mla_attention(qb, kb, vb, *, tq, tk, unroll):
    b, t, _ = qb.shape
    rows = 2 * tq
    return pl.pallas_call(
        functools.partial(_mla_kernel, tk=tk, unroll=unroll),
        grid=(b, MLA_HEADS // 2, t // tq),
        in_specs=[pl.BlockSpec((1, tq, 256), lambda i, j, n: (i, n, j)),
                  pl.BlockSpec((1, t, 256), lambda i, j, n: (i, 0, j)),
                  pl.BlockSpec((1, t, 256), lambda i, j, n: (i, 0, j))],
        out_specs=pl.BlockSpec((1, tq, LANES), lambda i, j, n: (i, n, j)),
        out_shape=jax.ShapeDtypeStruct((b, t, B_WIDTH), BF16),
        scratch_shapes=[pltpu.VMEM((rows, LANES), F32), pltpu.VMEM((rows, LANES), F32)],
        compiler_params=_params(("parallel", "parallel", "arbitrary")),
        name="mla_attention",
    )(qb, kb, vb)


A_TQ = 2048
A_SUB = 128
A_BAND = A_SUB + 2 * A_RADIUS
A_CLASSES = 16
A_LOCKSTEP = 8
A_PAD = A_RADIUS


def _class_runs(cfg, sub):
    if cfg == 2:
        return [(pl.multiple_of(sub * A_SUB, A_SUB), A_SUB)]
    if cfg == 1:
        r4, blk = sub & 3, sub >> 2
        return [(pl.multiple_of((r4 + 4 * a) * A_SUB + blk * 32, 32), 32) for a in range(4)]
    return [(pl.multiple_of(r * A_SUB + sub * 8, 8), 8) for r in range(A_CLASSES)]


def _dilated_kernel(q_ref, k_ref, v_ref, b_ref, o_ref,
                    nat_ref, mid_ref, qc_ref, kp1, kp16, vp1, vp16, oacc_ref, m_ref, l_ref):
    seq = k_ref.shape[1]
    units16 = seq // A_CLASSES
    qi = pl.program_id(2)

    def split_classes(n_rows, emit):
        quarter = n_rows // 4
        for r4 in range(4):
            mid_ref[r4 * quarter:(r4 + 1) * quarter] = nat_ref[pl.ds(r4, quarter, stride=4), :]
        for r4 in range(4):
            for a in range(4):
                emit(r4 + 4 * a, mid_ref[pl.ds(r4 * quarter + a, quarter // 4, stride=4), :])

    @pl.when(qi == 0)
    def _():
        zeros = jnp.zeros((A_PAD, LANES), BF16)
        for src, nat_pad, cls_pad in ((k_ref, kp1, kp16), (v_ref, vp1, vp16)):
            nat_pad[0:A_PAD] = zeros
            nat_pad[A_PAD + seq:] = zeros
            nat_pad[A_PAD:A_PAD + seq] = src[0]
            nat_ref[...] = src[0].astype(F32)

            def emit(r, rows, cls_pad=cls_pad):
                cls_pad[r, 0:A_PAD] = zeros
                cls_pad[r, A_PAD + units16:] = zeros
                cls_pad[r, A_PAD:A_PAD + units16] = rows.astype(BF16)

            split_classes(seq, emit)

    nat_ref[0:A_TQ] = q_ref[0].astype(F32)

    def emit_q(r, rows):
        qc_ref[r * A_SUB:(r + 1) * A_SUB] = rows

    split_classes(A_TQ, emit_q)
    low = lax.broadcasted_iota(jnp.int32, (A_SUB, LANES), 1) < A_HEAD_DIM
    col = lax.broadcasted_iota(jnp.int32, (1, A_BAND), 1)
    col_unit = (col, 4 * (col % 64) + col // 64, col)

    for ci, (_, dil) in enumerate(DILATED_CONFIGS):
        units = seq // dil
        per_step = A_TQ // dil

        def body(it, carry, ci=ci, units=units, per_step=per_step):
            subs = [it * A_LOCKSTEP + u for u in range(A_LOCKSTEP)]
            if ci == 2:
                first = [qi * per_step] * len(subs)
            elif ci == 1:
                first = [qi * per_step + (s_ >> 2) * A_SUB for s_ in subs]
            else:
                first = [qi * per_step + s_ * A_SUB for s_ in subs]
            runs = [_class_runs(ci, s_) for s_ in subs]

            def window(nat_pad, cls_pad, sub, f):
                if ci == 0:
                    return nat_pad[pl.ds(pl.multiple_of(f, A_SUB), A_BAND), :]
                if ci == 2:
                    return cls_pad[sub, pl.ds(pl.multiple_of(f, A_SUB), A_BAND), :]
                row = pl.multiple_of((f >> 2) + A_PAD - A_RADIUS // 4, 16)
                return jnp.concatenate([cls_pad[(sub & 3) + 4 * a, pl.ds(row, A_BAND // 4), :]
                                        for a in range(4)], axis=0)

            def gather(ref, sub_runs):
                parts = [ref[pl.ds(start, size), :] for start, size in sub_runs]
                return parts[0] if len(parts) == 1 else jnp.concatenate(parts, axis=0)

            def scatter(ref, sub_runs, val):
                row = 0
                for start, size in sub_runs:
                    ref[pl.ds(start, size), :] = val[row:row + size]
                    row += size

            qs = []
            for sub_runs in runs:
                q = gather(qc_ref, sub_runs)
                qs.append(jnp.concatenate([jnp.where(low, q, 0.0), jnp.where(low, 0.0, q)],
                                          axis=0).astype(BF16))
            kw = [window(kp1, kp16, s_, f) for s_, f in zip(subs, first)]
            vw = [window(vp1, vp16, s_, f) for s_, f in zip(subs, first)]
            s = [lax.dot_general(a, b_, (((1,), (1,)), ((), ())), preferred_element_type=F32)
                 for a, b_ in zip(qs, kw)]
            bias = b_ref[0, ci]
            masked = []
            for s_u, f in zip(s, first):
                unit = f - A_RADIUS + col_unit[ci]
                valid = jnp.where((unit >= 0) & (unit < units), 0.0, NEG_INF)
                masked.append(s_u + bias + valid)
            m_blk = [jnp.max(x, axis=-1, keepdims=True) for x in masked]
            p = [jnp.exp2(x - m) for x, m in zip(masked, m_blk)]
            l_blk = [jnp.sum(x, axis=-1, keepdims=True) for x in p]
            o_blk = [jnp.dot(x.astype(BF16), v, preferred_element_type=F32) for x, v in zip(p, vw)]
            for sub_runs, o_u, m_u, l_u in zip(runs, o_blk, m_blk, l_blk):
                o2 = jnp.where(low, o_u[:A_SUB], o_u[A_SUB:])
                m2 = jnp.where(low, m_u[:A_SUB], m_u[A_SUB:])
                l2 = jnp.where(low, l_u[:A_SUB], l_u[A_SUB:])
                if ci == 0:
                    scatter(oacc_ref, sub_runs, o2)
                    scatter(m_ref, sub_runs, m2)
                    scatter(l_ref, sub_runs, l2)
                else:
                    m_old = gather(m_ref, sub_runs)
                    m_new = jnp.maximum(m_old, m2)
                    a_old = jnp.exp2(m_old - m_new)
                    a_blk = jnp.exp2(m2 - m_new)
                    scatter(oacc_ref, sub_runs, gather(oacc_ref, sub_runs) * a_old + o2 * a_blk)
                    scatter(l_ref, sub_runs, gather(l_ref, sub_runs) * a_old + l2 * a_blk)
                    scatter(m_ref, sub_runs, m_new)
            return carry

        lax.fori_loop(0, A_TQ // A_SUB // A_LOCKSTEP, body, 0)

    quarter = A_TQ // 4
    for r4 in range(4):
        for a in range(4):
            rows = slice((r4 + 4 * a) * A_SUB, (r4 + 4 * a + 1) * A_SUB)
            mid_ref[pl.ds(r4 * quarter + a, A_SUB, stride=4), :] = oacc_ref[rows] / l_ref[rows]
    for r4 in range(4):
        nat_ref[pl.ds(r4, quarter, stride=4), :] = mid_ref[r4 * quarter:(r4 + 1) * quarter]
    o_ref[0] = nat_ref[0:A_TQ].astype(o_ref.dtype)


def dilated_attention(qkv, bias):
    b, t, _ = qkv.shape
    pairs = A_HEADS // 2
    state = pltpu.VMEM((A_TQ, LANES), F32)
    nat_pad = pltpu.VMEM((t + 2 * A_PAD, LANES), BF16)
    cls_pad = pltpu.VMEM((A_CLASSES, t // A_CLASSES + 2 * A_PAD, LANES), BF16)
    return pl.pallas_call(
        _dilated_kernel,
        grid=(b, pairs, t // A_TQ),
        in_specs=[pl.BlockSpec((1, A_TQ, LANES), lambda i, j, n: (i, n, j)),
                  pl.BlockSpec((1, t, LANES), lambda i, j, n: (i, 0, pairs + j)),
                  pl.BlockSpec((1, t, LANES), lambda i, j, n: (i, 0, 2 * pairs + j)),
                  pl.BlockSpec((1, len(DILATED_CONFIGS), 2 * A_SUB, A_BAND), lambda i, j, n: (j, 0, 0, 0))],
        out_specs=pl.BlockSpec((1, A_TQ, LANES), lambda i, j, n: (i, n, j)),
        out_shape=jax.ShapeDtypeStruct((b, t, A_WIDTH), BF16),
        scratch_shapes=[pltpu.VMEM((t, LANES), F32), pltpu.VMEM((t, LANES), F32), state,
                        nat_pad, cls_pad, nat_pad, cls_pad, state, state, state],
        compiler_params=_params(("parallel", "parallel", "arbitrary")),
        name="dilated_attention",
    )(qkv, qkv, qkv, bias)


def _dilated_bias():
    i = jnp.arange(A_SUB, dtype=jnp.int32)[:, None]
    c = jnp.arange(A_BAND, dtype=jnp.int32)[None, :]
    u_rel = {1: 16 * (i % 8) + i // 8, 4: 4 * (i % 32) + i // 32, 16: i}
    k_rel = {1: c, 4: 4 * (c % 64) + c // 64, 16: c}
    slopes = 2.0 ** (-8.0 * jnp.arange(1, A_HEADS + 1, dtype=F32) / A_HEADS)
    per_cfg = []
    for _, dil in DILATED_CONFIGS:
        hops = jnp.abs(k_rel[dil] - A_RADIUS - u_rel[dil])
        dist = (dil * hops).astype(F32)
        bias = jnp.where(hops <= A_RADIUS, -slopes[:, None, None] * dist[None] * LOG2E, NEG_INF)
        per_cfg.append(bias.reshape(A_HEADS // 2, 2 * A_SUB, A_BAND))
    return jnp.stack(per_cfg, axis=1)


def _gate_out_value(parts, gate_ref, x_ref, w_ref, g_ref):
    o = jnp.concatenate([r[...] for r in parts], axis=-1) if len(parts) > 1 else parts[0][...]
    gate = gate_ref[...].astype(F32)
    y = (o.astype(F32) * (gate * jax.nn.sigmoid(gate))).astype(BF16)
    y = jnp.dot(y, w_ref[...], preferred_element_type=F32)
    return x_ref[...] + _rms(y, g_ref[...])


def _gate_out_kernel(*refs, n_parts):
    gate_ref, x_ref, w_ref, g_ref, o_ref = refs[n_parts:]
    o_ref[...] = _gate_out_value(refs[:n_parts], gate_ref, x_ref, w_ref, g_ref)


def _gate_out_in_c_kernel(*refs, n_parts, q_scale):
    gate_ref, x_ref, w_ref, g_ref = refs[n_parts:n_parts + 4]
    in_c_inputs = refs[n_parts + 4:n_parts + 12]
    x_out_ref = refs[n_parts + 12]
    in_c_outputs = refs[n_parts + 13:]
    x_new = _gate_out_value(refs[:n_parts], gate_ref, x_ref, w_ref, g_ref)
    x_out_ref[...] = x_new
    _in_c_body(x_new, *in_c_inputs, *in_c_outputs, q_scale)


def _gate_out_specs(parts, gate, d, w):
    row = lambda i: (i, 0)
    const = lambda i: (0, 0)
    return ([pl.BlockSpec((ROW_TILE, p.shape[1]), row) for p in parts] +
            [pl.BlockSpec((ROW_TILE, gate.shape[1]), row),
             pl.BlockSpec((ROW_TILE, d), row),
             pl.BlockSpec(w.shape, const),
             pl.BlockSpec((1, d), const)])


def gate_out(parts, gate, x, w, g):
    m, d = x.shape
    return pl.pallas_call(
        functools.partial(_gate_out_kernel, n_parts=len(parts)),
        grid=(m // ROW_TILE,),
        in_specs=_gate_out_specs(parts, gate, d, w),
        out_specs=pl.BlockSpec((ROW_TILE, d), lambda i: (i, 0)),
        out_shape=jax.ShapeDtypeStruct((m, d), F32),
        compiler_params=_params(("parallel",)),
        name="gate_out",
    )(*parts, gate, x, w, g.reshape(1, d))


def gate_out_in_proj_c(parts, gate, x, w_out, g_post, seq, g_pre, w_in, gq, gk, bd, tabs):
    m, d = x.shape
    in_specs, out_specs, out_w = _in_c_specs(d, w_in, seq // ROW_TILE)
    tile4 = lambda v: jnp.tile(v.reshape(1, -1), (1, 4))
    return pl.pallas_call(
        functools.partial(_gate_out_in_c_kernel, n_parts=len(parts), q_scale=C_HEAD_DIM ** -0.5 * LOG2E),
        grid=(m // ROW_TILE,),
        in_specs=_gate_out_specs(parts, gate, d, w_out) + in_specs,
        out_specs=[pl.BlockSpec((ROW_TILE, d), lambda i: (i, 0))] + out_specs,
        out_shape=[jax.ShapeDtypeStruct((m, d), F32)] + [jax.ShapeDtypeStruct((m, n), BF16) for n in out_w],
        compiler_params=_params(("parallel",)),
        name="gate_out_in_proj_c",
    )(*parts, gate, x, w_out, g_post.reshape(1, d), g_pre.reshape(1, d), w_in, tile4(gq), tile4(gk), bd, *tabs)


def _rope_freqs():
    half = MLA_ROPE_DIM // 2
    return ROPE_THETA ** (-jnp.arange(half, dtype=F32) / half)


def _mla_tables(seq):
    ang = jnp.arange(seq, dtype=F32)[:, None] * _rope_freqs()[None, :]
    cos, sin = jnp.cos(ang), jnp.sin(ang)
    zeros = jnp.zeros_like(cos)
    ones_nope = jnp.ones((seq, MLA_NOPE_DIM), F32)
    zeros_nope = jnp.zeros((seq, MLA_NOPE_DIM), F32)
    tail = jnp.zeros((seq, LANES - MLA_NOPE_DIM - MLA_ROPE_DIM), F32)
    c = jnp.concatenate([ones_nope, cos, cos, tail], axis=1)
    sa = jnp.concatenate([zeros_nope, zeros, sin, tail], axis=1)
    sb = jnp.concatenate([zeros_nope, -sin, zeros, tail], axis=1)
    return c, sa, sb


def _axial_tables(seq):
    t = jnp.arange(seq, dtype=jnp.int32)
    f = _rope_freqs()[None, :]
    ar = (t // GRID_W).astype(F32)[:, None] * f
    ac = (t % GRID_W).astype(F32)[:, None] * f
    zeros = jnp.zeros_like(ar)
    c = jnp.concatenate([jnp.cos(ar), jnp.cos(ar), jnp.cos(ac), jnp.cos(ac)], axis=1)
    sa = jnp.concatenate([zeros, jnp.sin(ar), zeros, jnp.sin(ac)], axis=1)
    sb = jnp.concatenate([-jnp.sin(ar), zeros, -jnp.sin(ac), zeros], axis=1)
    return tuple(jnp.tile(x, (1, 2)) for x in (c, sa, sb))


def _head_block_diag():
    idx = np.arange(256) // C_HEAD_DIM
    return jnp.asarray(idx[:, None] == idx[None, :], BF16)


def _arrange_w_in_ab(w):
    qa, ka, va, cq, ckv, kr, gate = jnp.split(
        w, list(np.cumsum([A_WIDTH, A_WIDTH, A_WIDTH, MLA_Q_RANK, MLA_KV_RANK, MLA_ROPE_DIM])), axis=1)
    d = w.shape[0]
    kr_blk = jnp.concatenate([jnp.zeros((d, KR_LANE), F32), kr,
                              jnp.zeros((d, LANES - KR_LANE - MLA_ROPE_DIM), F32)], axis=1)
    qa = qa * (A_HEAD_DIM ** -0.5 * LOG2E)
    return jnp.concatenate([qa, ka, va, cq, ckv, kr_blk, gate], axis=1).astype(BF16)


def _arrange_w_in_c(w):
    d = w.shape[0]
    q, k, v, gate = jnp.split(w, list(np.cumsum([C_WIDTH, C_KV_HEADS * C_HEAD_DIM,
                                                 C_KV_HEADS * C_HEAD_DIM])), axis=1)
    k = k.reshape(d, C_KV_HEADS, 1, C_HEAD_DIM)
    kk = jnp.broadcast_to(k, (d, C_KV_HEADS, 2, C_HEAD_DIM)).reshape(d, C_KV_HEADS * LANES)
    v = v.reshape(d, C_KV_HEADS, C_HEAD_DIM)
    vv = jnp.concatenate([v, jnp.zeros_like(v)], axis=2).reshape(d, C_KV_HEADS * LANES)
    return jnp.concatenate([gate, q, kk, vv], axis=1).astype(BF16)


def _arrange_w_uq(w):
    r = w.shape[0]
    w = w.reshape(r, MLA_HEADS, MLA_NOPE_DIM + MLA_ROPE_DIM)
    pad = jnp.zeros((r, MLA_HEADS, LANES - MLA_NOPE_DIM - MLA_ROPE_DIM), F32)
    return jnp.concatenate([w, pad], axis=2).reshape(r, MLA_WIDE).astype(BF16)


def _arrange_w_ukv(w):
    r = w.shape[0]
    w = w.reshape(r, MLA_HEADS, MLA_NOPE_DIM + MLA_V_DIM)
    pad = jnp.zeros((r, MLA_HEADS, LANES - MLA_NOPE_DIM), F32)
    wk = jnp.concatenate([w[:, :, :MLA_NOPE_DIM], pad], axis=2).reshape(r, MLA_WIDE)
    wv = jnp.concatenate([w[:, :, MLA_NOPE_DIM:], pad], axis=2).reshape(r, MLA_WIDE)
    return wk.astype(BF16), wv.astype(BF16)


def _even_mixers(x, b, t, g_pre, w_in, gq, wuq, gkv, wuk, wuv, bias, tabs):
    qkv, gate, qb, kb, vb = in_proj_ab(x, t, g_pre, w_in, gq, gkv, wuq, wuk, wuv, tabs)
    three = lambda a: a.reshape(b, t, a.shape[1])
    out_a = dilated_attention(three(qkv), bias)
    out_b = mla_attention(three(qb), three(kb), three(vb), tq=1024, tk=1024, unroll=4)
    return [out_a.reshape(b * t, A_WIDTH), out_b.reshape(b * t, B_WIDTH)], gate


def _odd_mixer(b, t, q, kk, vv):
    three = lambda a: a.reshape(b, t, a.shape[1])
    o = gqa_attention(three(q), three(kk), three(vv), tq=512, tk=1024, unroll=4)
    return [o.reshape(b * t, C_WIDTH)]


def _trunk(x3, weights, consts):
    (norm_pre, norm_post, w_in_ab, mla_q_norm, w_uq, mla_kv_norm, w_uk, w_uv, w_out_ab,
     w_in_c, c_q_norm, c_k_norm, w_out_c) = weights
    b, t, d = x3.shape
    x = x3.reshape(b * t, d)
    bias, bd = consts
    mla_tabs = _mla_tables(t)
    axial_tabs = _axial_tables(t)
    depth = norm_pre.shape[0]
    open_layer = None
    for layer in range(depth):
        i = layer // 2
        if layer % 2 == 0:
            if open_layer is not None:
                x = gate_out(open_layer[0], open_layer[1], x, open_layer[2], open_layer[3])
            parts, gate = _even_mixers(x, b, t, norm_pre[layer], w_in_ab[i], mla_q_norm[i], w_uq[i],
                                       mla_kv_norm[i], w_uk[i], w_uv[i], bias, mla_tabs)
            open_layer = (parts, gate, w_out_ab[i], norm_post[layer])
        else:
            odd_in = (t, norm_pre[layer], w_in_c[i], c_q_norm[i], c_k_norm[i], bd, axial_tabs)
            if open_layer is not None:
                x, gate, q, kk, vv = gate_out_in_proj_c(open_layer[0], open_layer[1], x, open_layer[2],
                                                        open_layer[3], *odd_in)
            else:
                gate, q, kk, vv = in_proj_c(x, *odd_in)
            open_layer = (_odd_mixer(b, t, q, kk, vv), gate, w_out_c[i], norm_post[layer])
    x = gate_out(open_layer[0], open_layer[1], x, open_layer[2], open_layer[3])
    return x.reshape(b, t, d)


def kernel(x_prompt, x_sample, norm_pre, norm_post, w_in_ab, mla_q_norm, w_uq, mla_kv_norm, w_ukv,
           w_out_ab, w_in_c, c_q_norm, c_k_norm, w_out_c):
    n_even, n_odd = w_in_ab.shape[0], w_in_c.shape[0]
    w_in_ab_p = jnp.stack([_arrange_w_in_ab(w_in_ab[i]) for i in range(n_even)])
    w_uq_p = jnp.stack([_arrange_w_uq(w_uq[i]) for i in range(n_even)])
    w_ukv_p = [_arrange_w_ukv(w_ukv[i]) for i in range(n_even)]
    w_uk_p = jnp.stack([p[0] for p in w_ukv_p])
    w_uv_p = jnp.stack([p[1] for p in w_ukv_p])
    w_in_c_p = jnp.stack([_arrange_w_in_c(w_in_c[i]) for i in range(n_odd)])
    weights = (norm_pre, norm_post, w_in_ab_p, mla_q_norm, w_uq_p, mla_kv_norm, w_uk_p, w_uv_p,
               w_out_ab.astype(BF16), w_in_c_p, c_q_norm, c_k_norm, w_out_c.astype(BF16))
    consts = (_dilated_bias(), _head_block_diag())
    return (_trunk(x_prompt, weights, consts), _trunk(x_sample, weights, consts))
```

```python
import functools
import math

import numpy as np
import jax
import jax.numpy as jnp
from jax import lax
from jax.experimental import pallas as pl
from jax.experimental.pallas import tpu as pltpu

F32 = jnp.float32
BF16 = jnp.bfloat16

D_MODEL = 1024
EPS = 1e-6
ROPE_THETA = 10000.0
NEG_INF = -1e30
LOG2E = math.log2(math.e)
GRID_W = 64
LANES = 128
A_HEADS = 8
A_HEAD_DIM = 64
DILATED_CONFIGS = ((128, 1), (512, 4), (2048, 16))
A_RADIUS = 64
A_REACH = max(w // 2 for w, _ in DILATED_CONFIGS)
MLA_HEADS = 8
MLA_Q_RANK = 256
MLA_KV_RANK = 128
MLA_NOPE_DIM = 64
MLA_ROPE_DIM = 32
MLA_V_DIM = 64
C_HEADS = 16
C_KV_HEADS = 4
C_GROUPS = C_HEADS // C_KV_HEADS
C_HEAD_DIM = 64
A_WIDTH = A_HEADS * A_HEAD_DIM
B_WIDTH = MLA_HEADS * MLA_V_DIM
AB_WIDTH = A_WIDTH + B_WIDTH
C_WIDTH = C_HEADS * C_HEAD_DIM
MLA_WIDE = MLA_HEADS * LANES
COL_QKV_A, COL_LATENT, COL_GATE_AB = 0, 1536, 2048
IN_AB_PAD = 3072
KR_LANE = MLA_NOPE_DIM
COL_GATE_C, COL_QC, COL_KC, COL_VC = 0, 1024, 2048, 2560
IN_C_PAD = 3072

VMEM_LIMIT = 56 * 1024 * 1024
ROW_TILE = 512


def _params(sem):
    return pltpu.CompilerParams(dimension_semantics=sem, vmem_limit_bytes=VMEM_LIMIT)


def _rms(x, g):
    ms = jnp.mean(x * x, axis=-1, keepdims=True)
    return x * lax.rsqrt(ms + EPS) * g


def _rope_lanes(x, c, sa, sb):
    return x * c + pltpu.roll(x, 16, 1) * sa + pltpu.roll(x, 112, 1) * sb


def _ones_lane_row(width, lane):
    idx = lax.broadcasted_iota(jnp.int32, (1, width), 1)
    return jnp.where(idx % LANES == lane, 1.0, 0.0).astype(F32)


def _pipelined(jobs):
    nxt = jobs[0][0]()
    for i, (_, consume) in enumerate(jobs):
        cur = nxt
        nxt = jobs[i + 1][0]() if i + 1 < len(jobs) else None
        consume(cur)


def _in_ab_kernel(x_ref, g_ref, w_ref, gq_ref, gkv_ref, wuq_ref, wuk_ref, wuv_ref,
                  c_ref, sa_ref, sb_ref, qkv_ref, gate_ref, qb_ref, kb_ref, vb_ref, *, q_scale):
    h = _rms(x_ref[...], g_ref[...]).astype(BF16)
    c, sa, sb = c_ref[...], sa_ref[...], sb_ref[...]

    def proj(c0):
        return lambda: jnp.dot(h, w_ref[:, c0:c0 + 512], preferred_element_type=F32)

    def store(ref, c0):
        def consume(v):
            ref[:, c0:c0 + 512] = v.astype(BF16)
        return consume

    latent = {}

    def up_project(lat):
        cq = _rms(lat[:, :MLA_Q_RANK], gq_ref[...]).astype(BF16)
        ckv = _rms(lat[:, MLA_Q_RANK:MLA_Q_RANK + MLA_KV_RANK], gkv_ref[...]).astype(BF16)
        latent["kr"] = _rope_lanes(lat[:, MLA_Q_RANK + MLA_KV_RANK:], c, sa, sb)
        latent["qf"] = jnp.dot(cq, wuq_ref[...], preferred_element_type=F32)
        latent["kf"] = jnp.dot(ckv, wuk_ref[...], preferred_element_type=F32)
        vf = jnp.dot(ckv, wuv_ref[...], preferred_element_type=F32)
        vb_ref[...] = (vf + _ones_lane_row(MLA_WIDE, MLA_V_DIM)).astype(BF16)

    def rope_heads(ref, c0, heads):
        store_plain = store(ref, c0)

        def consume(v):
            store_plain(v)
            for hd in heads:
                sl = slice(hd * LANES, (hd + 1) * LANES)
                qb_ref[:, sl] = (_rope_lanes(latent["qf"][:, sl], c, sa, sb) * q_scale).astype(BF16)
                kb_ref[:, sl] = (latent["kf"][:, sl] + latent["kr"]).astype(BF16)
        return consume

    half = MLA_HEADS // 2
    _pipelined([
        (proj(COL_LATENT), up_project),
        (proj(COL_QKV_A), store(qkv_ref, 0)),
        (proj(COL_QKV_A + 512), store(qkv_ref, 512)),
        (proj(COL_QKV_A + 1024), rope_heads(qkv_ref, 1024, range(0, half))),
        (proj(COL_GATE_AB), rope_heads(gate_ref, 0, range(half, MLA_HEADS))),
        (proj(COL_GATE_AB + 512), store(gate_ref, 512)),
    ])


def in_proj_ab(x, seq, g, w, gq, gkv, wuq, wuk, wuv, tabs):
    m, d = x.shape
    tiles_per_seq = seq // ROW_TILE
    row = lambda i: (i, 0)
    const = lambda i: (0, 0)
    pos = lambda i: (i % tiles_per_seq, 0)
    out_w = (COL_LATENT, AB_WIDTH, MLA_WIDE, MLA_WIDE, MLA_WIDE)
    return pl.pallas_call(
        functools.partial(_in_ab_kernel, q_scale=(MLA_NOPE_DIM + MLA_ROPE_DIM) ** -0.5 * LOG2E),
        grid=(m // ROW_TILE,),
        in_specs=[pl.BlockSpec((ROW_TILE, d), row),
                  pl.BlockSpec((1, d), const),
                  pl.BlockSpec(w.shape, const),
                  pl.BlockSpec((1, MLA_Q_RANK), const),
                  pl.BlockSpec((1, MLA_KV_RANK), const),
                  pl.BlockSpec(wuq.shape, const),
                  pl.BlockSpec(wuk.shape, const),
                  pl.BlockSpec(wuv.shape, const)] + [pl.BlockSpec((ROW_TILE, LANES), pos)] * 3,
        out_specs=[pl.BlockSpec((ROW_TILE, n), row) for n in out_w],
        out_shape=[jax.ShapeDtypeStruct((m, n), BF16) for n in out_w],
        compiler_params=_params(("parallel",)),
        name="in_proj_ab",
    )(x, g.reshape(1, d), w, gq.reshape(1, -1), gkv.reshape(1, -1), wuq, wuk, wuv, *tabs)


def _in_c_body(x, g_ref, w_ref, gq_ref, gk_ref, bd_ref, c_ref, sa_ref, sb_ref,
               gate_ref, q_ref, kk_ref, vv_ref, q_scale):
    h = _rms(x, g_ref[...]).astype(BF16)
    c, sa, sb = c_ref[...], sa_ref[...], sb_ref[...]
    bd = bd_ref[...]

    def proj(c0):
        return lambda: jnp.dot(h, w_ref[:, c0:c0 + 512], preferred_element_type=F32)

    def head_norm_rope(x, gain, scale):
        sq = x * x
        hi = sq.astype(BF16)
        lo = (sq - hi.astype(F32)).astype(BF16)
        ss = (jnp.dot(hi, bd, preferred_element_type=F32) +
              jnp.dot(lo, bd, preferred_element_type=F32))
        xn = x * lax.rsqrt(ss * (1.0 / C_HEAD_DIM) + EPS) * gain
        halves = [_rope_lanes(xn[:, s:s + LANES], c, sa, sb) for s in (0, LANES)]
        return (jnp.concatenate(halves, axis=1) * scale).astype(BF16)

    def store(ref, c0):
        def consume(v):
            ref[:, c0:c0 + 512] = v.astype(BF16)
        return consume

    def store_heads(ref, c0, gain_ref, scale):
        def consume(v):
            for s in (0, 256):
                ref[:, c0 + s:c0 + s + 256] = head_norm_rope(v[:, s:s + 256], gain_ref[...], scale)
        return consume

    def store_values(v):
        vv_ref[...] = (v + _ones_lane_row(v.shape[1], C_HEAD_DIM)).astype(BF16)

    _pipelined([
        (proj(COL_QC), store_heads(q_ref, 0, gq_ref, q_scale)),
        (proj(COL_GATE_C), store(gate_ref, 0)),
        (proj(COL_QC + 512), store_heads(q_ref, 512, gq_ref, q_scale)),
        (proj(COL_GATE_C + 512), store(gate_ref, 512)),
        (proj(COL_KC), store_heads(kk_ref, 0, gk_ref, 1.0)),
        (proj(COL_VC), store_values),
    ])


def _in_c_kernel(x_ref, *refs, q_scale):
    _in_c_body(x_ref[...], *refs, q_scale)


def _in_c_specs(d, w, tiles_per_seq):
    row = lambda i: (i, 0)
    const = lambda i: (0, 0)
    pos = lambda i: (i % tiles_per_seq, 0)
    out_w = (C_WIDTH, C_WIDTH, C_KV_HEADS * LANES, C_KV_HEADS * LANES)
    in_specs = [pl.BlockSpec((1, d), const),
                pl.BlockSpec(w.shape, const),
                pl.BlockSpec((1, 256), const),
                pl.BlockSpec((1, 256), const),
                pl.BlockSpec((256, 256), const)] + [pl.BlockSpec((ROW_TILE, LANES), pos)] * 3
    out_specs = [pl.BlockSpec((ROW_TILE, n), row) for n in out_w]
    return in_specs, out_specs, out_w


def in_proj_c(x, seq, g, w, gq, gk, bd, tabs):
    m, d = x.shape
    in_specs, out_specs, out_w = _in_c_specs(d, w, seq // ROW_TILE)
    tile4 = lambda v: jnp.tile(v.reshape(1, -1), (1, 4))
    return pl.pallas_call(
        functools.partial(_in_c_kernel, q_scale=C_HEAD_DIM ** -0.5 * LOG2E),
        grid=(m // ROW_TILE,),
        in_specs=[pl.BlockSpec((ROW_TILE, d), lambda i: (i, 0))] + in_specs,
        out_specs=out_specs,
        out_shape=[jax.ShapeDtypeStruct((m, n), BF16) for n in out_w],
        compiler_params=_params(("parallel",)),
        name="in_proj_c",
    )(x, g.reshape(1, d), w, tile4(gq), tile4(gk), bd, *tabs)


def _online_softmax(q, k_ref, v_ref, m_ref, acc_ref, *, tk, unroll, v_groups):
    def scores(start):
        k = k_ref[0, pl.ds(start, tk), :]
        return lax.dot_general(q, k, (((1,), (1,)), ((), ())), preferred_element_type=F32)

    def products(p, start):
        return [(r0, r1, jnp.dot(p[r0:r1], v_ref[0, pl.ds(start, tk), l0:l0 + LANES],
                                 preferred_element_type=F32)) for r0, r1, l0 in v_groups]

    s = scores(0)
    m_new = jnp.broadcast_to(jnp.max(s, axis=-1, keepdims=True), m_ref.shape)
    p = jnp.exp2(s - jnp.tile(m_new, (1, tk // LANES))).astype(BF16)
    for r0, r1, pv in products(p, 0):
        acc_ref[r0:r1] = pv
    m_ref[...] = m_new

    def body(j, carry):
        start = pl.multiple_of(j * tk, tk)
        s = scores(start)
        m_old = m_ref[...]
        m_new = jnp.maximum(m_old, jnp.max(s, axis=-1, keepdims=True))
        alpha = jnp.exp2(m_old - m_new)
        p = jnp.exp2(s - jnp.tile(m_new, (1, tk // LANES))).astype(BF16)
        for r0, r1, pv in products(p, start):
            acc_ref[r0:r1] = alpha[r0:r1] * acc_ref[r0:r1] + pv
        m_ref[...] = m_new
        return carry

    lax.fori_loop(1, k_ref.shape[1] // tk, body, 0, unroll=unroll)


def _normalised(acc, dv):
    return acc[:, :dv] / acc[:, dv:dv + 1]


def _gqa_kernel(q_ref, k_ref, v_ref, o_ref, m_ref, acc_ref, *, tk, unroll):
    tq = q_ref.shape[1]
    q = q_ref[0].astype(F32)
    low = lax.broadcasted_iota(jnp.int32, (tq, LANES), 1) < C_HEAD_DIM
    parts = []
    for half in (q[:, :LANES], q[:, LANES:]):
        parts += [jnp.where(low, half, 0.0), jnp.where(low, 0.0, half)]
    qs = jnp.concatenate(parts, axis=0).astype(BF16)
    _online_softmax(qs, k_ref, v_ref, m_ref, acc_ref, tk=tk, unroll=unroll,
                    v_groups=((0, C_GROUPS * tq, 0),))
    acc = acc_ref[...]
    o_ref[0] = jnp.concatenate(
        [_normalised(acc[g * tq:(g + 1) * tq], C_HEAD_DIM) for g in range(C_GROUPS)],
        axis=1).astype(o_ref.dtype)


def gqa_attention(q, kk, vv, *, tq, tk, unroll):
    b, t, _ = q.shape
    rows = C_GROUPS * tq
    return pl.pallas_call(
        functools.partial(_gqa_kernel, tk=tk, unroll=unroll),
        grid=(b, C_KV_HEADS, t // tq),
        in_specs=[pl.BlockSpec((1, tq, 256), lambda i, j, n: (i, n, j)),
                  pl.BlockSpec((1, t, LANES), lambda i, j, n: (i, 0, j)),
                  pl.BlockSpec((1, t, LANES), lambda i, j, n: (i, 0, j))],
        out_specs=pl.BlockSpec((1, tq, 256), lambda i, j, n: (i, n, j)),
        out_shape=jax.ShapeDtypeStruct((b, t, C_WIDTH), BF16),
        scratch_shapes=[pltpu.VMEM((rows, LANES), F32), pltpu.VMEM((rows, LANES), F32)],
        compiler_params=_params(("parallel", "parallel", "arbitrary")),
        name="gqa_attention",
    )(q, kk, vv)


def _mla_kernel(q_ref, k_ref, v_ref, o_ref, m_ref, acc_ref, *, tk, unroll):
    tq = q_ref.shape[1]
    q = q_ref[0]
    zeros = jnp.zeros((tq, LANES), q.dtype)
    qs = jnp.concatenate([jnp.concatenate([q[:, :LANES], zeros], axis=1),
                          jnp.concatenate([zeros, q[:, LANES:]], axis=1)], axis=0)
    _online_softmax(qs, k_ref, v_ref, m_ref, acc_ref, tk=tk, unroll=unroll,
                    v_groups=((0, tq, 0), (tq, 2 * tq, LANES)))
    acc = acc_ref[...]
    o_ref[0] = jnp.concatenate(
        [_normalised(acc[g * tq:(g + 1) * tq], MLA_V_DIM) for g in range(2)],
        axis=1).astype(o_ref.dtype)


def mla_attention(qb, kb, vb, *, tq, tk, unroll):
    b, t, _ = qb.shape
    rows = 2 * tq
    return pl.pallas_call(
        functools.partial(_mla_kernel, tk=tk, unroll=unroll),
        grid=(b, MLA_HEADS // 2, t // tq),
        in_specs=[pl.BlockSpec((1, tq, 256), lambda i, j, n: (i, n, j)),
                  pl.BlockSpec((1, t, 256), lambda i, j, n: (i, 0, j)),
                  pl.BlockSpec((1, t, 256), lambda i, j, n: (i, 0, j))],
        out_specs=pl.BlockSpec((1, tq, LANES), lambda i, j, n: (i, n, j)),
        out_shape=jax.ShapeDtypeStruct((b, t, B_WIDTH), BF16),
        scratch_shapes=[pltpu.VMEM((rows, LANES), F32), pltpu.VMEM((rows, LANES), F32)],
        compiler_params=_params(("parallel", "parallel", "arbitrary")),
        name="mla_attention",
    )(qb, kb, vb)


A_TQ = 2048
A_SUB = 128
A_BAND = A_SUB + 2 * A_RADIUS
A_CLASSES = 16
A_LOCKSTEP = 8
A_PAD = A_RADIUS


def _class_runs(cfg, sub):
    if cfg == 2:
        return [(pl.multiple_of(sub * A_SUB, A_SUB), A_SUB)]
    if cfg == 1:
        r4, blk = sub & 3, sub >> 2
        return [(pl.multiple_of((r4 + 4 * a) * A_SUB + blk * 32, 32), 32) for a in range(4)]
    return [(pl.multiple_of(r * A_SUB + sub * 8, 8), 8) for r in range(A_CLASSES)]


def _dilated_kernel(q_ref, k_ref, v_ref, b_ref, o_ref,
                    nat_ref, mid_ref, qc_ref, kp1, kp16, vp1, vp16, oacc_ref, m_ref, l_ref):
    seq = k_ref.shape[1]
    units16 = seq // A_CLASSES
    qi = pl.program_id(2)

    def split_classes(n_rows, emit):
        quarter = n_rows // 4
        for r4 in range(4):
            mid_ref[r4 * quarter:(r4 + 1) * quarter] = nat_ref[pl.ds(r4, quarter, stride=4), :]
        for r4 in range(4):
            for a in range(4):
                emit(r4 + 4 * a, mid_ref[pl.ds(r4 * quarter + a, quarter // 4, stride=4), :])

    @pl.when(qi == 0)
    def _():
        zeros = jnp.zeros((A_PAD, LANES), BF16)
        for src, nat_pad, cls_pad in ((k_ref, kp1, kp16), (v_ref, vp1, vp16)):
            nat_pad[0:A_PAD] = zeros
            nat_pad[A_PAD + seq:] = zeros
            nat_pad[A_PAD:A_PAD + seq] = src[0]
            nat_ref[...] = src[0].astype(F32)

            def emit(r, rows, cls_pad=cls_pad):
                cls_pad[r, 0:A_PAD] = zeros
                cls_pad[r, A_PAD + units16:] = zeros
                cls_pad[r, A_PAD:A_PAD + units16] = rows.astype(BF16)

            split_classes(seq, emit)

    nat_ref[0:A_TQ] = q_ref[0].astype(F32)

    def emit_q(r, rows):
        qc_ref[r * A_SUB:(r + 1) * A_SUB] = rows

    split_classes(A_TQ, emit_q)
    low = lax.broadcasted_iota(jnp.int32, (A_SUB, LANES), 1) < A_HEAD_DIM
    col = lax.broadcasted_iota(jnp.int32, (1, A_BAND), 1)
    col_unit = (col, 4 * (col % 64) + col // 64, col)

    for ci, (_, dil) in enumerate(DILATED_CONFIGS):
        units = seq // dil
        per_step = A_TQ // dil

        def body(it, carry, ci=ci, units=units, per_step=per_step):
            subs = [it * A_LOCKSTEP + u for u in range(A_LOCKSTEP)]
            if ci == 2:
                first = [qi * per_step] * len(subs)
            elif ci == 1:
                first = [qi * per_step + (s_ >> 2) * A_SUB for s_ in subs]
            else:
                first = [qi * per_step + s_ * A_SUB for s_ in subs]
            runs = [_class_runs(ci, s_) for s_ in subs]

            def window(nat_pad, cls_pad, sub, f):
                if ci == 0:
                    return nat_pad[pl.ds(pl.multiple_of(f, A_SUB), A_BAND), :]
                if ci == 2:
                    return cls_pad[sub, pl.ds(pl.multiple_of(f, A_SUB), A_BAND), :]
                row = pl.multiple_of((f >> 2) + A_PAD - A_RADIUS // 4, 16)
                return jnp.concatenate([cls_pad[(sub & 3) + 4 * a, pl.ds(row, A_BAND // 4), :]
                                        for a in range(4)], axis=0)

            def gather(ref, sub_runs):
                parts = [ref[pl.ds(start, size), :] for start, size in sub_runs]
                return parts[0] if len(parts) == 1 else jnp.concatenate(parts, axis=0)

            def scatter(ref, sub_runs, val):
                row = 0
                for start, size in sub_runs:
                    ref[pl.ds(start, size), :] = val[row:row + size]
                    row += size

            qs = []
            for sub_runs in runs:
                q = gather(qc_ref, sub_runs)
                qs.append(jnp.concatenate([jnp.where(low, q, 0.0), jnp.where(low, 0.0, q)],
                                          axis=0).astype(BF16))
            kw = [window(kp1, kp16, s_, f) for s_, f in zip(subs, first)]
            vw = [window(vp1, vp16, s_, f) for s_, f in zip(subs, first)]
            s = [lax.dot_general(a, b_, (((1,), (1,)), ((), ())), preferred_element_type=F32)
                 for a, b_ in zip(qs, kw)]
            bias = b_ref[0, ci]
            masked = []
            for s_u, f in zip(s, first):
                unit = f - A_RADIUS + col_unit[ci]
                valid = jnp.where((unit >= 0) & (unit < units), 0.0, NEG_INF)
                masked.append(s_u + bias + valid)
            m_blk = [jnp.max(x, axis=-1, keepdims=True) for x in masked]
            p = [jnp.exp2(x - m) for x, m in zip(masked, m_blk)]
            l_blk = [jnp.sum(x, axis=-1, keepdims=True) for x in p]
            o_blk = [jnp.dot(x.astype(BF16), v, preferred_element_type=F32) for x, v in zip(p, vw)]
            for sub_runs, o_u, m_u, l_u in zip(runs, o_blk, m_blk, l_blk):
                o2 = jnp.where(low, o_u[:A_SUB], o_u[A_SUB:])
                m2 = jnp.where(low, m_u[:A_SUB], m_u[A_SUB:])
                l2 = jnp.where(low, l_u[:A_SUB], l_u[A_SUB:])
                if ci == 0:
                    scatter(oacc_ref, sub_runs, o2)
                    scatter(m_ref, sub_runs, m2)
                    scatter(l_ref, sub_runs, l2)
                else:
                    m_old = gather(m_ref, sub_runs)
                    m_new = jnp.maximum(m_old, m2)
                    a_old = jnp.exp2(m_old - m_new)
                    a_blk = jnp.exp2(m2 - m_new)
                    scatter(oacc_ref, sub_runs, gather(oacc_ref, sub_runs) * a_old + o2 * a_blk)
                    scatter(l_ref, sub_runs, gather(l_ref, sub_runs) * a_old + l2 * a_blk)
                    scatter(m_ref, sub_runs, m_new)
            return carry

        lax.fori_loop(0, A_TQ // A_SUB // A_LOCKSTEP, body, 0)

    quarter = A_TQ // 4
    for r4 in range(4):
        for a in range(4):
            rows = slice((r4 + 4 * a) * A_SUB, (r4 + 4 * a + 1) * A_SUB)
            mid_ref[pl.ds(r4 * quarter + a, A_SUB, stride=4), :] = oacc_ref[rows] / l_ref[rows]
    for r4 in range(4):
        nat_ref[pl.ds(r4, quarter, stride=4), :] = mid_ref[r4 * quarter:(r4 + 1) * quarter]
    o_ref[0] = nat_ref[0:A_TQ].astype(o_ref.dtype)


def dilated_attention(qkv, bias):
    b, t, _ = qkv.shape
    pairs = A_HEADS // 2
    state = pltpu.VMEM((A_TQ, LANES), F32)
    nat_pad = pltpu.VMEM((t + 2 * A_PAD, LANES), BF16)
    cls_pad = pltpu.VMEM((A_CLASSES, t // A_CLASSES + 2 * A_PAD, LANES), BF16)
    return pl.pallas_call(
        _dilated_kernel,
        grid=(b, pairs, t // A_TQ),
        in_specs=[pl.BlockSpec((1, A_TQ, LANES), lambda i, j, n: (i, n, j)),
                  pl.BlockSpec((1, t, LANES), lambda i, j, n: (i, 0, pairs + j)),
                  pl.BlockSpec((1, t, LANES), lambda i, j, n: (i, 0, 2 * pairs + j)),
                  pl.BlockSpec((1, len(DILATED_CONFIGS), 2 * A_SUB, A_BAND), lambda i, j, n: (j, 0, 0, 0))],
        out_specs=pl.BlockSpec((1, A_TQ, LANES), lambda i, j, n: (i, n, j)),
        out_shape=jax.ShapeDtypeStruct((b, t, A_WIDTH), BF16),
        scratch_shapes=[pltpu.VMEM((t, LANES), F32), pltpu.VMEM((t, LANES), F32), state,
                        nat_pad, cls_pad, nat_pad, cls_pad, state, state, state],
        compiler_params=_params(("parallel", "parallel", "arbitrary")),
        name="dilated_attention",
    )(qkv, qkv, qkv, bias)


def _dilated_bias():
    i = jnp.arange(A_SUB, dtype=jnp.int32)[:, None]
    c = jnp.arange(A_BAND, dtype=jnp.int32)[None, :]
    u_rel = {1: 16 * (i % 8) + i // 8, 4: 4 * (i % 32) + i // 32, 16: i}
    k_rel = {1: c, 4: 4 * (c % 64) + c // 64, 16: c}
    slopes = 2.0 ** (-8.0 * jnp.arange(1, A_HEADS + 1, dtype=F32) / A_HEADS)
    per_cfg = []
    for _, dil in DILATED_CONFIGS:
        hops = jnp.abs(k_rel[dil] - A_RADIUS - u_rel[dil])
        dist = (dil * hops).astype(F32)
        bias = jnp.where(hops <= A_RADIUS, -slopes[:, None, None] * dist[None] * LOG2E, NEG_INF)
        per_cfg.append(bias.reshape(A_HEADS // 2, 2 * A_SUB, A_BAND))
    return jnp.stack(per_cfg, axis=1)


def _gate_out_value(parts, gate_ref, x_ref, w_ref, g_ref):
    o = jnp.concatenate([r[...] for r in parts], axis=-1) if len(parts) > 1 else parts[0][...]
    gate = gate_ref[...].astype(F32)
    y = (o.astype(F32) * (gate * jax.nn.sigmoid(gate))).astype(BF16)
    y = jnp.dot(y, w_ref[...], preferred_element_type=F32)
    return x_ref[...] + _rms(y, g_ref[...])


def _gate_out_kernel(*refs, n_parts):
    gate_ref, x_ref, w_ref, g_ref, o_ref = refs[n_parts:]
    o_ref[...] = _gate_out_value(refs[:n_parts], gate_ref, x_ref, w_ref, g_ref)


def _gate_out_in_c_kernel(*refs, n_parts, q_scale):
    gate_ref, x_ref, w_ref, g_ref = refs[n_parts:n_parts + 4]
    in_c_inputs = refs[n_parts + 4:n_parts + 12]
    x_out_ref = refs[n_parts + 12]
    in_c_outputs = refs[n_parts + 13:]
    x_new = _gate_out_value(refs[:n_parts], gate_ref, x_ref, w_ref, g_ref)
    x_out_ref[...] = x_new
    _in_c_body(x_new, *in_c_inputs, *in_c_outputs, q_scale)


def _gate_out_specs(parts, gate, d, w):
    row = lambda i: (i, 0)
    const = lambda i: (0, 0)
    return ([pl.BlockSpec((ROW_TILE, p.shape[1]), row) for p in parts] +
            [pl.BlockSpec((ROW_TILE, gate.shape[1]), row),
             pl.BlockSpec((ROW_TILE, d), row),
             pl.BlockSpec(w.shape, const),
             pl.BlockSpec((1, d), const)])


def gate_out(parts, gate, x, w, g):
    m, d = x.shape
    return pl.pallas_call(
        functools.partial(_gate_out_kernel, n_parts=len(parts)),
        grid=(m // ROW_TILE,),
        in_specs=_gate_out_specs(parts, gate, d, w),
        out_specs=pl.BlockSpec((ROW_TILE, d), lambda i: (i, 0)),
        out_shape=jax.ShapeDtypeStruct((m, d), F32),
        compiler_params=_params(("parallel",)),
        name="gate_out",
    )(*parts, gate, x, w, g.reshape(1, d))


def gate_out_in_proj_c(parts, gate, x, w_out, g_post, seq, g_pre, w_in, gq, gk, bd, tabs):
    m, d = x.shape
    in_specs, out_specs, out_w = _in_c_specs(d, w_in, seq // ROW_TILE)
    tile4 = lambda v: jnp.tile(v.reshape(1, -1), (1, 4))
    return pl.pallas_call(
        functools.partial(_gate_out_in_c_kernel, n_parts=len(parts), q_scale=C_HEAD_DIM ** -0.5 * LOG2E),
        grid=(m // ROW_TILE,),
        in_specs=_gate_out_specs(parts, gate, d, w_out) + in_specs,
        out_specs=[pl.BlockSpec((ROW_TILE, d), lambda i: (i, 0))] + out_specs,
        out_shape=[jax.ShapeDtypeStruct((m, d), F32)] + [jax.ShapeDtypeStruct((m, n), BF16) for n in out_w],
        compiler_params=_params(("parallel",)),
        name="gate_out_in_proj_c",
    )(*parts, gate, x, w_out, g_post.reshape(1, d), g_pre.reshape(1, d), w_in, tile4(gq), tile4(gk), bd, *tabs)


def _rope_freqs():
    half = MLA_ROPE_DIM // 2
    return ROPE_THETA ** (-jnp.arange(half, dtype=F32) / half)


def _mla_tables(seq):
    ang = jnp.arange(seq, dtype=F32)[:, None] * _rope_freqs()[None, :]
    cos, sin = jnp.cos(ang), jnp.sin(ang)
    zeros = jnp.zeros_like(cos)
    ones_nope = jnp.ones((seq, MLA_NOPE_DIM), F32)
    zeros_nope = jnp.zeros((seq, MLA_NOPE_DIM), F32)
    tail = jnp.zeros((seq, LANES - MLA_NOPE_DIM - MLA_ROPE_DIM), F32)
    c = jnp.concatenate([ones_nope, cos, cos, tail], axis=1)
    sa = jnp.concatenate([zeros_nope, zeros, sin, tail], axis=1)
    sb = jnp.concatenate([zeros_nope, -sin, zeros, tail], axis=1)
    return c, sa, sb


def _axial_tables(seq):
    t = jnp.arange(seq, dtype=jnp.int32)
    f = _rope_freqs()[None, :]
    ar = (t // GRID_W).astype(F32)[:, None] * f
    ac = (t % GRID_W).astype(F32)[:, None] * f
    zeros = jnp.zeros_like(ar)
    c = jnp.concatenate([jnp.cos(ar), jnp.cos(ar), jnp.cos(ac), jnp.cos(ac)], axis=1)
    sa = jnp.concatenate([zeros, jnp.sin(ar), zeros, jnp.sin(ac)], axis=1)
    sb = jnp.concatenate([-jnp.sin(ar), zeros, -jnp.sin(ac), zeros], axis=1)
    return tuple(jnp.tile(x, (1, 2)) for x in (c, sa, sb))


def _head_block_diag():
    idx = np.arange(256) // C_HEAD_DIM
    return jnp.asarray(idx[:, None] == idx[None, :], BF16)


def _arrange_w_in_ab(w):
    qa, ka, va, cq, ckv, kr, gate = jnp.split(
        w, list(np.cumsum([A_WIDTH, A_WIDTH, A_WIDTH, MLA_Q_RANK, MLA_KV_RANK, MLA_ROPE_DIM])), axis=1)
    d = w.shape[0]
    kr_blk = jnp.concatenate([jnp.zeros((d, KR_LANE), F32), kr,
                              jnp.zeros((d, LANES - KR_LANE - MLA_ROPE_DIM), F32)], axis=1)
    qa = qa * (A_HEAD_DIM ** -0.5 * LOG2E)
    return jnp.concatenate([qa, ka, va, cq, ckv, kr_blk, gate], axis=1).astype(BF16)


def _arrange_w_in_c(w):
    d = w.shape[0]
    q, k, v, gate = jnp.split(w, list(np.cumsum([C_WIDTH, C_KV_HEADS * C_HEAD_DIM,
                                                 C_KV_HEADS * C_HEAD_DIM])), axis=1)
    k = k.reshape(d, C_KV_HEADS, 1, C_HEAD_DIM)
    kk = jnp.broadcast_to(k, (d, C_KV_HEADS, 2, C_HEAD_DIM)).reshape(d, C_KV_HEADS * LANES)
    v = v.reshape(d, C_KV_HEADS, C_HEAD_DIM)
    vv = jnp.concatenate([v, jnp.zeros_like(v)], axis=2).reshape(d, C_KV_HEADS * LANES)
    return jnp.concatenate([gate, q, kk, vv], axis=1).astype(BF16)


def _arrange_w_uq(w):
    r = w.shape[0]
    w = w.reshape(r, MLA_HEADS, MLA_NOPE_DIM + MLA_ROPE_DIM)
    pad = jnp.zeros((r, MLA_HEADS, LANES - MLA_NOPE_DIM - MLA_ROPE_DIM), F32)
    return jnp.concatenate([w, pad], axis=2).reshape(r, MLA_WIDE).astype(BF16)


def _arrange_w_ukv(w):
    r = w.shape[0]
    w = w.reshape(r, MLA_HEADS, MLA_NOPE_DIM + MLA_V_DIM)
    pad = jnp.zeros((r, MLA_HEADS, LANES - MLA_NOPE_DIM), F32)
    wk = jnp.concatenate([w[:, :, :MLA_NOPE_DIM], pad], axis=2).reshape(r, MLA_WIDE)
    wv = jnp.concatenate([w[:, :, MLA_NOPE_DIM:], pad], axis=2).reshape(r, MLA_WIDE)
    return wk.astype(BF16), wv.astype(BF16)


def _even_mixers(x, b, t, g_pre, w_in, gq, wuq, gkv, wuk, wuv, bias, tabs):
    qkv, gate, qb, kb, vb = in_proj_ab(x, t, g_pre, w_in, gq, gkv, wuq, wuk, wuv, tabs)
    three = lambda a: a.reshape(b, t, a.shape[1])
    out_a = dilated_attention(three(qkv), bias)
    out_b = mla_attention(three(qb), three(kb), three(vb), tq=1024, tk=2048, unroll=4)
    return [out_a.reshape(b * t, A_WIDTH), out_b.reshape(b * t, B_WIDTH)], gate


def _odd_mixer(b, t, q, kk, vv):
    three = lambda a: a.reshape(b, t, a.shape[1])
    o = gqa_attention(three(q), three(kk), three(vv), tq=512, tk=2048, unroll=4)
    return [o.reshape(b * t, C_WIDTH)]


def _trunk(x3, weights, consts):
    (norm_pre, norm_post, w_in_ab, mla_q_norm, w_uq, mla_kv_norm, w_uk, w_uv, w_out_ab,
     w_in_c, c_q_norm, c_k_norm, w_out_c) = weights
    b, t, d = x3.shape
    x = x3.reshape(b * t, d)
    bias, bd = consts
    mla_tabs = _mla_tables(t)
    axial_tabs = _axial_tables(t)
    depth = norm_pre.shape[0]
    open_layer = None
    for layer in range(depth):
        i = layer // 2
        if layer % 2 == 0:
            if open_layer is not None:
                x = gate_out(open_layer[0], open_layer[1], x, open_layer[2], open_layer[3])
            parts, gate = _even_mixers(x, b, t, norm_pre[layer], w_in_ab[i], mla_q_norm[i], w_uq[i],
                                       mla_kv_norm[i], w_uk[i], w_uv[i], bias, mla_tabs)
            open_layer = (parts, gate, w_out_ab[i], norm_post[layer])
        else:
            odd_in = (t, norm_pre[layer], w_in_c[i], c_q_norm[i], c_k_norm[i], bd, axial_tabs)
            if open_layer is not None:
                x, gate, q, kk, vv = gate_out_in_proj_c(open_layer[0], open_layer[1], x, open_layer[2],
                                                        open_layer[3], *odd_in)
            else:
                gate, q, kk, vv = in_proj_c(x, *odd_in)
            open_layer = (_odd_mixer(b, t, q, kk, vv), gate, w_out_c[i], norm_post[layer])
    x = gate_out(open_layer[0], open_layer[1], x, open_layer[2], open_layer[3])
    return x.reshape(b, t, d)


def kernel(x_prompt, x_sample, norm_pre, norm_post, w_in_ab, mla_q_norm, w_uq, mla_kv_norm, w_ukv,
           w_out_ab, w_in_c, c_q_norm, c_k_norm, w_out_c):
    n_even, n_odd = w_in_ab.shape[0], w_in_c.shape[0]
    w_in_ab_p = jnp.stack([_arrange_w_in_ab(w_in_ab[i]) for i in range(n_even)])
    w_uq_p = jnp.stack([_arrange_w_uq(w_uq[i]) for i in range(n_even)])
    w_ukv_p = [_arrange_w_ukv(w_ukv[i]) for i in range(n_even)]
    w_uk_p = jnp.stack([p[0] for p in w_ukv_p])
    w_uv_p = jnp.stack([p[1] for p in w_ukv_p])
    w_in_c_p = jnp.stack([_arrange_w_in_c(w_in_c[i]) for i in range(n_odd)])
    weights = (norm_pre, norm_post, w_in_ab_p, mla_q_norm, w_uq_p, mla_kv_norm, w_uk_p, w_uv_p,
               w_out_ab.astype(BF16), w_in_c_p, c_q_norm, c_k_norm, w_out_c.astype(BF16))
    consts = (_dilated_bias(), _head_block_diag())
    return (_trunk(x_prompt, weights, consts), _trunk(x_sample, weights, consts))
```

```python
import functools
import math

import numpy as np
import jax
import jax.numpy as jnp
from jax import lax
from jax.experimental import pallas as pl
from jax.experimental.pallas import tpu as pltpu

F32 = jnp.float32
BF16 = jnp.bfloat16

D_MODEL = 1024
EPS = 1e-6
ROPE_THETA = 10000.0
NEG_INF = -1e30
LOG2E = math.log2(math.e)
GRID_W = 64
LANES = 128
A_HEADS = 8
A_HEAD_DIM = 64
DILATED_CONFIGS = ((128, 1), (512, 4), (2048, 16))
A_RADIUS = 64
A_REACH = max(w // 2 for w, _ in DILATED_CONFIGS)
MLA_HEADS = 8
MLA_Q_RANK = 256
MLA_KV_RANK = 128
MLA_NOPE_DIM = 64
MLA_ROPE_DIM = 32
MLA_V_DIM = 64
C_HEADS = 16
C_KV_HEADS = 4
C_GROUPS = C_HEADS // C_KV_HEADS
C_HEAD_DIM = 64
A_WIDTH = A_HEADS * A_HEAD_DIM
B_WIDTH = MLA_HEADS * MLA_V_DIM
AB_WIDTH = A_WIDTH + B_WIDTH
C_WIDTH = C_HEADS * C_HEAD_DIM
MLA_WIDE = MLA_HEADS * LANES
COL_QKV_A, COL_LATENT, COL_GATE_AB = 0, 1536, 2048
IN_AB_PAD = 3072
KR_LANE = MLA_NOPE_DIM
COL_GATE_C, COL_QC, COL_KC, COL_VC = 0, 1024, 2048, 2560
IN_C_PAD = 3072

VMEM_LIMIT = 56 * 1024 * 1024
ROW_TILE = 512


def _params(sem):
    return pltpu.CompilerParams(dimension_semantics=sem, vmem_limit_bytes=VMEM_LIMIT)


def _rms(x, g):
    ms = jnp.mean(x * x, axis=-1, keepdims=True)
    return x * lax.rsqrt(ms + EPS) * g


def _rope_lanes(x, c, sa, sb):
    return x * c + pltpu.roll(x, 16, 1) * sa + pltpu.roll(x, 112, 1) * sb


def _ones_lane_row(width, lane):
    idx = lax.broadcasted_iota(jnp.int32, (1, width), 1)
    return jnp.where(idx % LANES == lane, 1.0, 0.0).astype(F32)


def _pipelined(jobs):
    nxt = jobs[0][0]()
    for i, (_, consume) in enumerate(jobs):
        cur = nxt
        nxt = jobs[i + 1][0]() if i + 1 < len(jobs) else None
        consume(cur)


def _in_ab_kernel(x_ref, g_ref, w_ref, gq_ref, gkv_ref, wuq_ref, wuk_ref, wuv_ref,
                  c_ref, sa_ref, sb_ref, qkv_ref, gate_ref, qb_ref, kb_ref, vb_ref, *, q_scale):
    h = _rms(x_ref[...], g_ref[...]).astype(BF16)
    c, sa, sb = c_ref[...], sa_ref[...], sb_ref[...]

    def proj(c0):
        return lambda: jnp.dot(h, w_ref[:, c0:c0 + 512], preferred_element_type=F32)

    def store(ref, c0):
        def consume(v):
            ref[:, c0:c0 + 512] = v.astype(BF16)
        return consume

    latent = {}

    def up_project(lat):
        cq = _rms(lat[:, :MLA_Q_RANK], gq_ref[...]).astype(BF16)
        ckv = _rms(lat[:, MLA_Q_RANK:MLA_Q_RANK + MLA_KV_RANK], gkv_ref[...]).astype(BF16)
        latent["kr"] = _rope_lanes(lat[:, MLA_Q_RANK + MLA_KV_RANK:], c, sa, sb)
        latent["qf"] = jnp.dot(cq, wuq_ref[...], preferred_element_type=F32)
        latent["kf"] = jnp.dot(ckv, wuk_ref[...], preferred_element_type=F32)
        vf = jnp.dot(ckv, wuv_ref[...], preferred_element_type=F32)
        vb_ref[...] = (vf + _ones_lane_row(MLA_WIDE, MLA_V_DIM)).astype(BF16)

    def rope_heads(ref, c0, heads):
        store_plain = store(ref, c0)

        def consume(v):
            store_plain(v)
            for hd in heads:
                sl = slice(hd * LANES, (hd + 1) * LANES)
                qb_ref[:, sl] = (_rope_lanes(latent["qf"][:, sl], c, sa, sb) * q_scale).astype(BF16)
                kb_ref[:, sl] = (latent["kf"][:, sl] + latent["kr"]).astype(BF16)
        return consume

    half = MLA_HEADS // 2
    _pipelined([
        (proj(COL_LATENT), up_project),
        (proj(COL_QKV_A), store(qkv_ref, 0)),
        (proj(COL_QKV_A + 512), store(qkv_ref, 512)),
        (proj(COL_QKV_A + 1024), rope_heads(qkv_ref, 1024, range(0, half))),
        (proj(COL_GATE_AB), rope_heads(gate_ref, 0, range(half, MLA_HEADS))),
        (proj(COL_GATE_AB + 512), store(gate_ref, 512)),
    ])


def in_proj_ab(x, seq, g, w, gq, gkv, wuq, wuk, wuv, tabs):
    m, d = x.shape
    tiles_per_seq = seq // ROW_TILE
    row = lambda i: (i, 0)
    const = lambda i: (0, 0)
    pos = lambda i: (i % tiles_per_seq, 0)
    out_w = (COL_LATENT, AB_WIDTH, MLA_WIDE, MLA_WIDE, MLA_WIDE)
    return pl.pallas_call(
        functools.partial(_in_ab_kernel, q_scale=(MLA_NOPE_DIM + MLA_ROPE_DIM) ** -0.5 * LOG2E),
        grid=(m // ROW_TILE,),
        in_specs=[pl.BlockSpec((ROW_TILE, d), row),
                  pl.BlockSpec((1, d), const),
                  pl.BlockSpec(w.shape, const),
                  pl.BlockSpec((1, MLA_Q_RANK), const),
                  pl.BlockSpec((1, MLA_KV_RANK), const),
                  pl.BlockSpec(wuq.shape, const),
                  pl.BlockSpec(wuk.shape, const),
                  pl.BlockSpec(wuv.shape, const)] + [pl.BlockSpec((ROW_TILE, LANES), pos)] * 3,
        out_specs=[pl.BlockSpec((ROW_TILE, n), row) for n in out_w],
        out_shape=[jax.ShapeDtypeStruct((m, n), BF16) for n in out_w],
        compiler_params=_params(("parallel",)),
        name="in_proj_ab",
    )(x, g.reshape(1, d), w, gq.reshape(1, -1), gkv.reshape(1, -1), wuq, wuk, wuv, *tabs)


def _in_c_body(x, g_ref, w_ref, gq_ref, gk_ref, bd_ref, c_ref, sa_ref, sb_ref,
               gate_ref, q_ref, kk_ref, vv_ref, q_scale):
    h = _rms(x, g_ref[...]).astype(BF16)
    c, sa, sb = c_ref[...], sa_ref[...], sb_ref[...]
    bd = bd_ref[...]

    def proj(c0):
        return lambda: jnp.dot(h, w_ref[:, c0:c0 + 512], preferred_element_type=F32)

    def head_norm_rope(x, gain, scale):
        sq = x * x
        hi = sq.astype(BF16)
        lo = (sq - hi.astype(F32)).astype(BF16)
        ss = (jnp.dot(hi, bd, preferred_element_type=F32) +
              jnp.dot(lo, bd, preferred_element_type=F32))
        xn = x * lax.rsqrt(ss * (1.0 / C_HEAD_DIM) + EPS) * gain
        halves = [_rope_lanes(xn[:, s:s + LANES], c, sa, sb) for s in (0, LANES)]
        return (jnp.concatenate(halves, axis=1) * scale).astype(BF16)

    def store(ref, c0):
        def consume(v):
            ref[:, c0:c0 + 512] = v.astype(BF16)
        return consume

    def store_heads(ref, c0, gain_ref, scale):
        def consume(v):
            for s in (0, 256):
                ref[:, c0 + s:c0 + s + 256] = head_norm_rope(v[:, s:s + 256], gain_ref[...], scale)
        return consume

    def store_values(v):
        vv_ref[...] = (v + _ones_lane_row(v.shape[1], C_HEAD_DIM)).astype(BF16)

    _pipelined([
        (proj(COL_QC), store_heads(q_ref, 0, gq_ref, q_scale)),
        (proj(COL_GATE_C), store(gate_ref, 0)),
        (proj(COL_QC + 512), store_heads(q_ref, 512, gq_ref, q_scale)),
        (proj(COL_GATE_C + 512), store(gate_ref, 512)),
        (proj(COL_KC), store_heads(kk_ref, 0, gk_ref, 1.0)),
        (proj(COL_VC), store_values),
    ])


def _in_c_kernel(x_ref, *refs, q_scale):
    _in_c_body(x_ref[...], *refs, q_scale)


def _in_c_specs(d, w, tiles_per_seq):
    row = lambda i: (i, 0)
    const = lambda i: (0, 0)
    pos = lambda i: (i % tiles_per_seq, 0)
    out_w = (C_WIDTH, C_WIDTH, C_KV_HEADS * LANES, C_KV_HEADS * LANES)
    in_specs = [pl.BlockSpec((1, d), const),
                pl.BlockSpec(w.shape, const),
                pl.BlockSpec((1, 256), const),
                pl.BlockSpec((1, 256), const),
                pl.BlockSpec((256, 256), const)] + [pl.BlockSpec((ROW_TILE, LANES), pos)] * 3
    out_specs = [pl.BlockSpec((ROW_TILE, n), row) for n in out_w]
    return in_specs, out_specs, out_w


def in_proj_c(x, seq, g, w, gq, gk, bd, tabs):
    m, d = x.shape
    in_specs, out_specs, out_w = _in_c_specs(d, w, seq // ROW_TILE)
    tile4 = lambda v: jnp.tile(v.reshape(1, -1), (1, 4))
    return pl.pallas_call(
        functools.partial(_in_c_kernel, q_scale=C_HEAD_DIM ** -0.5 * LOG2E),
        grid=(m // ROW_TILE,),
        in_specs=[pl.BlockSpec((ROW_TILE, d), lambda i: (i, 0))] + in_specs,
        out_specs=out_specs,
        out_shape=[jax.ShapeDtypeStruct((m, n), BF16) for n in out_w],
        compiler_params=_params(("parallel",)),
        name="in_proj_c",
    )(x, g.reshape(1, d), w, tile4(gq), tile4(gk), bd, *tabs)


def _online_softmax(q, k_ref, v_ref, m_ref, acc_ref, *, tk, unroll, v_groups):
    def scores(start):
        k = k_ref[0, pl.ds(start, tk), :]
        return lax.dot_general(q, k, (((1,), (1,)), ((), ())), preferred_element_type=F32)

    def products(p, start):
        return [(r0, r1, jnp.dot(p[r0:r1], v_ref[0, pl.ds(start, tk), l0:l0 + LANES],
                                 preferred_element_type=F32)) for r0, r1, l0 in v_groups]

    s = scores(0)
    m_new = jnp.broadcast_to(jnp.max(s, axis=-1, keepdims=True), m_ref.shape)
    p = jnp.exp2(s - jnp.tile(m_new, (1, tk // LANES))).astype(BF16)
    for r0, r1, pv in products(p, 0):
        acc_ref[r0:r1] = pv
    m_ref[...] = m_new

    def body(j, carry):
        start = pl.multiple_of(j * tk, tk)
        s = scores(start)
        m_old = m_ref[...]
        m_new = jnp.maximum(m_old, jnp.max(s, axis=-1, keepdims=True))
        alpha = jnp.exp2(m_old - m_new)
        p = jnp.exp2(s - jnp.tile(m_new, (1, tk // LANES))).astype(BF16)
        for r0, r1, pv in products(p, start):
            acc_ref[r0:r1] = alpha[r0:r1] * acc_ref[r0:r1] + pv
        m_ref[...] = m_new
        return carry

    lax.fori_loop(1, k_ref.shape[1] // tk, body, 0, unroll=unroll)


def _normalised(acc, dv):
    return acc[:, :dv] / acc[:, dv:dv + 1]


def _gqa_kernel(q_ref, k_ref, v_ref, o_ref, m_ref, acc_ref, *, tk, unroll):
    tq = q_ref.shape[1]
    q = q_ref[0].astype(F32)
    low = lax.broadcasted_iota(jnp.int32, (tq, LANES), 1) < C_HEAD_DIM
    parts = []
    for half in (q[:, :LANES], q[:, LANES:]):
        parts += [jnp.where(low, half, 0.0), jnp.where(low, 0.0, half)]
    qs = jnp.concatenate(parts, axis=0).astype(BF16)
    _online_softmax(qs, k_ref, v_ref, m_ref, acc_ref, tk=tk, unroll=unroll,
                    v_groups=((0, C_GROUPS * tq, 0),))
    acc = acc_ref[...]
    o_ref[0] = jnp.concatenate(
        [_normalised(acc[g * tq:(g + 1) * tq], C_HEAD_DIM) for g in range(C_GROUPS)],
        axis=1).astype(o_ref.dtype)


def gqa_attention(q, kk, vv, *, tq, tk, unroll):
    b, t, _ = q.shape
    rows = C_GROUPS * tq
    return pl.pallas_call(
        functools.partial(_gqa_kernel, tk=tk, unroll=unroll),
        grid=(b, C_KV_HEADS, t // tq),
        in_specs=[pl.BlockSpec((1, tq, 256), lambda i, j, n: (i, n, j)),
                  pl.BlockSpec((1, t, LANES), lambda i, j, n: (i, 0, j)),
                  pl.BlockSpec((1, t, LANES), lambda i, j, n: (i, 0, j))],
        out_specs=pl.BlockSpec((1, tq, 256), lambda i, j, n: (i, n, j)),
        out_shape=jax.ShapeDtypeStruct((b, t, C_WIDTH), BF16),
        scratch_shapes=[pltpu.VMEM((rows, LANES), F32), pltpu.VMEM((rows, LANES), F32)],
        compiler_params=_params(("parallel", "parallel", "arbitrary")),
        name="gqa_attention",
    )(q, kk, vv)


def _mla_kernel(q_ref, k_ref, v_ref, o_ref, m_ref, acc_ref, *, tk, unroll):
    tq = q_ref.shape[1]
    q = q_ref[0]
    zeros = jnp.zeros((tq, LANES), q.dtype)
    qs = jnp.concatenate([jnp.concatenate([q[:, :LANES], zeros], axis=1),
                          jnp.concatenate([zeros, q[:, LANES:]], axis=1)], axis=0)
    _online_softmax(qs, k_ref, v_ref, m_ref, acc_ref, tk=tk, unroll=unroll,
                    v_groups=((0, tq, 0), (tq, 2 * tq, LANES)))
    acc = acc_ref[...]
    o_ref[0] = jnp.concatenate(
        [_normalised(acc[g * tq:(g + 1) * tq], MLA_V_DIM) for g in range(2)],
        axis=1).astype(o_ref.dtype)


def mla_attention(qb, kb, vb, *, tq, tk, unroll):
    b, t, _ = qb.shape
    rows = 2 * tq
    return pl.pallas_call(
        functools.partial(_mla_kernel, tk=tk, unroll=unroll),
        grid=(b, MLA_HEADS // 2, t // tq),
        in_specs=[pl.BlockSpec((1, tq, 256), lambda i, j, n: (i, n, j)),
                  pl.BlockSpec((1, t, 256), lambda i, j, n: (i, 0, j)),
                  pl.BlockSpec((1, t, 256), lambda i, j, n: (i, 0, j))],
        out_specs=pl.BlockSpec((1, tq, LANES), lambda i, j, n: (i, n, j)),
        out_shape=jax.ShapeDtypeStruct((b, t, B_WIDTH), BF16),
        scratch_shapes=[pltpu.VMEM((rows, LANES), F32), pltpu.VMEM((rows, LANES), F32)],
        compiler_params=_params(("parallel", "parallel", "arbitrary")),
        name="mla_attention",
    )(qb, kb, vb)


A_TQ = 2048
A_SUB = 128
A_BAND = A_SUB + 2 * A_RADIUS
A_CLASSES = 16
A_LOCKSTEP = 8
A_PAD = A_RADIUS


def _class_runs(cfg, sub):
    if cfg == 2:
        return [(pl.multiple_of(sub * A_SUB, A_SUB), A_SUB)]
    if cfg == 1:
        r4, blk = sub & 3, sub >> 2
        return [(pl.multiple_of((r4 + 4 * a) * A_SUB + blk * 32, 32), 32) for a in range(4)]
    return [(pl.multiple_of(r * A_SUB + sub * 8, 8), 8) for r in range(A_CLASSES)]


def _dilated_kernel(q_ref, k_ref, v_ref, b_ref, o_ref,
                    nat_ref, mid_ref, qc_ref, kp1, kp16, vp1, vp16, oacc_ref, m_ref, l_ref):
    seq = k_ref.shape[1]
    units16 = seq // A_CLASSES
    qi = pl.program_id(2)

    def split_classes(n_rows, emit):
        quarter = n_rows // 4
        for r4 in range(4):
            mid_ref[r4 * quarter:(r4 + 1) * quarter] = nat_ref[pl.ds(r4, quarter, stride=4), :]
        for r4 in range(4):
            for a in range(4):
                emit(r4 + 4 * a, mid_ref[pl.ds(r4 * quarter + a, quarter // 4, stride=4), :])

    @pl.when(qi == 0)
    def _():
        zeros = jnp.zeros((A_PAD, LANES), BF16)
        for src, nat_pad, cls_pad in ((k_ref, kp1, kp16), (v_ref, vp1, vp16)):
            nat_pad[0:A_PAD] = zeros
            nat_pad[A_PAD + seq:] = zeros
            nat_pad[A_PAD:A_PAD + seq] = src[0]
            nat_ref[...] = src[0].astype(F32)

            def emit(r, rows, cls_pad=cls_pad):
                cls_pad[r, 0:A_PAD] = zeros
                cls_pad[r, A_PAD + units16:] = zeros
                cls_pad[r, A_PAD:A_PAD + units16] = rows.astype(BF16)

            split_classes(seq, emit)

    nat_ref[0:A_TQ] = q_ref[0].astype(F32)

    def emit_q(r, rows):
        qc_ref[r * A_SUB:(r + 1) * A_SUB] = rows

    split_classes(A_TQ, emit_q)
    low = lax.broadcasted_iota(jnp.int32, (A_SUB, LANES), 1) < A_HEAD_DIM
    col = lax.broadcasted_iota(jnp.int32, (1, A_BAND), 1)
    col_unit = (col, 4 * (col % 64) + col // 64, col)

    for ci, (_, dil) in enumerate(DILATED_CONFIGS):
        units = seq // dil
        per_step = A_TQ // dil

        def body(it, carry, ci=ci, units=units, per_step=per_step):
            subs = [it * A_LOCKSTEP + u for u in range(A_LOCKSTEP)]
            if ci == 2:
                first = [qi * per_step] * len(subs)
            elif ci == 1:
                first = [qi * per_step + (s_ >> 2) * A_SUB for s_ in subs]
            else:
                first = [qi * per_step + s_ * A_SUB for s_ in subs]
            runs = [_class_runs(ci, s_) for s_ in subs]

            def window(nat_pad, cls_pad, sub, f):
                if ci == 0:
                    return nat_pad[pl.ds(pl.multiple_of(f, A_SUB), A_BAND), :]
                if ci == 2:
                    return cls_pad[sub, pl.ds(pl.multiple_of(f, A_SUB), A_BAND), :]
                row = pl.multiple_of((f >> 2) + A_PAD - A_RADIUS // 4, 16)
                return jnp.concatenate([cls_pad[(sub & 3) + 4 * a, pl.ds(row, A_BAND // 4), :]
                                        for a in range(4)], axis=0)

            def gather(ref, sub_runs):
                parts = [ref[pl.ds(start, size), :] for start, size in sub_runs]
                return parts[0] if len(parts) == 1 else jnp.concatenate(parts, axis=0)

            def scatter(ref, sub_runs, val):
                row = 0
                for start, size in sub_runs:
                    ref[pl.ds(start, size), :] = val[row:row + size]
                    row += size

            qs = []
            for sub_runs in runs:
                q = gather(qc_ref, sub_runs)
                qs.append(jnp.concatenate([jnp.where(low, q, 0.0), jnp.where(low, 0.0, q)],
                                          axis=0).astype(BF16))
            kw = [window(kp1, kp16, s_, f) for s_, f in zip(subs, first)]
            vw = [window(vp1, vp16, s_, f) for s_, f in zip(subs, first)]
            s = [lax.dot_general(a, b_, (((1,), (1,)), ((), ())), preferred_element_type=F32)
                 for a, b_ in zip(qs, kw)]
            bias = b_ref[0, ci]
            masked = []
            for s_u, f in zip(s, first):
                unit = f - A_RADIUS + col_unit[ci]
                valid = jnp.where((unit >= 0) & (unit < units), 0.0, NEG_INF)
                masked.append(s_u + bias + valid)
            m_blk = [jnp.max(x, axis=-1, keepdims=True) for x in masked]
            p = [jnp.exp2(x - m) for x, m in zip(masked, m_blk)]
            l_blk = [jnp.sum(x, axis=-1, keepdims=True) for x in p]
            o_blk = [jnp.dot(x.astype(BF16), v, preferred_element_type=F32) for x, v in zip(p, vw)]
            for sub_runs, o_u, m_u, l_u in zip(runs, o_blk, m_blk, l_blk):
                o2 = jnp.where(low, o_u[:A_SUB], o_u[A_SUB:])
                m2 = jnp.where(low, m_u[:A_SUB], m_u[A_SUB:])
                l2 = jnp.where(low, l_u[:A_SUB], l_u[A_SUB:])
                if ci == 0:
                    scatter(oacc_ref, sub_runs, o2)
                    scatter(m_ref, sub_runs, m2)
                    scatter(l_ref, sub_runs, l2)
                else:
                    m_old = gather(m_ref, sub_runs)
                    m_new = jnp.maximum(m_old, m2)
                    a_old = jnp.exp2(m_old - m_new)
                    a_blk = jnp.exp2(m2 - m_new)
                    scatter(oacc_ref, sub_runs, gather(oacc_ref, sub_runs) * a_old + o2 * a_blk)
                    scatter(l_ref, sub_runs, gather(l_ref, sub_runs) * a_old + l2 * a_blk)
                    scatter(m_ref, sub_runs, m_new)
            return carry

        lax.fori_loop(0, A_TQ // A_SUB // A_LOCKSTEP, body, 0)

    quarter = A_TQ // 4
    for r4 in range(4):
        for a in range(4):
            rows = slice((r4 + 4 * a) * A_SUB, (r4 + 4 * a + 1) * A_SUB)
            mid_ref[pl.ds(r4 * quarter + a, A_SUB, stride=4), :] = oacc_ref[rows] / l_ref[rows]
    for r4 in range(4):
        nat_ref[pl.ds(r4, quarter, stride=4), :] = mid_ref[r4 * quarter:(r4 + 1) * quarter]
    o_ref[0] = nat_ref[0:A_TQ].astype(o_ref.dtype)


def dilated_attention(qkv, bias):
    b, t, _ = qkv.shape
    pairs = A_HEADS // 2
    state = pltpu.VMEM((A_TQ, LANES), F32)
    nat_pad = pltpu.VMEM((t + 2 * A_PAD, LANES), BF16)
    cls_pad = pltpu.VMEM((A_CLASSES, t // A_CLASSES + 2 * A_PAD, LANES), BF16)
    return pl.pallas_call(
        _dilated_kernel,
        grid=(b, pairs, t // A_TQ),
        in_specs=[pl.BlockSpec((1, A_TQ, LANES), lambda i, j, n: (i, n, j)),
                  pl.BlockSpec((1, t, LANES), lambda i, j, n: (i, 0, pairs + j)),
                  pl.BlockSpec((1, t, LANES), lambda i, j, n: (i, 0, 2 * pairs + j)),
                  pl.BlockSpec((1, len(DILATED_CONFIGS), 2 * A_SUB, A_BAND), lambda i, j, n: (j, 0, 0, 0))],
        out_specs=pl.BlockSpec((1, A_TQ, LANES), lambda i, j, n: (i, n, j)),
        out_shape=jax.ShapeDtypeStruct((b, t, A_WIDTH), BF16),
        scratch_shapes=[pltpu.VMEM((t, LANES), F32), pltpu.VMEM((t, LANES), F32), state,
                        nat_pad, cls_pad, nat_pad, cls_pad, state, state, state],
        compiler_params=_params(("parallel", "parallel", "arbitrary")),
        name="dilated_attention",
    )(qkv, qkv, qkv, bias)


def _dilated_bias():
    i = jnp.arange(A_SUB, dtype=jnp.int32)[:, None]
    c = jnp.arange(A_BAND, dtype=jnp.int32)[None, :]
    u_rel = {1: 16 * (i % 8) + i // 8, 4: 4 * (i % 32) + i // 32, 16: i}
    k_rel = {1: c, 4: 4 * (c % 64) + c // 64, 16: c}
    slopes = 2.0 ** (-8.0 * jnp.arange(1, A_HEADS + 1, dtype=F32) / A_HEADS)
    per_cfg = []
    for _, dil in DILATED_CONFIGS:
        hops = jnp.abs(k_rel[dil] - A_RADIUS - u_rel[dil])
        dist = (dil * hops).astype(F32)
        bias = jnp.where(hops <= A_RADIUS, -slopes[:, None, None] * dist[None] * LOG2E, NEG_INF)
        per_cfg.append(bias.reshape(A_HEADS // 2, 2 * A_SUB, A_BAND))
    return jnp.stack(per_cfg, axis=1)


def _gate_out_value(parts, gate_ref, x_ref, w_ref, g_ref):
    o = jnp.concatenate([r[...] for r in parts], axis=-1) if len(parts) > 1 else parts[0][...]
    y = None
    for c0 in range(0, o.shape[1], 512):
        gate = gate_ref[:, c0:c0 + 512].astype(F32)
        gated = (o[:, c0:c0 + 512].astype(F32) * (gate * jax.nn.sigmoid(gate))).astype(BF16)
        part = jnp.dot(gated, w_ref[c0:c0 + 512, :], preferred_element_type=F32)
        y = part if y is None else y + part
    return x_ref[...] + _rms(y, g_ref[...])


def _gate_out_kernel(*refs, n_parts):
    gate_ref, x_ref, w_ref, g_ref, o_ref = refs[n_parts:]
    o_ref[...] = _gate_out_value(refs[:n_parts], gate_ref, x_ref, w_ref, g_ref)


def _gate_out_in_c_kernel(*refs, n_parts, q_scale):
    gate_ref, x_ref, w_ref, g_ref = refs[n_parts:n_parts + 4]
    in_c_inputs = refs[n_parts + 4:n_parts + 12]
    x_out_ref = refs[n_parts + 12]
    in_c_outputs = refs[n_parts + 13:]
    x_new = _gate_out_value(refs[:n_parts], gate_ref, x_ref, w_ref, g_ref)
    x_out_ref[...] = x_new
    _in_c_body(x_new, *in_c_inputs, *in_c_outputs, q_scale)


def _gate_out_specs(parts, gate, d, w):
    row = lambda i: (i, 0)
    const = lambda i: (0, 0)
    return ([pl.BlockSpec((ROW_TILE, p.shape[1]), row) for p in parts] +
            [pl.BlockSpec((ROW_TILE, gate.shape[1]), row),
             pl.BlockSpec((ROW_TILE, d), row),
             pl.BlockSpec(w.shape, const),
             pl.BlockSpec((1, d), const)])


def gate_out(parts, gate, x, w, g):
    m, d = x.shape
    return pl.pallas_call(
        functools.partial(_gate_out_kernel, n_parts=len(parts)),
        grid=(m // ROW_TILE,),
        in_specs=_gate_out_specs(parts, gate, d, w),
        out_specs=pl.BlockSpec((ROW_TILE, d), lambda i: (i, 0)),
        out_shape=jax.ShapeDtypeStruct((m, d), F32),
        compiler_params=_params(("parallel",)),
        name="gate_out",
    )(*parts, gate, x, w, g.reshape(1, d))


def gate_out_in_proj_c(parts, gate, x, w_out, g_post, seq, g_pre, w_in, gq, gk, bd, tabs):
    m, d = x.shape
    in_specs, out_specs, out_w = _in_c_specs(d, w_in, seq // ROW_TILE)
    tile4 = lambda v: jnp.tile(v.reshape(1, -1), (1, 4))
    return pl.pallas_call(
        functools.partial(_gate_out_in_c_kernel, n_parts=len(parts), q_scale=C_HEAD_DIM ** -0.5 * LOG2E),
        grid=(m // ROW_TILE,),
        in_specs=_gate_out_specs(parts, gate, d, w_out) + in_specs,
        out_specs=[pl.BlockSpec((ROW_TILE, d), lambda i: (i, 0))] + out_specs,
        out_shape=[jax.ShapeDtypeStruct((m, d), F32)] + [jax.ShapeDtypeStruct((m, n), BF16) for n in out_w],
        compiler_params=_params(("parallel",)),
        name="gate_out_in_proj_c",
    )(*parts, gate, x, w_out, g_post.reshape(1, d), g_pre.reshape(1, d), w_in, tile4(gq), tile4(gk), bd, *tabs)


def _rope_freqs():
    half = MLA_ROPE_DIM // 2
    return ROPE_THETA ** (-jnp.arange(half, dtype=F32) / half)


def _mla_tables(seq):
    ang = jnp.arange(seq, dtype=F32)[:, None] * _rope_freqs()[None, :]
    cos, sin = jnp.cos(ang), jnp.sin(ang)
    zeros = jnp.zeros_like(cos)
    ones_nope = jnp.ones((seq, MLA_NOPE_DIM), F32)
    zeros_nope = jnp.zeros((seq, MLA_NOPE_DIM), F32)
    tail = jnp.zeros((seq, LANES - MLA_NOPE_DIM - MLA_ROPE_DIM), F32)
    c = jnp.concatenate([ones_nope, cos, cos, tail], axis=1)
    sa = jnp.concatenate([zeros_nope, zeros, sin, tail], axis=1)
    sb = jnp.concatenate([zeros_nope, -sin, zeros, tail], axis=1)
    return c, sa, sb


def _axial_tables(seq):
    t = jnp.arange(seq, dtype=jnp.int32)
    f = _rope_freqs()[None, :]
    ar = (t // GRID_W).astype(F32)[:, None] * f
    ac = (t % GRID_W).astype(F32)[:, None] * f
    zeros = jnp.zeros_like(ar)
    c = jnp.concatenate([jnp.cos(ar), jnp.cos(ar), jnp.cos(ac), jnp.cos(ac)], axis=1)
    sa = jnp.concatenate([zeros, jnp.sin(ar), zeros, jnp.sin(ac)], axis=1)
    sb = jnp.concatenate([-jnp.sin(ar), zeros, -jnp.sin(ac), zeros], axis=1)
    return tuple(jnp.tile(x, (1, 2)) for x in (c, sa, sb))


def _head_block_diag():
    idx = np.arange(256) // C_HEAD_DIM
    return jnp.asarray(idx[:, None] == idx[None, :], BF16)


def _arrange_w_in_ab(w):
    qa, ka, va, cq, ckv, kr, gate = jnp.split(
        w, list(np.cumsum([A_WIDTH, A_WIDTH, A_WIDTH, MLA_Q_RANK, MLA_KV_RANK, MLA_ROPE_DIM])), axis=1)
    d = w.shape[0]
    kr_blk = jnp.concatenate([jnp.zeros((d, KR_LANE), F32), kr,
                              jnp.zeros((d, LANES - KR_LANE - MLA_ROPE_DIM), F32)], axis=1)
    qa = qa * (A_HEAD_DIM ** -0.5 * LOG2E)
    return jnp.concatenate([qa, ka, va, cq, ckv, kr_blk, gate], axis=1).astype(BF16)


def _arrange_w_in_c(w):
    d = w.shape[0]
    q, k, v, gate = jnp.split(w, list(np.cumsum([C_WIDTH, C_KV_HEADS * C_HEAD_DIM,
                                                 C_KV_HEADS * C_HEAD_DIM])), axis=1)
    k = k.reshape(d, C_KV_HEADS, 1, C_HEAD_DIM)
    kk = jnp.broadcast_to(k, (d, C_KV_HEADS, 2, C_HEAD_DIM)).reshape(d, C_KV_HEADS * LANES)
    v = v.reshape(d, C_KV_HEADS, C_HEAD_DIM)
    vv = jnp.concatenate([v, jnp.zeros_like(v)], axis=2).reshape(d, C_KV_HEADS * LANES)
    return jnp.concatenate([gate, q, kk, vv], axis=1).astype(BF16)


def _arrange_w_uq(w):
    r = w.shape[0]
    w = w.reshape(r, MLA_HEADS, MLA_NOPE_DIM + MLA_ROPE_DIM)
    pad = jnp.zeros((r, MLA_HEADS, LANES - MLA_NOPE_DIM - MLA_ROPE_DIM), F32)
    return jnp.concatenate([w, pad], axis=2).reshape(r, MLA_WIDE).astype(BF16)


def _arrange_w_ukv(w):
    r = w.shape[0]
    w = w.reshape(r, MLA_HEADS, MLA_NOPE_DIM + MLA_V_DIM)
    pad = jnp.zeros((r, MLA_HEADS, LANES - MLA_NOPE_DIM), F32)
    wk = jnp.concatenate([w[:, :, :MLA_NOPE_DIM], pad], axis=2).reshape(r, MLA_WIDE)
    wv = jnp.concatenate([w[:, :, MLA_NOPE_DIM:], pad], axis=2).reshape(r, MLA_WIDE)
    return wk.astype(BF16), wv.astype(BF16)


def _even_mixers(x, b, t, g_pre, w_in, gq, wuq, gkv, wuk, wuv, bias, tabs):
    qkv, gate, qb, kb, vb = in_proj_ab(x, t, g_pre, w_in, gq, gkv, wuq, wuk, wuv, tabs)
    three = lambda a: a.reshape(b, t, a.shape[1])
    out_a = dilated_attention(three(qkv), bias)
    out_b = mla_attention(three(qb), three(kb), three(vb), tq=1024, tk=2048, unroll=4)
    return [out_a.reshape(b * t, A_WIDTH), out_b.reshape(b * t, B_WIDTH)], gate


def _odd_mixer(b, t, q, kk, vv):
    three = lambda a: a.reshape(b, t, a.shape[1])
    o = gqa_attention(three(q), three(kk), three(vv), tq=512, tk=2048, unroll=4)
    return [o.reshape(b * t, C_WIDTH)]


def _trunk(x3, weights, consts):
    (norm_pre, norm_post, w_in_ab, mla_q_norm, w_uq, mla_kv_norm, w_uk, w_uv, w_out_ab,
     w_in_c, c_q_norm, c_k_norm, w_out_c) = weights
    b, t, d = x3.shape
    x = x3.reshape(b * t, d)
    bias, bd = consts
    mla_tabs = _mla_tables(t)
    axial_tabs = _axial_tables(t)
    depth = norm_pre.shape[0]
    open_layer = None
    for layer in range(depth):
        i = layer // 2
        if layer % 2 == 0:
            if open_layer is not None:
                x = gate_out(open_layer[0], open_layer[1], x, open_layer[2], open_layer[3])
            parts, gate = _even_mixers(x, b, t, norm_pre[layer], w_in_ab[i], mla_q_norm[i], w_uq[i],
                                       mla_kv_norm[i], w_uk[i], w_uv[i], bias, mla_tabs)
            open_layer = (parts, gate, w_out_ab[i], norm_post[layer])
        else:
            odd_in = (t, norm_pre[layer], w_in_c[i], c_q_norm[i], c_k_norm[i], bd, axial_tabs)
            if open_layer is not None:
                x, gate, q, kk, vv = gate_out_in_proj_c(open_layer[0], open_layer[1], x, open_layer[2],
                                                        open_layer[3], *odd_in)
            else:
                gate, q, kk, vv = in_proj_c(x, *odd_in)
            open_layer = (_odd_mixer(b, t, q, kk, vv), gate, w_out_c[i], norm_post[layer])
    x = gate_out(open_layer[0], open_layer[1], x, open_layer[2], open_layer[3])
    return x.reshape(b, t, d)


def kernel(x_prompt, x_sample, norm_pre, norm_post, w_in_ab, mla_q_norm, w_uq, mla_kv_norm, w_ukv,
           w_out_ab, w_in_c, c_q_norm, c_k_norm, w_out_c):
    n_even, n_odd = w_in_ab.shape[0], w_in_c.shape[0]
    w_in_ab_p = jnp.stack([_arrange_w_in_ab(w_in_ab[i]) for i in range(n_even)])
    w_uq_p = jnp.stack([_arrange_w_uq(w_uq[i]) for i in range(n_even)])
    w_ukv_p = [_arrange_w_ukv(w_ukv[i]) for i in range(n_even)]
    w_uk_p = jnp.stack([p[0] for p in w_ukv_p])
    w_uv_p = jnp.stack([p[1] for p in w_ukv_p])
    w_in_c_p = jnp.stack([_arrange_w_in_c(w_in_c[i]) for i in range(n_odd)])
    weights = (norm_pre, norm_post, w_in_ab_p, mla_q_norm, w_uq_p, mla_kv_norm, w_uk_p, w_uv_p,
               w_out_ab.astype(BF16), w_in_c_p, c_q_norm, c_k_norm, w_out_c.astype(BF16))
    consts = (_dilated_bias(), _head_block_diag())
    return (_trunk(x_prompt, weights, consts), _trunk(x_sample, weights, consts))
```
